```python
import math
import jax, jax.numpy as jnp
from jax import lax
import numpy as np

D_MODEL = 1024
BATCH = 4
SEQ = 4096
DEPTH = 4
DEC_BATCH = 128
DEC_SEQ = 1
PAST_LEN = 2048
PAGE_SIZE = 128

N_MIXERS = 3
N_A = (DEPTH + 2) // 3
N_B = (DEPTH + 1) // 3
N_C = DEPTH // 3

HG_HEADS = 8
HG_DK = 128
HG_DV = D_MODEL // HG_HEADS
HG_F = HG_HEADS * HG_DK
HG_CHUNK = 32
HG_IN = 2 * HG_F + 2 * D_MODEL

AT_HEADS = 8
AT_DH = D_MODEL // AT_HEADS
IDX_HEADS = 8
IDX_DIM = 64
TOPK_MAX = 256
Q_BLOCK = 128
REL_BUCKETS = 32
REL_MAX_DIST = 128
DSA_IN = 3 * AT_HEADS * AT_DH + IDX_HEADS * IDX_DIM + IDX_DIM + IDX_HEADS
NEG_BIG = -1e30

S5_GROUP = 16
S5_GROUPS = D_MODEL // S5_GROUP
S5_STATE = 64
S5_DT_MIN = 1e-3
S5_DT_MAX = 1e-1

FFN_HIDDEN = 4 * D_MODEL
DN_ALPHA = (2 * DEPTH) ** 0.25
DN_BETA = (8 * DEPTH) ** -0.25
LN_EPS = 1e-5
RMS_EPS = 1e-6

kernel_name = 'hgrn2_dsa_s5_hybrid_decode_step'


def layer_norm(x, g, b):
    xf = x.astype(jnp.float32)
    mu = jnp.mean(xf, -1, keepdims=True)
    var = jnp.mean(jnp.square(xf - mu), -1, keepdims=True)
    return ((xf - mu) * lax.rsqrt(var + LN_EPS) * g + b).astype(x.dtype)


def sq_relu_mlp(x, w1, w2):
    return jnp.square(jax.nn.relu(x @ w1)) @ w2


def hgrn2_chunk_scan(q, k, v, logf, s0):
    n, l = q.shape[:2]
    c = min(HG_CHUNK, l)
    pad = (-l) % c
    if pad:
        padw = ((0, 0), (0, pad), (0, 0), (0, 0))
        q, k, v, logf = [jnp.pad(a, padw) for a in (q, k, v, logf)]
    nc = (l + pad) // c

    def to_chunks(a):
        return a.reshape(n, nc, c, a.shape[2], a.shape[3]).transpose(1, 0, 3, 2, 4)

    causal = jnp.tril(jnp.ones((c, c), bool))[:, :, None]

    def step(s, inp):
        qc, kc, vc, gc = inp
        b = jnp.cumsum(gc, axis=2)
        diff = b[:, :, :, None, :] - b[:, :, None, :, :]
        decay = jnp.where(causal, jnp.exp(jnp.where(causal, diff, 0.0)), 0.0)
        att = jnp.einsum('nhtsd,nhsd->nhts', qc[:, :, :, None, :] * decay, kc)
        o = jnp.einsum('nhts,nhsv->nhtv', att, vc) + jnp.einsum('nhtd,nhdv->nhtv', qc * jnp.exp(b), s)
        b_last = b[:, :, -1:, :]
        s_new = jnp.exp(b_last[:, :, 0, :])[..., None] * s + jnp.einsum('nhsd,nhsv->nhdv', kc * jnp.exp(b_last - b), vc)
        return s_new, o

    s_fin, o = lax.scan(step, s0, (to_chunks(q), to_chunks(k), to_chunks(v), to_chunks(logf)))
    o = o.transpose(1, 0, 3, 2, 4).reshape(n, nc * c, HG_HEADS, HG_DV)[:, :l]
    return o, s_fin


def hgrn2_mixer(x, s0, w_in, lb, gnorm, w_out):
    f32 = jnp.float32
    n, l, _ = x.shape
    q, fz, vi, g = jnp.split(x @ w_in, [HG_F, 2 * HG_F, 2 * HG_F + D_MODEL], axis=-1)
    q = jax.nn.silu(q.astype(f32)).reshape(n, l, HG_HEADS, HG_DK)
    fz = fz.astype(f32).reshape(n, l, HG_HEADS, HG_DK)
    lb = lb.astype(f32).reshape(HG_HEADS, HG_DK)
    f = lb + (1.0 - lb) * jax.nn.sigmoid(fz)
    logf = jnp.log(f)
    k = (1.0 - lb) * jax.nn.sigmoid(-fz)
    v = vi.astype(f32).reshape(n, l, HG_HEADS, HG_DV)
    o, s_fin = hgrn2_chunk_scan(q, k, v, logf, s0.astype(f32))
    o = o * lax.rsqrt(jnp.mean(jnp.square(o), -1, keepdims=True) + RMS_EPS) * gnorm.astype(f32).reshape(HG_HEADS, HG_DV)
    o = o.reshape(n, l, D_MODEL) * jax.nn.silu(g.astype(f32))
    return o.astype(x.dtype) @ w_out, s_fin


def t5_bucket(rel):
    n = jnp.maximum(rel, 0)
    exact = REL_BUCKETS // 2
    nf = jnp.maximum(n, exact).astype(jnp.float32)
    large = exact + (jnp.log(nf / exact) / math.log(REL_MAX_DIST / exact) * (REL_BUCKETS - exact)).astype(jnp.int32)
    large = jnp.minimum(large, REL_BUCKETS - 1)
    return jnp.where(n < exact, n, large)


def dsa_project(x, w_in, kln_g, kln_b):
    n, l, _ = x.shape
    hd = AT_HEADS * AT_DH
    qd = IDX_HEADS * IDX_DIM
    q, k, v, qi, ki, wi = jnp.split(x @ w_in, [hd, 2 * hd, 3 * hd, 3 * hd + qd, 3 * hd + qd + IDX_DIM], axis=-1)
    q = q.reshape(n, l, AT_HEADS, AT_DH)
    k = k.reshape(n, l, AT_HEADS, AT_DH)
    v = v.reshape(n, l, AT_HEADS, AT_DH)
    qi = qi.reshape(n, l, IDX_HEADS, IDX_DIM) * (IDX_DIM ** -0.5)
    ki = layer_norm(ki, kln_g, kln_b)
    wi = wi * (IDX_HEADS ** -0.5)
    return q, k, v, qi, ki, wi


def dsa_attend(q, qi, wi, qpos, kidx, gather_kv, rel_bias, n_sel):
    l = kidx.shape[0]
    dots = jax.nn.relu(jnp.einsum('qhd,sd->qhs', qi, kidx).astype(jnp.float32))
    score = jnp.einsum('qh,qhs->qs', wi.astype(jnp.float32), dots)
    kpos = jnp.arange(l, dtype=jnp.int32)
    admissible = kpos[None, :] <= qpos[:, None]
    _, sel = lax.top_k(jnp.where(admissible, score, NEG_BIG), n_sel)
    valid = sel <= qpos[:, None]
    k_sel, v_sel = gather_kv(sel)
    bias = rel_bias[t5_bucket(qpos[:, None] - sel)].astype(jnp.float32)
    logits = jnp.einsum('qhd,qkhd->qhk', q, k_sel).astype(jnp.float32) * (AT_DH ** -0.5) + bias.transpose(0, 2, 1)
    logits = jnp.where(valid[:, None, :], logits, NEG_BIG)
    p = jax.nn.softmax(logits, axis=-1)
    return jnp.einsum('qhk,qkhd->qhd', p.astype(v_sel.dtype), v_sel)


def dsa_prompt(x, w_in, kln_g, kln_b, w_out, rel_bias):
    n, l, _ = x.shape
    q, k, v, qi, ki, wi = dsa_project(x, w_in, kln_g, kln_b)
    nb = l // Q_BLOCK
    n_sel = min(TOPK_MAX, l // 4)

    def blockify(a):
        return a.reshape((n * nb, Q_BLOCK) + a.shape[2:])

    b_ids = jnp.repeat(jnp.arange(n, dtype=jnp.int32), nb)
    blk_ids = jnp.tile(jnp.arange(nb, dtype=jnp.int32), n)

    def one_block(args):
        b, blk, qb, qib, wib = args
        qpos = blk * Q_BLOCK + jnp.arange(Q_BLOCK, dtype=jnp.int32)
        gather = lambda sel: (k[b, sel], v[b, sel])
        return dsa_attend(qb, qib, wib, qpos, ki[b], gather, rel_bias, n_sel)

    o = lax.map(one_block, (b_ids, blk_ids, blockify(q), blockify(qi), blockify(wi)))
    return o.reshape(n, l, AT_HEADS * AT_DH) @ w_out, k, v, ki


def dsa_sample(x, j, cache_k, cache_v, cache_kidx, page_table, w_in, kln_g, kln_b, w_out, rel_bias):
    n, l, _ = x.shape
    past = page_table.shape[1] * PAGE_SIZE
    n_sel = min(TOPK_MAX, (past + l) // 4)
    q, k, v, qi, ki, wi = dsa_project(x, w_in, kln_g, kln_b)
    qpos = past + jnp.arange(l, dtype=jnp.int32)

    def one_seq(qs, qis, wis, ks, vs, kis, pt):
        kidx_past = cache_kidx[j, pt].reshape(past, IDX_DIM)
        kidx_all = jnp.concatenate([kidx_past, kis], axis=0)

        def gather(sel):
            in_past = sel < past
            sp = jnp.minimum(sel, past - 1)
            phys = pt[sp // PAGE_SIZE]
            off = sp % PAGE_SIZE
            sn = jnp.clip(sel - past, 0, l - 1)
            m = in_past[..., None, None]
            return (jnp.where(m, cache_k[j, phys, off], ks[sn]), jnp.where(m, cache_v[j, phys, off], vs[sn]))

        return dsa_attend(qs, qis, wis, qpos, kidx_all, gather, rel_bias, n_sel)

    o = jax.vmap(one_seq)(q, qi, wi, k, v, ki, page_table)
    return o.reshape(n, l, AT_HEADS * AT_DH) @ w_out, k, v, ki


def s5_mixer(x, h0_re, h0_im, a_re, a_im, log_dt, b_re, b_im, c_re, c_im, d_skip, w_glu, b_glu):
    f32 = jnp.float32
    n, l, _ = x.shape
    u = x.astype(f32).reshape(n, l, S5_GROUPS, S5_GROUP)
    dt = jnp.exp(log_dt.astype(f32))[:, None]
    ar, ai = a_re.astype(f32), a_im.astype(f32)
    mag = jnp.exp(dt * ar)
    abar_re, abar_im = mag * jnp.cos(dt * ai), mag * jnp.sin(dt * ai)
    den = ar * ar + ai * ai
    nr, ni = abar_re - 1.0, abar_im
    z_re = (nr * ar + ni * ai) / den
    z_im = (ni * ar - nr * ai) / den
    br, bi = b_re.astype(f32), b_im.astype(f32)
    bbar_re = z_re[..., None] * br - z_im[..., None] * bi
    bbar_im = z_re[..., None] * bi + z_im[..., None] * br
    bu_re = jnp.einsum('nlgk,gpk->nlgp', u, bbar_re)
    bu_im = jnp.einsum('nlgk,gpk->nlgp', u, bbar_im)
    h0r, h0i = h0_re.astype(f32), h0_im.astype(f32)
    bu_re = bu_re.at[:, 0].add(abar_re * h0r - abar_im * h0i)
    bu_im = bu_im.at[:, 0].add(abar_re * h0i + abar_im * h0r)
    a_l_re = jnp.broadcast_to(abar_re, bu_re.shape)
    a_l_im = jnp.broadcast_to(abar_im, bu_im.shape)

    def combine(e1, e2):
        a1r, a1i, b1r, b1i = e1
        a2r, a2i, b2r, b2i = e2
        return (a2r * a1r - a2i * a1i, a2r * a1i + a2i * a1r,
                a2r * b1r - a2i * b1i + b2r, a2r * b1i + a2i * b1r + b2i)

    _, _, hr, hi = lax.associative_scan(combine, (a_l_re, a_l_im, bu_re, bu_im), axis=1)
    y = jnp.einsum('nlgp,gkp->nlgk', hr, c_re.astype(f32)) - jnp.einsum('nlgp,gkp->nlgk', hi, c_im.astype(f32))
    y = y.reshape(n, l, D_MODEL) + d_skip.astype(f32) * x.astype(f32)
    g = jax.nn.gelu(y).astype(x.dtype)
    val, gate = jnp.split(g @ w_glu + b_glu, 2, axis=-1)
    out = val * jax.nn.sigmoid(gate.astype(f32))
    return out.astype(x.dtype), hr[:, -1], hi[:, -1]


def setup_inputs(seed: int = 0) -> dict:
    key = jax.random.key(seed)
    ks = iter(jax.random.split(key, 48))
    f32 = jnp.float32
    nrm = lambda shape, s: jax.random.normal(next(ks), shape, f32) * s
    n_pages = PAST_LEN // PAGE_SIZE
    n_used = DEC_BATCH * n_pages
    n_pool = n_used + max(1, n_used // 4)
    page_table = jax.random.permutation(next(ks), n_pool)[:n_used].reshape(DEC_BATCH, n_pages).astype(jnp.int32)
    a_im0 = jnp.pi * jnp.arange(S5_STATE, dtype=f32)
    w_glu = nrm((N_C, D_MODEL, 2 * D_MODEL), D_MODEL ** -0.5)
    w_glu = w_glu * jnp.concatenate([jnp.full((D_MODEL,), DN_BETA, f32), jnp.ones((D_MODEL,), f32)])
    return {
        'x_prompt': nrm((BATCH, SEQ, D_MODEL), 1.0),
        'x_sample': nrm((DEC_BATCH, DEC_SEQ, D_MODEL), 1.0),
        'state_hgrn': nrm((N_A, DEC_BATCH, HG_HEADS, HG_DK, HG_DV), 0.3),
        'cache_k': nrm((N_B, n_pool, PAGE_SIZE, AT_HEADS, AT_DH), 1.0),
        'cache_v': nrm((N_B, n_pool, PAGE_SIZE, AT_HEADS, AT_DH), 1.0),
        'cache_kidx': nrm((N_B, n_pool, PAGE_SIZE, IDX_DIM), 1.0),
        'page_table': page_table,
        'state_s5_re': nrm((N_C, DEC_BATCH, S5_GROUPS, S5_STATE), 0.1),
        'state_s5_im': nrm((N_C, DEC_BATCH, S5_GROUPS, S5_STATE), 0.1),
        'hg_w_in': nrm((N_A, D_MODEL, HG_IN), D_MODEL ** -0.5),
        'hg_lb': nrm((DEPTH, HG_F), 0.1),
        'hg_gnorm': 1.0 + nrm((N_A, D_MODEL), 0.05),
        'hg_w_out': nrm((N_A, D_MODEL, D_MODEL), D_MODEL ** -0.5 * DN_BETA),
        'dsa_w_in': nrm((N_B, D_MODEL, DSA_IN), D_MODEL ** -0.5),
        'dsa_kln_g': 1.0 + nrm((N_B, IDX_DIM), 0.05),
        'dsa_kln_b': nrm((N_B, IDX_DIM), 0.02),
        'dsa_w_out': nrm((N_B, AT_HEADS * AT_DH, D_MODEL), (AT_HEADS * AT_DH) ** -0.5 * DN_BETA),
        'rel_bias': nrm((REL_BUCKETS, AT_HEADS), 0.5),
        's5_a_re': -0.5 + nrm((N_C, S5_GROUPS, S5_STATE), 0.01),
        's5_a_im': a_im0 + nrm((N_C, S5_GROUPS, S5_STATE), 0.01),
        's5_log_dt': jax.random.uniform(next(ks), (N_C, S5_GROUPS), f32, math.log(S5_DT_MIN), math.log(S5_DT_MAX)),
        's5_b_re': nrm((N_C, S5_GROUPS, S5_STATE, S5_GROUP), (2 * S5_GROUP) ** -0.5),
        's5_b_im': nrm((N_C, S5_GROUPS, S5_STATE, S5_GROUP), (2 * S5_GROUP) ** -0.5),
        's5_c_re': nrm((N_C, S5_GROUPS, S5_GROUP, S5_STATE), (2 * S5_STATE) ** -0.5),
        's5_c_im': nrm((N_C, S5_GROUPS, S5_GROUP, S5_STATE), (2 * S5_STATE) ** -0.5),
        's5_d': nrm((N_C, D_MODEL), 1.0),
        's5_w_glu': w_glu,
        's5_b_glu': nrm((N_C, 2 * D_MODEL), 0.02),
        'ln_mix_g': 1.0 + nrm((DEPTH, D_MODEL), 0.05),
        'ln_mix_b': nrm((DEPTH, D_MODEL), 0.02),
        'mlp_w1': nrm((DEPTH, D_MODEL, FFN_HIDDEN), D_MODEL ** -0.5),
        'mlp_w2': nrm((DEPTH, FFN_HIDDEN, D_MODEL), FFN_HIDDEN ** -0.5 * DN_BETA),
        'ln_ffn_g': 1.0 + nrm((DEPTH, D_MODEL), 0.05),
        'ln_ffn_b': nrm((DEPTH, D_MODEL), 0.02),
    }


def reference(x_prompt, x_sample, state_hgrn, cache_k, cache_v, cache_kidx, page_table, state_s5_re, state_s5_im,
              hg_w_in, hg_lb, hg_gnorm, hg_w_out, dsa_w_in, dsa_kln_g, dsa_kln_b, dsa_w_out, rel_bias,
              s5_a_re, s5_a_im, s5_log_dt, s5_b_re, s5_b_im, s5_c_re, s5_c_im, s5_d, s5_w_glu, s5_b_glu,
              ln_mix_g, ln_mix_b, mlp_w1, mlp_w2, ln_ffn_g, ln_ffn_b):
    f32 = jnp.float32
    lb_sm = jax.nn.softmax(hg_lb.astype(f32), axis=0)
    lower_bounds = jnp.cumsum(lb_sm, axis=0) - lb_sm[0]
    xp, xs = x_prompt, x_sample
    n_p = xp.shape[0]
    hg_p, hg_s = [], []
    k_p, v_p, ki_p, k_s, v_s, ki_s = [], [], [], [], [], []
    s5r_p, s5i_p, s5r_s, s5i_s = [], [], [], []
    for i in range(DEPTH):
        j = i // N_MIXERS
        kind = i % N_MIXERS
        if kind == 0:
            zero = jnp.zeros((n_p, HG_HEADS, HG_DK, HG_DV), f32)
            mp, st_p = hgrn2_mixer(xp, zero, hg_w_in[j], lower_bounds[i], hg_gnorm[j], hg_w_out[j])
            ms, st_s = hgrn2_mixer(xs, state_hgrn[j], hg_w_in[j], lower_bounds[i], hg_gnorm[j], hg_w_out[j])
            hg_p.append(st_p)
            hg_s.append(st_s)
        elif kind == 1:
            mp, kk, vv, kki = dsa_prompt(xp, dsa_w_in[j], dsa_kln_g[j], dsa_kln_b[j], dsa_w_out[j], rel_bias)
            ms, kk2, vv2, kki2 = dsa_sample(xs, j, cache_k, cache_v, cache_kidx, page_table,
                                           dsa_w_in[j], dsa_kln_g[j], dsa_kln_b[j], dsa_w_out[j], rel_bias)
            k_p.append(kk)
            v_p.append(vv)
            ki_p.append(kki)
            k_s.append(kk2)
            v_s.append(vv2)
            ki_s.append(kki2)
        else:
            zr = jnp.zeros((n_p, S5_GROUPS, S5_STATE), f32)
            s5w = (s5_a_re[j], s5_a_im[j], s5_log_dt[j], s5_b_re[j], s5_b_im[j], s5_c_re[j], s5_c_im[j],
                   s5_d[j], s5_w_glu[j], s5_b_glu[j])
            mp, hr, hi = s5_mixer(xp, zr, zr, *s5w)
            ms, hr2, hi2 = s5_mixer(xs, state_s5_re[j], state_s5_im[j], *s5w)
            s5r_p.append(hr)
            s5i_p.append(hi)
            s5r_s.append(hr2)
            s5i_s.append(hi2)
        xp = layer_norm(DN_ALPHA * xp + mp, ln_mix_g[i], ln_mix_b[i])
        xs = layer_norm(DN_ALPHA * xs + ms, ln_mix_g[i], ln_mix_b[i])
        xp = layer_norm(DN_ALPHA * xp + sq_relu_mlp(xp, mlp_w1[i], mlp_w2[i]), ln_ffn_g[i], ln_ffn_b[i])
        xs = layer_norm(DN_ALPHA * xs + sq_relu_mlp(xs, mlp_w1[i], mlp_w2[i]), ln_ffn_g[i], ln_ffn_b[i])
    return (xp, xs, jnp.stack(hg_p), jnp.stack(hg_s), jnp.stack(k_p), jnp.stack(v_p), jnp.stack(ki_p),
            jnp.stack(k_s), jnp.stack(v_s), jnp.stack(ki_s), jnp.stack(s5r_p), jnp.stack(s5i_p),
            jnp.stack(s5r_s), jnp.stack(s5i_s))
```

```python
import functools
import math

import numpy as np
import jax
import jax.numpy as jnp
from jax import lax
from jax.experimental import pallas as pl
from jax.experimental.pallas import tpu as pltpu

F32 = jnp.float32
BF16 = jnp.bfloat16
I32 = jnp.int32

LN_EPS = 1e-5
RMS_EPS = 1e-6
NEG_BIG = -1e30
N_MIXERS = 3
HG_HEADS = 8
HG_DK = 128
AT_HEADS = 8
AT_DH = 128
IDX_HEADS = 8
IDX_DIM = 64
TOPK_MAX = 256
REL_BUCKETS = 32
REL_MAX_DIST = 128
S5_GROUP = 16
S5_STATE = 64
PAGE = 128

LANES = 128
SUBLANES = 8
VMEM_LIMIT = 56 * 1024 * 1024

NT = (((1,), (1,)), ((), ()))
TN = (((0,), (0,)), ((), ()))


def _params(*sem):
    return pltpu.CompilerParams(dimension_semantics=sem, vmem_limit_bytes=VMEM_LIMIT)


def _const(shape):
    nd = len(shape)
    return pl.BlockSpec(shape, lambda *_: (0,) * nd, pipeline_mode=pl.Buffered(1))


def _ln(z, g, b):
    mu = jnp.mean(z, axis=-1, keepdims=True)
    zc = z - mu
    var = jnp.mean(zc * zc, axis=-1, keepdims=True)
    return zc * lax.rsqrt(var + LN_EPS) * g + b


def _sigmoid(z):
    return 1.0 / (1.0 + jnp.exp(-z))


def _bdot(a, b):
    return jnp.dot(a, b, preferred_element_type=F32)


def _mlp_kernel(x_ref, w1_ref, w2_ref, g_ref, b_ref, o_ref, *, alpha, hc):
    x = x_ref[...]
    xb = x.astype(BF16)
    acc = jnp.zeros(x.shape, F32)
    for c in range(w1_ref.shape[1] // hc):
        h = _bdot(xb, w1_ref[:, c * hc:(c + 1) * hc])
        h = jnp.maximum(h, 0.0)
        acc = acc + _bdot((h * h).astype(BF16), w2_ref[c * hc:(c + 1) * hc, :])
    o_ref[...] = _ln(alpha * x + acc, g_ref[...], b_ref[...])


def mlp_res_ln(x, w1, w2, g, b, alpha, tm):
    t, d = x.shape
    f = w1.shape[1]
    return pl.pallas_call(
        functools.partial(_mlp_kernel, alpha=alpha, hc=1024),
        grid=(t // tm,),
        in_specs=[pl.BlockSpec((tm, d), lambda i: (i, 0)), _const((d, f)), _const((f, d)),
                  _const((1, d)), _const((1, d))],
        out_specs=pl.BlockSpec((tm, d), lambda i: (i, 0)),
        out_shape=jax.ShapeDtypeStruct((t, d), F32),
        compiler_params=_params("parallel"),
    )(x, w1, w2, g, b)


def _mm_kernel(x_ref, w_ref, o_ref):
    o_ref[...] = _bdot(x_ref[...].astype(BF16), w_ref[...])


def mm(x, w, tm):
    t, d = x.shape
    n = w.shape[1]
    return pl.pallas_call(
        _mm_kernel,
        grid=(t // tm,),
        in_specs=[pl.BlockSpec((tm, d), lambda i: (i, 0)), _const((d, n))],
        out_specs=pl.BlockSpec((tm, n), lambda i: (i, 0)),
        out_shape=jax.ShapeDtypeStruct((t, n), F32),
        compiler_params=_params("parallel"),
    )(x, w)


def _mm_res_ln_kernel(a_ref, w_ref, x_ref, g_ref, b_ref, o_ref, *, alpha):
    y = _bdot(a_ref[...].astype(BF16), w_ref[...])
    o_ref[...] = _ln(alpha * x_ref[...] + y, g_ref[...], b_ref[...])


def mm_res_ln(a, w, x, g, b, alpha, tm):
    t, k = a.shape
    d = w.shape[1]
    return pl.pallas_call(
        functools.partial(_mm_res_ln_kernel, alpha=alpha),
        grid=(t // tm,),
        in_specs=[pl.BlockSpec((tm, k), lambda i: (i, 0)), _const((k, d)),
                  pl.BlockSpec((tm, d), lambda i: (i, 0)), _const((1, d)), _const((1, d))],
        out_specs=pl.BlockSpec((tm, d), lambda i: (i, 0)),
        out_shape=jax.ShapeDtypeStruct((t, d), F32),
        compiler_params=_params("parallel"),
    )(a, w, x, g, b)


HG_C = 128
HG_LEVELS = 7


def _hgrn_constants():
    c = HG_C
    t = np.arange(c)[:, None]
    u = np.arange(c)[None, :]
    mats = [(u <= t), (u > t)]
    for l in range(HG_LEVELS):
        blk = 2 << l
        mid = (t // blk) * blk + blk // 2
        second = t >= mid
        mats.append(np.where(second, (u >= mid) & (u <= t), (u > t) & (u < mid)))
    m = np.concatenate(mats, axis=0).astype(np.float32)
    x = t ^ u
    lv = np.where(u < t, np.floor(np.log2(np.maximum(x, 1))).astype(np.int32),
                  np.where(u == t, HG_LEVELS, -1)).astype(np.int32)
    return np.concatenate([m, m, m], axis=1), lv


def _lower_bound(lb_ref, layer):
    rows = [lb_ref[i] for i in range(lb_ref.shape[0])]
    mx = rows[0]
    for r in rows[1:]:
        mx = jnp.maximum(mx, r)
    ex = [jnp.exp(r - mx) for r in rows]
    tot = ex[0]
    for e in ex[1:]:
        tot = tot + e
    num = jnp.zeros_like(tot)
    for i in range(1, layer + 1):
        num = num + ex[i]
    return num / tot


def _split3(x):
    hi = x.astype(BF16)
    r1 = x - hi.astype(F32)
    mid = r1.astype(BF16)
    lo = (r1 - mid.astype(F32)).astype(BF16)
    return hi, mid, lo


def _hgrn_prompt_kernel(x_ref, win_ref, lb_ref, gn_ref, wout_ref, g_ref, b_ref, mst_ref, lv_ref,
                        o_ref, st_ref, og_scr, *, alpha, layer):
    c = HG_C
    dk = HG_DK

    @pl.when(pl.program_id(1) == 0)
    def _():
        st_ref[...] = jnp.zeros(st_ref.shape, F32)

    x = x_ref[0]
    d = x.shape[1]
    proj = _bdot(x.astype(BF16), win_ref[...])
    qp = proj[:, 0:d]
    fz = proj[:, d:2 * d]
    v = proj[:, 2 * d:3 * d]
    gp = proj[:, 3 * d:4 * d]
    lb = _lower_bound(lb_ref, layer)
    q = qp * _sigmoid(qp)
    sg = _sigmoid(fz)
    f = lb + (1.0 - lb) * sg
    k = (1.0 - lb) * (1.0 - sg)
    logf = jnp.log(f)
    gate = gp * _sigmoid(gp)
    hi, mid, lo = _split3(logf)
    ex = _bdot(mst_ref[...], jnp.concatenate([hi, mid, lo], axis=0))

    lv = lv_ref[...]
    row = lax.broadcasted_iota(I32, (c, dk), 0)
    for h in range(HG_HEADS):
        hs = slice(h * dk, (h + 1) * dk)
        qh = q[:, hs]
        kh = k[:, hs]
        vh = v[:, hs].astype(BF16)
        st = st_ref[0, h]
        eb = jnp.exp(ex[0:c, hs])
        o = lax.dot_general((qh * eb).astype(BF16), st.astype(BF16), NT, preferred_element_type=F32)
        att = jnp.zeros((c, c), F32)
        for l in range(HG_LEVELS):
            e = jnp.exp(ex[(2 + l) * c:(3 + l) * c, hs])
            second = ((row >> l) & 1) == 1
            w = (jnp.where(second, qh, kh) * e).astype(BF16)
            a = lax.dot_general(w, w, NT, preferred_element_type=F32)
            att = jnp.where(lv == l, a, att)
        a0 = lax.dot_general(qh.astype(BF16), kh.astype(BF16), NT, preferred_element_type=F32)
        att = jnp.where(lv == HG_LEVELS, a0, att)
        o = o + _bdot(att.astype(BF16), vh)
        kd = (kh * jnp.exp(ex[c:2 * c, hs])).astype(BF16)
        st_ref[0, h] = eb[c - 1:c, :] * st + lax.dot_general(vh, kd, TN, preferred_element_type=F32)
        ms = jnp.mean(o * o, axis=-1, keepdims=True)
        og_scr[:, hs] = o * lax.rsqrt(ms + RMS_EPS) * gn_ref[:, hs] * gate[:, hs]
    y = _bdot(og_scr[...].astype(BF16), wout_ref[...])
    o_ref[0] = _ln(alpha * x + y, g_ref[...], b_ref[...])


def hgrn_prompt(x, w_in, hg_lb, gnorm, w_out, g, b, alpha, layer):
    n, l, d = x.shape
    mst, lv = _hgrn_constants()
    nl = hg_lb.shape[0]
    out, st = pl.pallas_call(
        functools.partial(_hgrn_prompt_kernel, alpha=alpha, layer=layer),
        grid=(n, l // HG_C),
        in_specs=[pl.BlockSpec((1, HG_C, d), lambda s, c: (s, c, 0)),
                  _const((d, 4 * d)), _const((nl, 1, d)), _const((1, d)), _const((d, d)),
                  _const((1, d)), _const((1, d)), _const(mst.shape), _const(lv.shape)],
        out_specs=[pl.BlockSpec((1, HG_C, d), lambda s, c: (s, c, 0)),
                   pl.BlockSpec((1, HG_HEADS, HG_DK, HG_DK), lambda s, c: (s, 0, 0, 0))],
        out_shape=[jax.ShapeDtypeStruct((n, l, d), F32),
                   jax.ShapeDtypeStruct((n, HG_HEADS, HG_DK, HG_DK), F32)],
        scratch_shapes=[pltpu.VMEM((HG_C, d), F32)],
        compiler_params=_params("parallel", "arbitrary"),
    )(x, w_in, hg_lb.reshape(nl, 1, d), gnorm, w_out, g, b,
      jnp.asarray(mst, BF16), jnp.asarray(lv))
    return out, jnp.swapaxes(st, 2, 3)


def _hgrn_sample_kernel(qt_ref, fzt_ref, lbt_ref, v_ref, gp_ref, gn_ref, s0_ref, o_ref, s1_ref, *, layer):
    dk = HG_DK
    lbt = _lower_bound(lbt_ref, layer)
    qt = qt_ref[0]
    q = qt * _sigmoid(qt)
    sg = _sigmoid(fzt_ref[0])
    f = lbt + (1.0 - lbt) * sg
    k = (1.0 - lbt) * (1.0 - sg)
    gp = gp_ref[0]
    gate = gp * _sigmoid(gp)
    for h in range(HG_HEADS):
        hs = slice(h * dk, (h + 1) * dk)
        s_new = f[:, h:h + 1] * s0_ref[0, h] + k[:, h:h + 1] * v_ref[0, :, hs]
        s1_ref[0, h] = s_new
        o = jnp.sum(s_new * q[:, h:h + 1], axis=0, keepdims=True)
        ms = jnp.mean(o * o, axis=-1, keepdims=True)
        o_ref[0, :, hs] = o * lax.rsqrt(ms + RMS_EPS) * gn_ref[:, hs] * gate[:, hs]


def hgrn_sample(x, s0, w_in, hg_lb, gnorm, w_out, g, b, alpha, layer):
    n, d = x.shape
    nl = hg_lb.shape[0]
    proj = mm(x, w_in, n)
    to_cols = lambda a: a.reshape(n, HG_HEADS, HG_DK).transpose(0, 2, 1)
    lbt = hg_lb.reshape(nl, HG_HEADS, HG_DK).transpose(0, 2, 1)
    row3 = lambda a: a.reshape(n, 1, d)
    o, s1 = pl.pallas_call(
        functools.partial(_hgrn_sample_kernel, layer=layer),
        grid=(n,),
        in_specs=[pl.BlockSpec((1, HG_DK, HG_HEADS), lambda i: (i, 0, 0)),
                  pl.BlockSpec((1, HG_DK, HG_HEADS), lambda i: (i, 0, 0)),
                  _const((nl, HG_DK, HG_HEADS)),
                  pl.BlockSpec((1, 1, d), lambda i: (i, 0, 0)),
                  pl.BlockSpec((1, 1, d), lambda i: (i, 0, 0)),
                  _const((1, d)),
                  pl.BlockSpec((1, HG_HEADS, HG_DK, HG_DK), lambda i: (i, 0, 0, 0))],
        out_specs=[pl.BlockSpec((1, 1, d), lambda i: (i, 0, 0)),
                   pl.BlockSpec((1, HG_HEADS, HG_DK, HG_DK), lambda i: (i, 0, 0, 0))],
        out_shape=[jax.ShapeDtypeStruct((n, 1, d), F32),
                   jax.ShapeDtypeStruct(s0.shape, F32)],
        compiler_params=_params("parallel"),
    )(to_cols(proj[:, 0:d]), to_cols(proj[:, d:2 * d]), lbt, row3(proj[:, 2 * d:3 * d]),
      row3(proj[:, 3 * d:4 * d]), gnorm, s0)
    return mm_res_ln(o.reshape(n, d), w_out, x, g, b, alpha, n), s1


S5_TC = 128
S5_SLABS = 8
S5_KS = 3


def _cmul(ar, ai, br, bi):
    return ar * br - ai * bi, ar * bi + ai * br


def _s5_prep_kernel(are_ref, aim_ref, ldt_ref, are2_ref, aim2_ref, ldt2_ref, bre_ref, bim_ref,
                    pwre_ref, pwim_ref, ckre_ref, ckim_ref, bbre_ref, bbim_ref):
    dt = jnp.exp(ldt_ref[...])
    mag = jnp.exp(dt * are_ref[...])
    a_r = mag * jnp.cos(dt * aim_ref[...])
    a_i = mag * jnp.sin(dt * aim_ref[...])
    pr, pi_ = [a_r], [a_i]
    for _ in range(SUBLANES - 1):
        nr, ni = _cmul(pr[-1], pi_[-1], a_r, a_i)
        pr.append(nr)
        pi_.append(ni)
    pwre_ref[...] = jnp.concatenate(pr, axis=0)
    pwim_ref[...] = jnp.concatenate(pi_, axis=0)
    rows = lax.broadcasted_iota(I32, pwre_ref.shape, 0)
    for k in range(S5_KS):
        s = 1 << k
        ckre_ref[k] = jnp.where(rows >= s, pr[s - 1], 0.0)
        ckim_ref[k] = jnp.where(rows >= s, pi_[s - 1], 0.0)
    ar, ai = are2_ref[...], aim2_ref[...]
    dt2 = jnp.exp(ldt2_ref[...])
    mag2 = jnp.exp(dt2 * ar)
    nr = mag2 * jnp.cos(dt2 * ai) - 1.0
    ni = mag2 * jnp.sin(dt2 * ai)
    den = ar * ar + ai * ai
    z_re = ((nr * ar + ni * ai) / den)[:, None, :]
    z_im = ((ni * ar - nr * ai) / den)[:, None, :]
    br, bi = bre_ref[...], bim_ref[...]
    bbre_ref[...] = z_re * br - z_im * bi
    bbim_ref[...] = z_re * bi + z_im * br


def s5_prep(a_re, a_im, log_dt, b_re, b_im):
    g, p = a_re.shape
    k = b_re.shape[2]
    n = g * p
    ldt2 = jnp.broadcast_to(log_dt[:, None], (g, p))
    flat = lambda a: a.reshape(1, n)
    return pl.pallas_call(
        _s5_prep_kernel,
        out_shape=[jax.ShapeDtypeStruct((SUBLANES, n), F32), jax.ShapeDtypeStruct((SUBLANES, n), F32),
                   jax.ShapeDtypeStruct((S5_KS, SUBLANES, n), F32), jax.ShapeDtypeStruct((S5_KS, SUBLANES, n), F32),
                   jax.ShapeDtypeStruct((g, k, p), F32), jax.ShapeDtypeStruct((g, k, p), F32)],
    )(flat(a_re), flat(a_im), flat(ldt2), a_re, a_im, ldt2,
      b_re.transpose(0, 2, 1), b_im.transpose(0, 2, 1))


def _block_diag_slabs(w_gkp, rows_are_inputs):
    g, k, p = w_gkp.shape
    per = g // S5_SLABS
    w4 = w_gkp.reshape(S5_SLABS, per, k, p)
    eye = jnp.eye(per, dtype=w_gkp.dtype)
    if rows_are_inputs:
        return jnp.einsum('iakp,ab->iakbp', w4, eye).reshape(S5_SLABS, per * k, per * p)
    return jnp.einsum('iakp,ab->ibpak', w4, eye).reshape(S5_SLABS, per * p, per * k)


def _s5_tail(x, y, d_ref, wglu_ref, bglu_ref, g_ref, b_ref, alpha):
    d = x.shape[1]
    gl = jax.nn.gelu(y + d_ref[...] * x)
    u = _bdot(gl.astype(BF16), wglu_ref[...]) + bglu_ref[...]
    out = u[:, 0:d] * _sigmoid(u[:, d:2 * d])
    return _ln(alpha * x + out, g_ref[...], b_ref[...])


def _s5_prompt_kernel(x_ref, wbre_ref, wbim_ref, wcre_ref, wcim_ref, pwre_ref, pwim_ref, ckre_ref, ckim_ref,
                      d_ref, wglu_ref, bglu_ref, g_ref, b_ref,
                      o_ref, hre_ref, him_ref, cre_scr, cim_scr, y_scr, *, alpha):
    tc = x_ref.shape[1]
    ng = tc // SUBLANES
    ci = pl.program_id(1)

    @pl.when(ci == 0)
    def _():
        cre_scr[...] = jnp.zeros(cre_scr.shape, F32)
        cim_scr[...] = jnp.zeros(cim_scr.shape, F32)

    x = x_ref[0]
    xb = x.astype(BF16)
    sw = wbre_ref.shape[2]
    for i in range(S5_SLABS):
        cs = slice(i * sw, (i + 1) * sw)
        xi = xb[:, i * LANES:(i + 1) * LANES]
        br = _bdot(xi, wbre_ref[i]).reshape(ng, SUBLANES, sw)
        bi = _bdot(xi, wbim_ref[i]).reshape(ng, SUBLANES, sw)
        for k in range(S5_KS):
            sr = pltpu.roll(br, 1 << k, axis=1)
            si = pltpu.roll(bi, 1 << k, axis=1)
            tr, ti = _cmul(ckre_ref[k, :, cs], ckim_ref[k, :, cs], sr, si)
            br, bi = br + tr, bi + ti
        pr, pi_ = pwre_ref[:, cs], pwim_ref[:, cs]
        hpr, hpi = cre_scr[:, cs], cim_scr[:, cs]
        hr_all, hi_all = [], []
        for m in range(ng):
            tr, ti = _cmul(pr, pi_, hpr, hpi)
            hr, hi = br[m] + tr, bi[m] + ti
            hr_all.append(hr)
            hi_all.append(hi)
            hpr = jnp.broadcast_to(hr[SUBLANES - 1:SUBLANES, :], hr.shape)
            hpi = jnp.broadcast_to(hi[SUBLANES - 1:SUBLANES, :], hi.shape)
        cre_scr[:, cs] = hpr
        cim_scr[:, cs] = hpi
        hr = jnp.concatenate(hr_all, axis=0).astype(BF16)
        hi = jnp.concatenate(hi_all, axis=0).astype(BF16)
        y_scr[:, i * LANES:(i + 1) * LANES] = _bdot(hr, wcre_ref[i]) - _bdot(hi, wcim_ref[i])
    o_ref[0] = _s5_tail(x, y_scr[...], d_ref, wglu_ref, bglu_ref, g_ref, b_ref, alpha)

    @pl.when(ci == pl.num_programs(1) - 1)
    def _():
        hre_ref[0] = cre_scr[0:1, :]
        him_ref[0] = cim_scr[0:1, :]


def _s5_weight_specs(wb, wc, pw, ck, d, wglu):
    return [_const(wb.shape), _const(wb.shape), _const(wc.shape), _const(wc.shape),
            _const(pw.shape), _const(pw.shape), _const(ck.shape), _const(ck.shape),
            _const((1, d)), _const(wglu.shape), _const((1, wglu.shape[1])), _const((1, d)), _const((1, d))]


def s5_prompt(x, prm, g, b, alpha):
    n, l, d = x.shape
    wb, wc, pw, ck, wglu = prm[0], prm[2], prm[4], prm[6], prm[9]
    ns = pw.shape[1]
    out, hre, him = pl.pallas_call(
        functools.partial(_s5_prompt_kernel, alpha=alpha),
        grid=(n, l // S5_TC),
        in_specs=[pl.BlockSpec((1, S5_TC, d), lambda s, c: (s, c, 0))] + _s5_weight_specs(wb, wc, pw, ck, d, wglu),
        out_specs=[pl.BlockSpec((1, S5_TC, d), lambda s, c: (s, c, 0)),
                   pl.BlockSpec((1, 1, ns), lambda s, c: (s, 0, 0)),
                   pl.BlockSpec((1, 1, ns), lambda s, c: (s, 0, 0))],
        out_shape=[jax.ShapeDtypeStruct((n, l, d), F32),
                   jax.ShapeDtypeStruct((n, 1, ns), F32), jax.ShapeDtypeStruct((n, 1, ns), F32)],
        scratch_shapes=[pltpu.VMEM((SUBLANES, ns), F32), pltpu.VMEM((SUBLANES, ns), F32),
                        pltpu.VMEM((S5_TC, d), F32)],
        compiler_params=_params("parallel", "arbitrary"),
    )(x, *prm, g, b)
    return out, hre, him


def _s5_sample_kernel(x_ref, h0re_ref, h0im_ref, wbre_ref, wbim_ref, wcre_ref, wcim_ref, pwre_ref, pwim_ref,
                      ckre_ref, ckim_ref, d_ref, wglu_ref, bglu_ref, g_ref, b_ref,
                      o_ref, hre_ref, him_ref, y_scr, *, alpha):
    x = x_ref[...]
    xb = x.astype(BF16)
    sw = wbre_ref.shape[2]
    for i in range(S5_SLABS):
        cs = slice(i * sw, (i + 1) * sw)
        xi = xb[:, i * LANES:(i + 1) * LANES]
        tr, ti = _cmul(pwre_ref[0:1, cs], pwim_ref[0:1, cs], h0re_ref[:, cs], h0im_ref[:, cs])
        hr = _bdot(xi, wbre_ref[i]) + tr
        hi = _bdot(xi, wbim_ref[i]) + ti
        hre_ref[:, cs] = hr
        him_ref[:, cs] = hi
        y_scr[:, i * LANES:(i + 1) * LANES] = (_bdot(hr.astype(BF16), wcre_ref[i])
                                               - _bdot(hi.astype(BF16), wcim_ref[i]))
    o_ref[...] = _s5_tail(x, y_scr[...], d_ref, wglu_ref, bglu_ref, g_ref, b_ref, alpha)


def s5_sample(x, h0re, h0im, prm, g, b, alpha):
    n, d = x.shape
    ns = h0re.shape[1]
    return pl.pallas_call(
        functools.partial(_s5_sample_kernel, alpha=alpha),
        out_shape=[jax.ShapeDtypeStruct((n, d), F32),
                   jax.ShapeDtypeStruct((n, ns), F32), jax.ShapeDtypeStruct((n, ns), F32)],
        scratch_shapes=[pltpu.VMEM((n, d), F32)],
        compiler_params=pltpu.CompilerParams(vmem_limit_bytes=VMEM_LIMIT),
    )(x, h0re, h0im, *prm, g, b)


def s5_params(a_re, a_im, log_dt, b_re, b_im, c_re, c_im, d_skip, w_glu, b_glu):
    pwre, pwim, ckre, ckim, bbre, bbim = s5_prep(a_re, a_im, log_dt, b_re, b_im)
    return (_block_diag_slabs(bbre, True).astype(BF16), _block_diag_slabs(bbim, True).astype(BF16),
            _block_diag_slabs(c_re, False).astype(BF16), _block_diag_slabs(c_im, False).astype(BF16),
            pwre, pwim, ckre, ckim, d_skip.reshape(1, -1), w_glu.astype(BF16), b_glu.reshape(1, -1))


DSA_TQ = 128
DSA_CK = 512
INT_MIN = -2 ** 31
INT_MAX = 2 ** 31 - 1


def _dsa_proj_kernel(x_ref, wq_ref, wk_ref, wv_ref, wqi_ref, wki_ref, wwi_ref, kg_ref, kb_ref,
                     qb_ref, k_ref, v_ref, kbf_ref, vbf_ref, qi_ref, kiln_ref, kibf_ref, wi_ref):
    xb = x_ref[...].astype(BF16)
    qb_ref[...] = (_bdot(xb, wq_ref[...]) * (AT_DH ** -0.5)).astype(BF16)
    k = _bdot(xb, wk_ref[...])
    v = _bdot(xb, wv_ref[...])
    k_ref[...] = k
    v_ref[...] = v
    kbf_ref[...] = k.astype(BF16)
    vbf_ref[...] = v.astype(BF16)
    qi_ref[...] = (_bdot(xb, wqi_ref[...]) * (IDX_DIM ** -0.5)).astype(BF16)
    ki = _ln(_bdot(xb, wki_ref[...])[:, 0:IDX_DIM], kg_ref[...], kb_ref[...])
    kiln_ref[...] = ki
    kibf_ref[...] = ki.astype(BF16)
    wi_ref[...] = _bdot(xb, wwi_ref[...]) * (IDX_HEADS ** -0.5)


def dsa_proj(x, w_in, kln_g, kln_b, tm):
    t, d = x.shape
    hd = AT_HEADS * AT_DH
    qd = IDX_HEADS * IDX_DIM
    wb = w_in.astype(BF16)
    pad = lambda w: jnp.pad(w, ((0, 0), (0, LANES - w.shape[1])))
    ws = [wb[:, 0:hd], wb[:, hd:2 * hd], wb[:, 2 * hd:3 * hd], wb[:, 3 * hd:3 * hd + qd],
          pad(wb[:, 3 * hd + qd:3 * hd + qd + IDX_DIM]), pad(wb[:, 3 * hd + qd + IDX_DIM:])]
    row = lambda n, dt: (pl.BlockSpec((tm, n), lambda i: (i, 0)), jax.ShapeDtypeStruct((t, n), dt))
    outs = [row(hd, BF16), row(hd, F32), row(hd, F32), row(hd, BF16), row(hd, BF16), row(qd, BF16),
            row(IDX_DIM, F32), row(IDX_DIM, BF16), row(LANES, F32)]
    return pl.pallas_call(
        _dsa_proj_kernel,
        grid=(t // tm,),
        in_specs=[pl.BlockSpec((tm, d), lambda i: (i, 0))] + [_const(w.shape) for w in ws]
                 + [_const((1, IDX_DIM)), _const((1, IDX_DIM))],
        out_specs=[o[0] for o in outs],
        out_shape=[o[1] for o in outs],
        compiler_params=_params("parallel"),
    )(x, *ws, kln_g.reshape(1, IDX_DIM), kln_b.reshape(1, IDX_DIM))


def _t5_bucket(rel):
    n = jnp.maximum(rel, 0)
    exact = REL_BUCKETS // 2
    nf = jnp.maximum(n, exact).astype(F32)
    large = exact + (jnp.log(nf / exact) / math.log(REL_MAX_DIST / exact) * (REL_BUCKETS - exact)).astype(I32)
    large = jnp.minimum(large, REL_BUCKETS - 1)
    return jnp.where(n < exact, n, large)


def _bias_lookup(bucket, rb_ref, h):
    out = jnp.zeros(bucket.shape, F32)
    for bk in range(REL_BUCKETS):
        out = jnp.where(bucket == bk, rb_ref[bk, h], out)
    return out


def _dsa_bias_kernel(rb_ref, tile_ref, far_ref, samp_ref, *, past):
    i = lax.broadcasted_iota(I32, tile_ref.shape[1:], 0)
    c = lax.broadcasted_iota(I32, tile_ref.shape[1:], 1)
    bt = _t5_bucket(i - c + DSA_TQ)
    bs = _t5_bucket(past - lax.broadcasted_iota(I32, (1, samp_ref.shape[1]), 1))
    for h in range(AT_HEADS):
        tile_ref[h] = _bias_lookup(bt, rb_ref, h)
        far_ref[h] = jnp.full(far_ref.shape[1:], rb_ref[REL_BUCKETS - 1, h], F32)
        samp_ref[h:h + 1, :] = _bias_lookup(bs, rb_ref, h)


def dsa_bias(rel_bias, past):
    return pl.pallas_call(
        functools.partial(_dsa_bias_kernel, past=past),
        in_specs=[pl.BlockSpec(memory_space=pltpu.SMEM)],
        out_shape=[jax.ShapeDtypeStruct((AT_HEADS, DSA_TQ, 2 * DSA_TQ), F32),
                   jax.ShapeDtypeStruct((AT_HEADS, 1, LANES), F32),
                   jax.ShapeDtypeStruct((AT_HEADS, past + PAGE), F32)],
    )(rel_bias)


def _order_key(s):
    bits = pltpu.bitcast(s, I32)
    return bits ^ ((bits >> 31) & INT_MAX)


def _count(key_scr, nck, pred):
    r, cw = key_scr.shape[1], key_scr.shape[2]

    def body(c, acc):
        m = jnp.where(pred(key_scr[c], c), 1.0, 0.0)
        part = m[:, 0:LANES]
        for j in range(1, cw // LANES):
            part = part + m[:, j * LANES:(j + 1) * LANES]
        return acc + part

    acc = lax.fori_loop(0, nck, body, jnp.zeros((r, LANES), F32))
    return jnp.sum(acc, axis=1, keepdims=True)


def _topk_select(key_scr, cut_scr, nck, n_sel, idx_bits):
    r, cw = key_scr.shape[1], key_scr.shape[2]
    nf = float(n_sel)
    lane = lax.broadcasted_iota(I32, (r, cw), 1)

    cnt = _count(key_scr, nck, lambda kk, c: kk >= 0)
    thr = jnp.where(cnt >= nf, 0, INT_MIN).astype(I32)

    def bit_body(i, thr):
        trial = thr | jnp.left_shift(jnp.int32(1), 30 - i)
        cnt = _count(key_scr, nck, lambda kk, c: kk >= trial)
        return jnp.where(cnt >= nf, trial, thr)

    thr = lax.fori_loop(0, 31, bit_body, thr)
    cnt_ge = _count(key_scr, nck, lambda kk, c: kk >= thr)
    need = nf - _count(key_scr, nck, lambda kk, c: kk > thr)

    cut_scr[...] = jnp.full(cut_scr.shape, INT_MAX, I32)

    @pl.when(jnp.max(cnt_ge) > nf)
    def _():
        def tie_body(i, rr):
            t = rr | jnp.left_shift(jnp.int32(1), idx_bits - 1 - i)
            below = _count(key_scr, nck, lambda kk, c: jnp.where(kk == thr, c * cw + lane, INT_MAX) < t)
            return jnp.where(below <= need - 1.0, t, rr)

        rr = lax.fori_loop(0, idx_bits, tie_body, jnp.zeros((r, 1), I32))
        cut_scr[...] = jnp.broadcast_to(jnp.where(cnt_ge > nf, rr, INT_MAX), cut_scr.shape)

    cut = cut_scr[:, 0:1]

    def sel_body(c, carry):
        kk = key_scr[c]
        tie = jnp.where(c * cw + lane <= cut, 1, 0)
        key_scr[c] = jnp.where(kk > thr, 1, jnp.where(kk == thr, tie, 0))
        return carry

    lax.fori_loop(0, nck, sel_body, 0)


def _dsa_prompt_kernel(x_ref, q_ref, qi_ref, wi_ref, ki_ref, k_ref, v_ref, tile_ref, far_ref,
                       wout_ref, g_ref, b_ref, o_ref, key_scr, cut_scr, att_scr, *, alpha, n_sel, idx_bits):
    tq = DSA_TQ
    ck = DSA_CK
    qb = pl.program_id(1)
    q0 = qb * tq
    nck = (q0 + tq + ck - 1) // ck
    qpos = q0 + lax.broadcasted_iota(I32, (tq, ck), 0)
    lane = lax.broadcasted_iota(I32, (tq, ck), 1)

    qi = qi_ref[0]
    wi = wi_ref[0]
    qis = [qi[:, h * IDX_DIM:(h + 1) * IDX_DIM] for h in range(IDX_HEADS)]
    wis = [wi[:, h:h + 1] for h in range(IDX_HEADS)]

    def score_body(c, carry):
        kic = ki_ref[0, pl.ds(pl.multiple_of(c * ck, ck), ck), :]
        s = jnp.zeros((tq, ck), F32)
        for h in range(IDX_HEADS):
            dots = lax.dot_general(qis[h], kic, NT, preferred_element_type=F32)
            s = s + wis[h] * jnp.maximum(dots, 0.0)
        s = jnp.where(c * ck + lane <= qpos, s, NEG_BIG)
        key_scr[c] = _order_key(s)
        return carry

    lax.fori_loop(0, nck, score_body, 0)
    _topk_select(key_scr, cut_scr, nck, n_sel, idx_bits)

    for h in range(AT_HEADS):
        hs = slice(h * AT_DH, (h + 1) * AT_DH)
        qh = q_ref[0, :, hs]
        near = tile_ref[h]
        far = jnp.broadcast_to(far_ref[h], (tq, LANES))

        def att_body(c, carry):
            m, l, acc = carry
            k0 = pl.multiple_of(c * ck, ck)
            kc = k_ref[0, pl.ds(k0, ck), hs]
            vc = v_ref[0, pl.ds(k0, ck), hs]
            s = lax.dot_general(qh, kc, NT, preferred_element_type=F32)
            bias = []
            for j in range(ck // LANES):
                delta = q0 - (k0 + j * LANES)
                bias.append(jnp.where(delta == 0, near[:, tq:2 * tq],
                                      jnp.where(delta == tq, near[:, 0:tq], far)))
            ok = jnp.where(k0 + lane <= qpos, key_scr[c], 0) > 0
            s = jnp.where(ok, s + jnp.concatenate(bias, axis=1), NEG_BIG)
            mn = jnp.maximum(m, jnp.max(s, axis=1, keepdims=True))
            a = jnp.exp(m - mn)
            p = jnp.where(ok, jnp.exp(s - mn), 0.0)
            l = a * l + jnp.sum(p, axis=1, keepdims=True)
            acc = a * acc + _bdot(p.astype(BF16), vc)
            return mn, l, acc

        m, l, acc = lax.fori_loop(0, nck, att_body,
                                  (jnp.full((tq, 1), NEG_BIG, F32), jnp.zeros((tq, 1), F32),
                                   jnp.zeros((tq, AT_DH), F32)))
        att_scr[:, hs] = acc / l
    y = _bdot(att_scr[...].astype(BF16), wout_ref[...])
    o_ref[0] = _ln(alpha * x_ref[0] + y, g_ref[...], b_ref[...])


def dsa_prompt(x, qb, qi, wi, kibf, kbf, vbf, tile, far, w_out, g, b, alpha):
    n, l, d = x.shape
    n_sel = min(TOPK_MAX, l // 4)
    nck_max = (l + DSA_CK - 1) // DSA_CK
    idx_bits = max(1, int(nck_max * DSA_CK - 1).bit_length())
    tok = lambda w: pl.BlockSpec((1, DSA_TQ, w), lambda s, t: (s, t, 0))
    seq = lambda w: pl.BlockSpec((1, l, w), lambda s, t: (s, 0, 0), pipeline_mode=pl.Buffered(1))
    return pl.pallas_call(
        functools.partial(_dsa_prompt_kernel, alpha=alpha, n_sel=n_sel, idx_bits=idx_bits),
        grid=(n, l // DSA_TQ),
        in_specs=[tok(d), tok(d), tok(qi.shape[2]), tok(LANES), seq(IDX_DIM), seq(d), seq(d),
                  _const(tile.shape), _const(far.shape), _const(w_out.shape), _const((1, d)), _const((1, d))],
        out_specs=tok(d),
        out_shape=jax.ShapeDtypeStruct((n, l, d), F32),
        scratch_shapes=[pltpu.VMEM((nck_max, DSA_TQ, DSA_CK), I32), pltpu.VMEM((DSA_TQ, LANES), I32),
                        pltpu.VMEM((DSA_TQ, d), F32)],
        compiler_params=_params("parallel", "arbitrary"),
    )(x, qb, qi, wi, kibf, kbf, vbf, tile, far, w_out, g, b)


def _dsa_sample_score_kernel(pt_ref, qi_ref, wi_ref, kidx_ref, o_ref):
    kp = kidx_ref[0, 0].astype(BF16)
    dots = lax.dot_general(qi_ref[0], kp, NT, preferred_element_type=F32)
    o_ref[0, 0] = jnp.sum(wi_ref[0] * jnp.maximum(dots, 0.0), axis=0, keepdims=True)


def _dsa_sample_select_kernel(sc_ref, qi_ref, ki_ref, wi_ref, sel_ref, key_scr, cut_scr, *, n_sel, idx_bits):
    npg, nb, pw = sc_ref.shape
    for c in range(npg):
        key_scr[c] = _order_key(sc_ref[c])
    qi = qi_ref[...].astype(F32)
    ki = ki_ref[...].astype(F32)
    wi = wi_ref[...]
    s = jnp.zeros((nb, 1), F32)
    for h in range(IDX_HEADS):
        dot = jnp.sum(qi[:, h * IDX_DIM:(h + 1) * IDX_DIM] * ki, axis=1, keepdims=True)
        s = s + wi[:, h:h + 1] * jnp.maximum(dot, 0.0)
    lane = lax.broadcasted_iota(I32, (nb, pw), 1)
    key_scr[npg] = _order_key(jnp.where(lane == 0, s, NEG_BIG))
    _topk_select(key_scr, cut_scr, npg + 1, n_sel, idx_bits)
    for c in range(npg + 1):
        sel_ref[c] = key_scr[c]


def _dsa_sample_attn_kernel(pt_ref, q_ref, k_ref, v_ref, sel_ref, seln_ref, sb_ref, sbn_ref, kn_ref, vn_ref,
                            o_ref, m_scr, l_scr, acc_scr):
    pg = pl.program_id(1)

    @pl.when(pg == 0)
    def _():
        m_scr[...] = jnp.full(m_scr.shape, NEG_BIG, F32)
        l_scr[...] = jnp.zeros(l_scr.shape, F32)
        acc_scr[...] = jnp.zeros(acc_scr.shape, F32)

    def update(s, ok, pv):
        s = jnp.where(ok, s, NEG_BIG)
        m = m_scr[:, 0:1]
        mn = jnp.maximum(m, jnp.max(s, axis=1, keepdims=True))
        a = jnp.exp(m - mn)
        p = jnp.where(ok, jnp.exp(s - mn), 0.0)
        l_scr[...] = jnp.broadcast_to(a * l_scr[:, 0:1] + jnp.sum(p, axis=1, keepdims=True), l_scr.shape)
        acc_scr[...] = a * acc_scr[...] + pv(p)
        m_scr[...] = jnp.broadcast_to(mn, m_scr.shape)

    q = q_ref[0]
    s = lax.dot_general(q, k_ref[0, 0].astype(BF16), NT, preferred_element_type=F32) + sb_ref[...]
    vals = v_ref[0, 0].astype(BF16)
    update(s, jnp.broadcast_to(sel_ref[0, 0], s.shape) > 0, lambda p: _bdot(p.astype(BF16), vals))

    @pl.when(pg == pl.num_programs(1) - 1)
    def _():
        sn = jnp.sum(q.astype(F32) * kn_ref[0].astype(F32), axis=1, keepdims=True) + sbn_ref[:, 0:1]
        okn = jnp.broadcast_to(seln_ref[0, 0][:, 0:1], sn.shape) > 0
        vn = vn_ref[0].astype(F32)
        update(sn, okn, lambda p: p.astype(BF16).astype(F32) * vn)
        for h in range(AT_HEADS):
            hs = slice(h * AT_DH, (h + 1) * AT_DH)
            o_ref[0, :, hs] = acc_scr[h:h + 1, hs] / l_scr[h:h + 1, 0:1]


def dsa_sample(x, j, qb, qib, kiln, wi, kbf, vbf, cache_k, cache_v, cache_kidx, page_table, sbias,
               w_out, g, b, alpha):
    n, d = x.shape
    npg = page_table.shape[1]
    past = npg * PAGE
    n_sel = min(TOPK_MAX, (past + 1) // 4)
    idx_bits = int((npg + 1) * PAGE - 1).bit_length()
    pt = page_table.reshape(-1)
    npool = cache_k.shape[1]

    scores = pl.pallas_call(
        _dsa_sample_score_kernel,
        grid_spec=pltpu.PrefetchScalarGridSpec(
            num_scalar_prefetch=1, grid=(n, npg),
            in_specs=[pl.BlockSpec((1, IDX_HEADS, IDX_DIM), lambda i, p, t: (i, 0, 0)),
                      pl.BlockSpec((1, IDX_HEADS, 1), lambda i, p, t: (i, 0, 0)),
                      pl.BlockSpec((1, 1, PAGE, IDX_DIM), lambda i, p, t: (j, t[i * npg + p], 0, 0))],
            out_specs=pl.BlockSpec((1, 1, 1, PAGE), lambda i, p, t: (p, i, 0, 0))),
        out_shape=jax.ShapeDtypeStruct((npg, n, 1, PAGE), F32),
        compiler_params=_params("parallel", "arbitrary"),
    )(pt, qib.reshape(n, IDX_HEADS, IDX_DIM), wi[:, 0:IDX_HEADS].reshape(n, IDX_HEADS, 1), cache_kidx)

    sel = pl.pallas_call(
        functools.partial(_dsa_sample_select_kernel, n_sel=n_sel, idx_bits=idx_bits),
        out_shape=jax.ShapeDtypeStruct((npg + 1, n, PAGE), I32),
        scratch_shapes=[pltpu.VMEM((npg + 1, n, PAGE), I32), pltpu.VMEM((n, LANES), I32)],
        compiler_params=pltpu.CompilerParams(vmem_limit_bytes=VMEM_LIMIT),
    )(scores.reshape(npg, n, PAGE), qib, kiln.astype(BF16), wi)
    sel = sel.reshape(npg + 1, n, 1, PAGE)

    qbd = jnp.einsum('nhd,hg->nhgd', qb.reshape(n, AT_HEADS, AT_DH),
                     jnp.eye(AT_HEADS, dtype=BF16)).reshape(n, AT_HEADS, d)
    ck4 = cache_k.reshape(cache_k.shape[0], npool, PAGE, d)
    cv4 = cache_v.reshape(cache_v.shape[0], npool, PAGE, d)
    page = pl.BlockSpec((1, 1, PAGE, d), lambda i, p, t: (j, t[i * npg + p], 0, 0))
    tokrow = pl.BlockSpec((1, 1, d), lambda i, p, t: (i, 0, 0))
    o = pl.pallas_call(
        _dsa_sample_attn_kernel,
        grid_spec=pltpu.PrefetchScalarGridSpec(
            num_scalar_prefetch=1, grid=(n, npg),
            in_specs=[pl.BlockSpec((1, AT_HEADS, d), lambda i, p, t: (i, 0, 0)), page, page,
                      pl.BlockSpec((1, 1, 1, PAGE), lambda i, p, t: (p, i, 0, 0)),
                      pl.BlockSpec((1, 1, 1, PAGE), lambda i, p, t: (npg, i, 0, 0)),
                      pl.BlockSpec((AT_HEADS, PAGE), lambda i, p, t: (0, p)),
                      pl.BlockSpec((AT_HEADS, PAGE), lambda i, p, t: (0, npg)),
                      tokrow, tokrow],
            out_specs=tokrow,
            scratch_shapes=[pltpu.VMEM((AT_HEADS, LANES), F32), pltpu.VMEM((AT_HEADS, LANES), F32),
                            pltpu.VMEM((AT_HEADS, d), F32)]),
        out_shape=jax.ShapeDtypeStruct((n, 1, d), F32),
        compiler_params=_params("parallel", "arbitrary"),
    )(pt, qbd, ck4, cv4, sel, sel, sbias, sbias, kbf.reshape(n, 1, d), vbf.reshape(n, 1, d))
    return mm_res_ln(o.reshape(n, d), w_out, x, g, b, alpha, n)


MLP_TM = 512


def kernel(x_prompt, x_sample, state_hgrn, cache_k, cache_v, cache_kidx, page_table, state_s5_re, state_s5_im, hg_w_in, hg_lb, hg_gnorm, hg_w_out, dsa_w_in, dsa_kln_g, dsa_kln_b, dsa_w_out, rel_bias, s5_a_re, s5_a_im, s5_log_dt, s5_b_re, s5_b_im, s5_c_re, s5_c_im, s5_d, s5_w_glu, s5_b_glu, ln_mix_g, ln_mix_b, mlp_w1, mlp_w2, ln_ffn_g, ln_ffn_b):
    n, l, d = x_prompt.shape
    nb = x_sample.shape[0]
    depth = ln_mix_g.shape[0]
    alpha = (2 * depth) ** 0.25
    past = page_table.shape[1] * PAGE
    row = lambda a: a.reshape(1, -1)
    xp = x_prompt
    xs = x_sample.reshape(nb, d)
    hg_p, hg_s = [], []
    k_p, v_p, ki_p, k_s, v_s, ki_s = [], [], [], [], [], []
    s5r_p, s5i_p, s5r_s, s5i_s = [], [], [], []
    for i in range(depth):
        j = i // N_MIXERS
        kind = i % N_MIXERS
        g, b = row(ln_mix_g[i]), row(ln_mix_b[i])
        if kind == 0:
            w_in, w_out, gn = hg_w_in[j].astype(BF16), hg_w_out[j].astype(BF16), row(hg_gnorm[j])
            xp, st_p = hgrn_prompt(xp, w_in, hg_lb, gn, w_out, g, b, alpha, i)
            xs, st_s = hgrn_sample(xs, state_hgrn[j], w_in, hg_lb, gn, w_out, g, b, alpha, i)
            hg_p.append(st_p)
            hg_s.append(st_s)
        elif kind == 1:
            w_out = dsa_w_out[j].astype(BF16)
            tile, far, sbias = dsa_bias(rel_bias, past)
            qb, k, v, kbf, vbf, qib, kiln, kibf, wi = dsa_proj(xp.reshape(n * l, d), dsa_w_in[j],
                                                               dsa_kln_g[j], dsa_kln_b[j], MLP_TM)
            seq = lambda a: a.reshape(n, l, a.shape[1])
            xp = dsa_prompt(xp, seq(qb), seq(qib), seq(wi), seq(kibf), seq(kbf), seq(vbf), tile, far,
                            w_out, g, b, alpha)
            k_p.append(k.reshape(n, l, AT_HEADS, AT_DH))
            v_p.append(v.reshape(n, l, AT_HEADS, AT_DH))
            ki_p.append(kiln.reshape(n, l, IDX_DIM))
            qb, k, v, kbf, vbf, qib, kiln, kibf, wi = dsa_proj(xs, dsa_w_in[j], dsa_kln_g[j], dsa_kln_b[j], nb)
            xs = dsa_sample(xs, j, qb, qib, kiln, wi, kbf, vbf, cache_k, cache_v, cache_kidx, page_table,
                            sbias, w_out, g, b, alpha)
            k_s.append(k.reshape(nb, 1, AT_HEADS, AT_DH))
            v_s.append(v.reshape(nb, 1, AT_HEADS, AT_DH))
            ki_s.append(kiln.reshape(nb, 1, IDX_DIM))
        else:
            prm = s5_params(s5_a_re[j], s5_a_im[j], s5_log_dt[j], s5_b_re[j], s5_b_im[j], s5_c_re[j],
                            s5_c_im[j], s5_d[j], s5_w_glu[j], s5_b_glu[j])
            sg, sp = s5_a_re.shape[1], s5_a_re.shape[2]
            xp, hr, hi = s5_prompt(xp, prm, g, b, alpha)
            xs, hr2, hi2 = s5_sample(xs, state_s5_re[j].reshape(nb, sg * sp), state_s5_im[j].reshape(nb, sg * sp),
                                     prm, g, b, alpha)
            s5r_p.append(hr.reshape(n, sg, sp))
            s5i_p.append(hi.reshape(n, sg, sp))
            s5r_s.append(hr2.reshape(nb, sg, sp))
            s5i_s.append(hi2.reshape(nb, sg, sp))
        w1, w2 = mlp_w1[i].astype(BF16), mlp_w2[i].astype(BF16)
        fg, fb = row(ln_ffn_g[i]), row(ln_ffn_b[i])
        xp = mlp_res_ln(xp.reshape(n * l, d), w1, w2, fg, fb, alpha, MLP_TM).reshape(n, l, d)
        xs = mlp_res_ln(xs, w1, w2, fg, fb, alpha, nb)
    return (xp, xs.reshape(nb, 1, d), jnp.stack(hg_p), jnp.stack(hg_s), jnp.stack(k_p), jnp.stack(v_p),
            jnp.stack(ki_p), jnp.stack(k_s), jnp.stack(v_s), jnp.stack(ki_s), jnp.stack(s5r_p),
            jnp.stack(s5i_p), jnp.stack(s5r_s), jnp.stack(s5i_s))
```

```python
import functools
import math

import numpy as np
import jax
import jax.numpy as jnp
from jax import lax
from jax.experimental import pallas as pl
from jax.experimental.pallas import tpu as pltpu

F32 = jnp.float32
BF16 = jnp.bfloat16
I32 = jnp.int32

LN_EPS = 1e-5
RMS_EPS = 1e-6
NEG_BIG = -1e30
N_MIXERS = 3
HG_HEADS = 8
HG_DK = 128
AT_HEADS = 8
AT_DH = 128
IDX_HEADS = 8
IDX_DIM = 64
TOPK_MAX = 256
REL_BUCKETS = 32
REL_MAX_DIST = 128
S5_GROUP = 16
S5_STATE = 64
PAGE = 128

LANES = 128
SUBLANES = 8
VMEM_LIMIT = 56 * 1024 * 1024

NT = (((1,), (1,)), ((), ()))
TN = (((0,), (0,)), ((), ()))


def _params(*sem):
    return pltpu.CompilerParams(dimension_semantics=sem, vmem_limit_bytes=VMEM_LIMIT)


def _const(shape):
    nd = len(shape)
    return pl.BlockSpec(shape, lambda *_: (0,) * nd, pipeline_mode=pl.Buffered(1))


def _ln(z, g, b):
    mu = jnp.mean(z, axis=-1, keepdims=True)
    zc = z - mu
    var = jnp.mean(zc * zc, axis=-1, keepdims=True)
    return zc * lax.rsqrt(var + LN_EPS) * g + b


def _sigmoid(z):
    return 1.0 / (1.0 + jnp.exp(-z))


def _bdot(a, b):
    return jnp.dot(a, b, preferred_element_type=F32)


def _mlp_kernel(x_ref, w1_ref, w2_ref, g_ref, b_ref, o_ref, *, alpha, hc):
    x = x_ref[...]
    xb = x.astype(BF16)
    acc = jnp.zeros(x.shape, F32)
    for c in range(w1_ref.shape[1] // hc):
        h = _bdot(xb, w1_ref[:, c * hc:(c + 1) * hc])
        h = jnp.maximum(h, 0.0)
        acc = acc + _bdot((h * h).astype(BF16), w2_ref[c * hc:(c + 1) * hc, :])
    o_ref[...] = _ln(alpha * x + acc, g_ref[...], b_ref[...])


def mlp_res_ln(x, w1, w2, g, b, alpha, tm):
    t, d = x.shape
    f = w1.shape[1]
    return pl.pallas_call(
        functools.partial(_mlp_kernel, alpha=alpha, hc=1024),
        grid=(t // tm,),
        in_specs=[pl.BlockSpec((tm, d), lambda i: (i, 0)), _const((d, f)), _const((f, d)),
                  _const((1, d)), _const((1, d))],
        out_specs=pl.BlockSpec((tm, d), lambda i: (i, 0)),
        out_shape=jax.ShapeDtypeStruct((t, d), F32),
        compiler_params=_params("parallel"),
    )(x, w1, w2, g, b)


def _mm_kernel(x_ref, w_ref, o_ref):
    o_ref[...] = _bdot(x_ref[...].astype(BF16), w_ref[...])


def mm(x, w, tm):
    t, d = x.shape
    n = w.shape[1]
    return pl.pallas_call(
        _mm_kernel,
        grid=(t // tm,),
        in_specs=[pl.BlockSpec((tm, d), lambda i: (i, 0)), _const((d, n))],
        out_specs=pl.BlockSpec((tm, n), lambda i: (i, 0)),
        out_shape=jax.ShapeDtypeStruct((t, n), F32),
        compiler_params=_params("parallel"),
    )(x, w)


def _mm_res_ln_kernel(a_ref, w_ref, x_ref, g_ref, b_ref, o_ref, *, alpha):
    y = _bdot(a_ref[...].astype(BF16), w_ref[...])
    o_ref[...] = _ln(alpha * x_ref[...] + y, g_ref[...], b_ref[...])


def mm_res_ln(a, w, x, g, b, alpha, tm):
    t, k = a.shape
    d = w.shape[1]
    return pl.pallas_call(
        functools.partial(_mm_res_ln_kernel, alpha=alpha),
        grid=(t // tm,),
        in_specs=[pl.BlockSpec((tm, k), lambda i: (i, 0)), _const((k, d)),
                  pl.BlockSpec((tm, d), lambda i: (i, 0)), _const((1, d)), _const((1, d))],
        out_specs=pl.BlockSpec((tm, d), lambda i: (i, 0)),
        out_shape=jax.ShapeDtypeStruct((t, d), F32),
        compiler_params=_params("parallel"),
    )(a, w, x, g, b)


HG_C = 128
HG_LEVELS = 7


def _hgrn_constants():
    c = HG_C
    t = np.arange(c)[:, None]
    u = np.arange(c)[None, :]
    mats = [(u <= t), (u > t)]
    for l in range(HG_LEVELS):
        blk = 2 << l
        mid = (t // blk) * blk + blk // 2
        second = t >= mid
        mats.append(np.where(second, (u >= mid) & (u <= t), (u > t) & (u < mid)))
    m = np.concatenate(mats, axis=0).astype(np.float32)
    x = t ^ u
    lv = np.where(u < t, np.floor(np.log2(np.maximum(x, 1))).astype(np.int32),
                  np.where(u == t, HG_LEVELS, -1)).astype(np.int32)
    return np.concatenate([m, m, m], axis=1), lv


def _lower_bound(lb_ref, layer):
    rows = [lb_ref[i] for i in range(lb_ref.shape[0])]
    mx = rows[0]
    for r in rows[1:]:
        mx = jnp.maximum(mx, r)
    ex = [jnp.exp(r - mx) for r in rows]
    tot = ex[0]
    for e in ex[1:]:
        tot = tot + e
    num = jnp.zeros_like(tot)
    for i in range(1, layer + 1):
        num = num + ex[i]
    return num / tot


def _split3(x):
    hi = x.astype(BF16)
    r1 = x - hi.astype(F32)
    mid = r1.astype(BF16)
    lo = (r1 - mid.astype(F32)).astype(BF16)
    return hi, mid, lo


def _hgrn_prompt_kernel(x_ref, win_ref, lb_ref, gn_ref, wout_ref, g_ref, b_ref, mst_ref, lv_ref,
                        o_ref, st_ref, og_scr, *, alpha, layer):
    c = HG_C
    dk = HG_DK

    @pl.when(pl.program_id(1) == 0)
    def _():
        st_ref[...] = jnp.zeros(st_ref.shape, F32)

    x = x_ref[0]
    d = x.shape[1]
    proj = _bdot(x.astype(BF16), win_ref[...])
    qp = proj[:, 0:d]
    fz = proj[:, d:2 * d]
    v = proj[:, 2 * d:3 * d]
    gp = proj[:, 3 * d:4 * d]
    lb = _lower_bound(lb_ref, layer)
    q = qp * _sigmoid(qp)
    sg = _sigmoid(fz)
    f = lb + (1.0 - lb) * sg
    k = (1.0 - lb) * (1.0 - sg)
    logf = jnp.log(f)
    gate = gp * _sigmoid(gp)
    hi, mid, lo = _split3(logf)
    ex = _bdot(mst_ref[...], jnp.concatenate([hi, mid, lo], axis=0))

    lv = lv_ref[...]
    row = lax.broadcasted_iota(I32, (c, dk), 0)
    for h in range(HG_HEADS):
        hs = slice(h * dk, (h + 1) * dk)
        qh = q[:, hs]
        kh = k[:, hs]
        vh = v[:, hs].astype(BF16)
        st = st_ref[0, h]
        eb = jnp.exp(ex[0:c, hs])
        o = lax.dot_general((qh * eb).astype(BF16), st.astype(BF16), NT, preferred_element_type=F32)
        att = jnp.zeros((c, c), F32)
        for l in range(HG_LEVELS):
            e = jnp.exp(ex[(2 + l) * c:(3 + l) * c, hs])
            second = ((row >> l) & 1) == 1
            w = (jnp.where(second, qh, kh) * e).astype(BF16)
            a = lax.dot_general(w, w, NT, preferred_element_type=F32)
            att = jnp.where(lv == l, a, att)
        a0 = lax.dot_general(qh.astype(BF16), kh.astype(BF16), NT, preferred_element_type=F32)
        att = jnp.where(lv == HG_LEVELS, a0, att)
        o = o + _bdot(att.astype(BF16), vh)
        kd = (kh * jnp.exp(ex[c:2 * c, hs])).astype(BF16)
        st_ref[0, h] = eb[c - 1:c, :] * st + lax.dot_general(vh, kd, TN, preferred_element_type=F32)
        ms = jnp.mean(o * o, axis=-1, keepdims=True)
        og_scr[:, hs] = o * lax.rsqrt(ms + RMS_EPS) * gn_ref[:, hs] * gate[:, hs]
    y = _bdot(og_scr[...].astype(BF16), wout_ref[...])
    o_ref[0] = _ln(alpha * x + y, g_ref[...], b_ref[...])


def hgrn_prompt(x, w_in, hg_lb, gnorm, w_out, g, b, alpha, layer):
    n, l, d = x.shape
    mst, lv = _hgrn_constants()
    nl = hg_lb.shape[0]
    out, st = pl.pallas_call(
        functools.partial(_hgrn_prompt_kernel, alpha=alpha, layer=layer),
        grid=(n, l // HG_C),
        in_specs=[pl.BlockSpec((1, HG_C, d), lambda s, c: (s, c, 0)),
                  _const((d, 4 * d)), _const((nl, 1, d)), _const((1, d)), _const((d, d)),
                  _const((1, d)), _const((1, d)), _const(mst.shape), _const(lv.shape)],
        out_specs=[pl.BlockSpec((1, HG_C, d), lambda s, c: (s, c, 0)),
                   pl.BlockSpec((1, HG_HEADS, HG_DK, HG_DK), lambda s, c: (s, 0, 0, 0))],
        out_shape=[jax.ShapeDtypeStruct((n, l, d), F32),
                   jax.ShapeDtypeStruct((n, HG_HEADS, HG_DK, HG_DK), F32)],
        scratch_shapes=[pltpu.VMEM((HG_C, d), F32)],
        compiler_params=_params("parallel", "arbitrary"),
    )(x, w_in, hg_lb.reshape(nl, 1, d), gnorm, w_out, g, b,
      jnp.asarray(mst, BF16), jnp.asarray(lv))
    return out, jnp.swapaxes(st, 2, 3)


def _hgrn_sample_kernel(qt_ref, fzt_ref, lbt_ref, v_ref, gp_ref, gn_ref, s0_ref, o_ref, s1_ref, *, layer):
    dk = HG_DK
    lbt = _lower_bound(lbt_ref, layer)
    qt = qt_ref[0]
    q = qt * _sigmoid(qt)
    sg = _sigmoid(fzt_ref[0])
    f = lbt + (1.0 - lbt) * sg
    k = (1.0 - lbt) * (1.0 - sg)
    gp = gp_ref[0]
    gate = gp * _sigmoid(gp)
    for h in range(HG_HEADS):
        hs = slice(h * dk, (h + 1) * dk)
        s_new = f[:, h:h + 1] * s0_ref[0, h] + k[:, h:h + 1] * v_ref[0, :, hs]
        s1_ref[0, h] = s_new
        o = jnp.sum(s_new * q[:, h:h + 1], axis=0, keepdims=True)
        ms = jnp.mean(o * o, axis=-1, keepdims=True)
        o_ref[0, :, hs] = o * lax.rsqrt(ms + RMS_EPS) * gn_ref[:, hs] * gate[:, hs]


def hgrn_sample(x, s0, w_in, hg_lb, gnorm, w_out, g, b, alpha, layer):
    n, d = x.shape
    nl = hg_lb.shape[0]
    proj = mm(x, w_in, n)
    to_cols = lambda a: a.reshape(n, HG_HEADS, HG_DK).transpose(0, 2, 1)
    lbt = hg_lb.reshape(nl, HG_HEADS, HG_DK).transpose(0, 2, 1)
    row3 = lambda a: a.reshape(n, 1, d)
    o, s1 = pl.pallas_call(
        functools.partial(_hgrn_sample_kernel, layer=layer),
        grid=(n,),
        in_specs=[pl.BlockSpec((1, HG_DK, HG_HEADS), lambda i: (i, 0, 0)),
                  pl.BlockSpec((1, HG_DK, HG_HEADS), lambda i: (i, 0, 0)),
                  _const((nl, HG_DK, HG_HEADS)),
                  pl.BlockSpec((1, 1, d), lambda i: (i, 0, 0)),
                  pl.BlockSpec((1, 1, d), lambda i: (i, 0, 0)),
                  _const((1, d)),
                  pl.BlockSpec((1, HG_HEADS, HG_DK, HG_DK), lambda i: (i, 0, 0, 0))],
        out_specs=[pl.BlockSpec((1, 1, d), lambda i: (i, 0, 0)),
                   pl.BlockSpec((1, HG_HEADS, HG_DK, HG_DK), lambda i: (i, 0, 0, 0))],
        out_shape=[jax.ShapeDtypeStruct((n, 1, d), F32),
                   jax.ShapeDtypeStruct(s0.shape, F32)],
        compiler_params=_params("parallel"),
    )(to_cols(proj[:, 0:d]), to_cols(proj[:, d:2 * d]), lbt, row3(proj[:, 2 * d:3 * d]),
      row3(proj[:, 3 * d:4 * d]), gnorm, s0)
    return mm_res_ln(o.reshape(n, d), w_out, x, g, b, alpha, n), s1


S5_TC = 128
S5_SLABS = 8
S5_KS = 3


def _cmul(ar, ai, br, bi):
    return ar * br - ai * bi, ar * bi + ai * br


def _s5_prep_kernel(are_ref, aim_ref, ldt_ref, are2_ref, aim2_ref, ldt2_ref, bre_ref, bim_ref,
                    pwre_ref, pwim_ref, ckre_ref, ckim_ref, bbre_ref, bbim_ref):
    dt = jnp.exp(ldt_ref[...])
    mag = jnp.exp(dt * are_ref[...])
    a_r = mag * jnp.cos(dt * aim_ref[...])
    a_i = mag * jnp.sin(dt * aim_ref[...])
    pr, pi_ = [a_r], [a_i]
    for _ in range(SUBLANES - 1):
        nr, ni = _cmul(pr[-1], pi_[-1], a_r, a_i)
        pr.append(nr)
        pi_.append(ni)
    pwre_ref[...] = jnp.concatenate(pr, axis=0)
    pwim_ref[...] = jnp.concatenate(pi_, axis=0)
    rows = lax.broadcasted_iota(I32, pwre_ref.shape, 0)
    for k in range(S5_KS):
        s = 1 << k
        ckre_ref[k] = jnp.where(rows >= s, pr[s - 1], 0.0)
        ckim_ref[k] = jnp.where(rows >= s, pi_[s - 1], 0.0)
    ar, ai = are2_ref[...], aim2_ref[...]
    dt2 = jnp.exp(ldt2_ref[...])
    mag2 = jnp.exp(dt2 * ar)
    nr = mag2 * jnp.cos(dt2 * ai) - 1.0
    ni = mag2 * jnp.sin(dt2 * ai)
    den = ar * ar + ai * ai
    z_re = ((nr * ar + ni * ai) / den)[:, None, :]
    z_im = ((ni * ar - nr * ai) / den)[:, None, :]
    br, bi = bre_ref[...], bim_ref[...]
    bbre_ref[...] = z_re * br - z_im * bi
    bbim_ref[...] = z_re * bi + z_im * br


def s5_prep(a_re, a_im, log_dt, b_re, b_im):
    g, p = a_re.shape
    k = b_re.shape[2]
    n = g * p
    ldt2 = jnp.broadcast_to(log_dt[:, None], (g, p))
    flat = lambda a: a.reshape(1, n)
    return pl.pallas_call(
        _s5_prep_kernel,
        out_shape=[jax.ShapeDtypeStruct((SUBLANES, n), F32), jax.ShapeDtypeStruct((SUBLANES, n), F32),
                   jax.ShapeDtypeStruct((S5_KS, SUBLANES, n), F32), jax.ShapeDtypeStruct((S5_KS, SUBLANES, n), F32),
                   jax.ShapeDtypeStruct((g, k, p), F32), jax.ShapeDtypeStruct((g, k, p), F32)],
    )(flat(a_re), flat(a_im), flat(ldt2), a_re, a_im, ldt2,
      b_re.transpose(0, 2, 1), b_im.transpose(0, 2, 1))


def _block_diag_slabs(w_gkp, rows_are_inputs):
    g, k, p = w_gkp.shape
    per = g // S5_SLABS
    w4 = w_gkp.reshape(S5_SLABS, per, k, p)
    eye = jnp.eye(per, dtype=w_gkp.dtype)
    if rows_are_inputs:
        return jnp.einsum('iakp,ab->iakbp', w4, eye).reshape(S5_SLABS, per * k, per * p)
    return jnp.einsum('iakp,ab->ibpak', w4, eye).reshape(S5_SLABS, per * p, per * k)


def _s5_tail(x, y, d_ref, wglu_ref, bglu_ref, g_ref, b_ref, alpha):
    d = x.shape[1]
    gl = jax.nn.gelu(y + d_ref[...] * x)
    u = _bdot(gl.astype(BF16), wglu_ref[...]) + bglu_ref[...]
    out = u[:, 0:d] * _sigmoid(u[:, d:2 * d])
    return _ln(alpha * x + out, g_ref[...], b_ref[...])


def _s5_prompt_kernel(x_ref, wbre_ref, wbim_ref, wcre_ref, wcim_ref, pwre_ref, pwim_ref, ckre_ref, ckim_ref,
                      d_ref, wglu_ref, bglu_ref, g_ref, b_ref,
                      o_ref, hre_ref, him_ref, cre_scr, cim_scr, y_scr, *, alpha):
    tc = x_ref.shape[1]
    ng = tc // SUBLANES
    ci = pl.program_id(1)

    @pl.when(ci == 0)
    def _():
        cre_scr[...] = jnp.zeros(cre_scr.shape, F32)
        cim_scr[...] = jnp.zeros(cim_scr.shape, F32)

    x = x_ref[0]
    xb = x.astype(BF16)
    sw = wbre_ref.shape[2]
    for i in range(S5_SLABS):
        cs = slice(i * sw, (i + 1) * sw)
        xi = xb[:, i * LANES:(i + 1) * LANES]
        br = _bdot(xi, wbre_ref[i]).reshape(ng, SUBLANES, sw)
        bi = _bdot(xi, wbim_ref[i]).reshape(ng, SUBLANES, sw)
        for k in range(S5_KS):
            sr = pltpu.roll(br, 1 << k, axis=1)
            si = pltpu.roll(bi, 1 << k, axis=1)
            tr, ti = _cmul(ckre_ref[k, :, cs], ckim_ref[k, :, cs], sr, si)
            br, bi = br + tr, bi + ti
        pr, pi_ = pwre_ref[:, cs], pwim_ref[:, cs]
        hpr, hpi = cre_scr[:, cs], cim_scr[:, cs]
        hr_all, hi_all = [], []
        for m in range(ng):
            tr, ti = _cmul(pr, pi_, hpr, hpi)
            hr, hi = br[m] + tr, bi[m] + ti
            hr_all.append(hr)
            hi_all.append(hi)
            hpr = jnp.broadcast_to(hr[SUBLANES - 1:SUBLANES, :], hr.shape)
            hpi = jnp.broadcast_to(hi[SUBLANES - 1:SUBLANES, :], hi.shape)
        cre_scr[:, cs] = hpr
        cim_scr[:, cs] = hpi
        hr = jnp.concatenate(hr_all, axis=0).astype(BF16)
        hi = jnp.concatenate(hi_all, axis=0).astype(BF16)
        y_scr[:, i * LANES:(i + 1) * LANES] = _bdot(hr, wcre_ref[i]) - _bdot(hi, wcim_ref[i])
    o_ref[0] = _s5_tail(x, y_scr[...], d_ref, wglu_ref, bglu_ref, g_ref, b_ref, alpha)

    @pl.when(ci == pl.num_programs(1) - 1)
    def _():
        hre_ref[0] = cre_scr[0:1, :]
        him_ref[0] = cim_scr[0:1, :]


def _s5_weight_specs(wb, wc, pw, ck, d, wglu):
    return [_const(wb.shape), _const(wb.shape), _const(wc.shape), _const(wc.shape),
            _const(pw.shape), _const(pw.shape), _const(ck.shape), _const(ck.shape),
            _const((1, d)), _const(wglu.shape), _const((1, wglu.shape[1])), _const((1, d)), _const((1, d))]


def s5_prompt(x, prm, g, b, alpha):
    n, l, d = x.shape
    wb, wc, pw, ck, wglu = prm[0], prm[2], prm[4], prm[6], prm[9]
    ns = pw.shape[1]
    out, hre, him = pl.pallas_call(
        functools.partial(_s5_prompt_kernel, alpha=alpha),
        grid=(n, l // S5_TC),
        in_specs=[pl.BlockSpec((1, S5_TC, d), lambda s, c: (s, c, 0))] + _s5_weight_specs(wb, wc, pw, ck, d, wglu),
        out_specs=[pl.BlockSpec((1, S5_TC, d), lambda s, c: (s, c, 0)),
                   pl.BlockSpec((1, 1, ns), lambda s, c: (s, 0, 0)),
                   pl.BlockSpec((1, 1, ns), lambda s, c: (s, 0, 0))],
        out_shape=[jax.ShapeDtypeStruct((n, l, d), F32),
                   jax.ShapeDtypeStruct((n, 1, ns), F32), jax.ShapeDtypeStruct((n, 1, ns), F32)],
        scratch_shapes=[pltpu.VMEM((SUBLANES, ns), F32), pltpu.VMEM((SUBLANES, ns), F32),
                        pltpu.VMEM((S5_TC, d), F32)],
        compiler_params=_params("parallel", "arbitrary"),
    )(x, *prm, g, b)
    return out, hre, him


def _s5_sample_kernel(x_ref, h0re_ref, h0im_ref, wbre_ref, wbim_ref, wcre_ref, wcim_ref, pwre_ref, pwim_ref,
                      ckre_ref, ckim_ref, d_ref, wglu_ref, bglu_ref, g_ref, b_ref,
                      o_ref, hre_ref, him_ref, y_scr, *, alpha):
    x = x_ref[...]
    xb = x.astype(BF16)
    sw = wbre_ref.shape[2]
    for i in range(S5_SLABS):
        cs = slice(i * sw, (i + 1) * sw)
        xi = xb[:, i * LANES:(i + 1) * LANES]
        tr, ti = _cmul(pwre_ref[0:1, cs], pwim_ref[0:1, cs], h0re_ref[:, cs], h0im_ref[:, cs])
        hr = _bdot(xi, wbre_ref[i]) + tr
        hi = _bdot(xi, wbim_ref[i]) + ti
        hre_ref[:, cs] = hr
        him_ref[:, cs] = hi
        y_scr[:, i * LANES:(i + 1) * LANES] = (_bdot(hr.astype(BF16), wcre_ref[i])
                                               - _bdot(hi.astype(BF16), wcim_ref[i]))
    o_ref[...] = _s5_tail(x, y_scr[...], d_ref, wglu_ref, bglu_ref, g_ref, b_ref, alpha)


def s5_sample(x, h0re, h0im, prm, g, b, alpha):
    n, d = x.shape
    ns = h0re.shape[1]
    return pl.pallas_call(
        functools.partial(_s5_sample_kernel, alpha=alpha),
        out_shape=[jax.ShapeDtypeStruct((n, d), F32),
                   jax.ShapeDtypeStruct((n, ns), F32), jax.ShapeDtypeStruct((n, ns), F32)],
        scratch_shapes=[pltpu.VMEM((n, d), F32)],
        compiler_params=pltpu.CompilerParams(vmem_limit_bytes=VMEM_LIMIT),
    )(x, h0re, h0im, *prm, g, b)


def s5_params(a_re, a_im, log_dt, b_re, b_im, c_re, c_im, d_skip, w_glu, b_glu):
    pwre, pwim, ckre, ckim, bbre, bbim = s5_prep(a_re, a_im, log_dt, b_re, b_im)
    return (_block_diag_slabs(bbre, True).astype(BF16), _block_diag_slabs(bbim, True).astype(BF16),
            _block_diag_slabs(c_re, False).astype(BF16), _block_diag_slabs(c_im, False).astype(BF16),
            pwre, pwim, ckre, ckim, d_skip.reshape(1, -1), w_glu.astype(BF16), b_glu.reshape(1, -1))


DSA_TQ = 128
DSA_CK = 512
INT_MIN = -2 ** 31
INT_MAX = 2 ** 31 - 1
LOG2E = math.log2(math.e)


def _dsa_proj_kernel(x_ref, wq_ref, wk_ref, wv_ref, wqi_ref, wki_ref, wwi_ref, kg_ref, kb_ref,
                     qb_ref, k_ref, v_ref, kbf_ref, vbf_ref, qi_ref, kiln_ref, kibf_ref, wi_ref):
    xb = x_ref[...].astype(BF16)
    qb_ref[...] = (_bdot(xb, wq_ref[...]) * (AT_DH ** -0.5 * LOG2E)).astype(BF16)
    k = _bdot(xb, wk_ref[...])
    v = _bdot(xb, wv_ref[...])
    k_ref[...] = k
    v_ref[...] = v
    kbf_ref[...] = k.astype(BF16)
    vbf_ref[...] = v.astype(BF16)
    qi_ref[...] = (_bdot(xb, wqi_ref[...]) * (IDX_DIM ** -0.5)).astype(BF16)
    ki = _ln(_bdot(xb, wki_ref[...])[:, 0:IDX_DIM], kg_ref[...], kb_ref[...])
    kiln_ref[...] = ki
    kibf_ref[...] = ki.astype(BF16)
    wi_ref[...] = _bdot(xb, wwi_ref[...]) * (IDX_HEADS ** -0.5)


def dsa_proj(x, w_in, kln_g, kln_b, tm):
    t, d = x.shape
    hd = AT_HEADS * AT_DH
    qd = IDX_HEADS * IDX_DIM
    wb = w_in.astype(BF16)
    pad = lambda w: jnp.pad(w, ((0, 0), (0, LANES - w.shape[1])))
    ws = [wb[:, 0:hd], wb[:, hd:2 * hd], wb[:, 2 * hd:3 * hd], wb[:, 3 * hd:3 * hd + qd],
          pad(wb[:, 3 * hd + qd:3 * hd + qd + IDX_DIM]), pad(wb[:, 3 * hd + qd + IDX_DIM:])]
    row = lambda n, dt: (pl.BlockSpec((tm, n), lambda i: (i, 0)), jax.ShapeDtypeStruct((t, n), dt))
    outs = [row(hd, BF16), row(hd, F32), row(hd, F32), row(hd, BF16), row(hd, BF16), row(qd, BF16),
            row(IDX_DIM, F32), row(IDX_DIM, BF16), row(LANES, F32)]
    return pl.pallas_call(
        _dsa_proj_kernel,
        grid=(t // tm,),
        in_specs=[pl.BlockSpec((tm, d), lambda i: (i, 0))] + [_const(w.shape) for w in ws]
                 + [_const((1, IDX_DIM)), _const((1, IDX_DIM))],
        out_specs=[o[0] for o in outs],
        out_shape=[o[1] for o in outs],
        compiler_params=_params("parallel"),
    )(x, *ws, kln_g.reshape(1, IDX_DIM), kln_b.reshape(1, IDX_DIM))


def _t5_bucket(rel):
    n = jnp.maximum(rel, 0)
    exact = REL_BUCKETS // 2
    nf = jnp.maximum(n, exact).astype(F32)
    large = exact + (jnp.log(nf / exact) / math.log(REL_MAX_DIST / exact) * (REL_BUCKETS - exact)).astype(I32)
    large = jnp.minimum(large, REL_BUCKETS - 1)
    return jnp.where(n < exact, n, large)


def _bias_lookup(bucket, rb_ref, h):
    out = jnp.zeros(bucket.shape, F32)
    for bk in range(REL_BUCKETS):
        out = jnp.where(bucket == bk, rb_ref[bk, h], out)
    return out


def _dsa_bias_kernel(rb_ref, tile_ref, far_ref, samp_ref, *, past):
    c = lax.broadcasted_iota(I32, tile_ref.shape[1:], 0)
    i = lax.broadcasted_iota(I32, tile_ref.shape[1:], 1)
    bt = _t5_bucket(i - c + DSA_TQ)
    bs = _t5_bucket(past - lax.broadcasted_iota(I32, (1, samp_ref.shape[1]), 1))
    for h in range(AT_HEADS):
        tile_ref[h] = _bias_lookup(bt, rb_ref, h) * LOG2E
        far_ref[h] = jnp.full(far_ref.shape[1:], rb_ref[REL_BUCKETS - 1, h] * LOG2E, F32)
        samp_ref[h:h + 1, :] = _bias_lookup(bs, rb_ref, h) * LOG2E


def dsa_bias(rel_bias, past):
    return pl.pallas_call(
        functools.partial(_dsa_bias_kernel, past=past),
        in_specs=[pl.BlockSpec(memory_space=pltpu.SMEM)],
        out_shape=[jax.ShapeDtypeStruct((AT_HEADS, 2 * DSA_TQ, DSA_TQ), F32),
                   jax.ShapeDtypeStruct((AT_HEADS, 1, LANES), F32),
                   jax.ShapeDtypeStruct((AT_HEADS, past + PAGE), F32)],
    )(rel_bias)


def _order_key(s):
    bits = pltpu.bitcast(s, I32)
    return bits ^ ((bits >> 31) & INT_MAX)


def _count(key_scr, nck, pred):
    cw, nq = key_scr.shape[1], key_scr.shape[2]

    def body(c, acc):
        kk = key_scr[c]
        parts = [jnp.where(pred(kk[g * SUBLANES:(g + 1) * SUBLANES], c, g * SUBLANES), 1.0, 0.0)
                 for g in range(cw // SUBLANES)]
        while len(parts) > 1:
            parts = [parts[i] + parts[i + 1] for i in range(0, len(parts), 2)]
        return acc + parts[0]

    acc = lax.fori_loop(0, nck, body, jnp.zeros((SUBLANES, nq), F32))
    return jnp.sum(acc, axis=0, keepdims=True)


def _topk_select(key_scr, cut_scr, nck, n_sel, idx_bits, emit):
    cw, nq = key_scr.shape[1], key_scr.shape[2]
    nf = float(n_sel)
    kidx = lax.broadcasted_iota(I32, (cw, nq), 0)
    sidx = lax.broadcasted_iota(I32, (SUBLANES, nq), 0)

    cnt = _count(key_scr, nck, lambda kk, c, r: kk >= 0)
    thr = jnp.where(cnt >= nf, 0, INT_MIN).astype(I32)

    def bit_body(i, thr):
        trial = thr | jnp.left_shift(jnp.int32(1), 30 - i)
        cnt = _count(key_scr, nck, lambda kk, c, r: kk >= trial)
        return jnp.where(cnt >= nf, trial, thr)

    thr = lax.fori_loop(0, 31, bit_body, thr)
    cnt_ge = _count(key_scr, nck, lambda kk, c, r: kk >= thr)
    need = nf - _count(key_scr, nck, lambda kk, c, r: kk > thr)

    cut_scr[...] = jnp.full(cut_scr.shape, INT_MAX, I32)

    @pl.when(jnp.max(cnt_ge) > nf)
    def _():
        def tie_body(i, rr):
            t = rr | jnp.left_shift(jnp.int32(1), idx_bits - 1 - i)
            below = _count(key_scr, nck,
                           lambda kk, c, r: jnp.where(kk == thr, c * cw + r + sidx, INT_MAX) < t)
            return jnp.where(below <= need - 1.0, t, rr)

        rr = lax.fori_loop(0, idx_bits, tie_body, jnp.zeros((1, nq), I32))
        cut_scr[...] = jnp.broadcast_to(jnp.where(cnt_ge > nf, rr, INT_MAX), cut_scr.shape)

    cut = cut_scr[0:1, :]

    def sel_body(c, carry):
        kk = key_scr[c]
        tie = jnp.where(c * cw + kidx <= cut, 1, 0)
        key_scr[c] = emit(c, jnp.where(kk > thr, 1, jnp.where(kk == thr, tie, 0)) > 0)
        return carry

    lax.fori_loop(0, nck, sel_body, 0)


def _dsa_prompt_kernel(x_ref, qt_ref, qit_ref, wit_ref, ki_ref, k_ref, vt_ref, tile_ref, far_ref,
                       wout_ref, g_ref, b_ref, o_ref, key_scr, cut_scr, m_scr, l_scr, acc_scr,
                       *, alpha, n_sel, idx_bits):
    tq = DSA_TQ
    ck = DSA_CK
    qb = pl.program_id(1)
    q0 = qb * tq
    nck = (q0 + tq + ck - 1) // ck
    nfar = jnp.maximum(q0 - (REL_MAX_DIST - 1), 0) // ck
    kloc = lax.broadcasted_iota(I32, (ck, tq), 0)
    qpos = q0 + lax.broadcasted_iota(I32, (ck, tq), 1)

    wit = wit_ref[0, 0]

    def score_body(c, carry):
        kic = ki_ref[0, pl.ds(pl.multiple_of(c * ck, ck), ck), :]
        s = None
        for h in range(IDX_HEADS):
            dots = _bdot(kic, qit_ref[0, 0, h])
            t = wit[h:h + 1, :] * jnp.maximum(dots, 0.0)
            s = t if s is None else s + t
        s = jnp.where(c * ck + kloc <= qpos, s, NEG_BIG)
        key_scr[c] = _order_key(s)
        return carry

    lax.fori_loop(0, nck, score_body, 0)

    def mask_bits(c, selected):
        keep = jnp.where(c * ck + kloc <= qpos, 0.0, NEG_BIG)
        return pltpu.bitcast(jnp.where(selected, keep, NEG_BIG), I32)

    _topk_select(key_scr, cut_scr, nck, n_sel, idx_bits, mask_bits)

    m_scr[...] = jnp.full(m_scr.shape, NEG_BIG, F32)
    l_scr[...] = jnp.zeros(l_scr.shape, F32)
    acc_scr[...] = jnp.zeros(acc_scr.shape, F32)
    hsl = [slice(h * AT_DH, (h + 1) * AT_DH) for h in range(AT_HEADS)]
    ones = jnp.ones((SUBLANES, ck), BF16)

    def logits(c):
        k0 = pl.multiple_of(c * ck, ck)
        return jnp.stack([_bdot(k_ref[0, pl.ds(k0, ck), hsl[h]], qt_ref[0, 0, h]) for h in range(AT_HEADS)])

    def att_chunk(c, s_cur, near):
        s_nxt = logits(jnp.minimum(c + 1, nck - 1))
        k0 = c * ck
        madd = pltpu.bitcast(key_scr[c], F32)
        for h in range(AT_HEADS):
            s = s_cur[h] + madd
            m = m_scr[h][0:1, :]
            if near:
                bias = []
                for j in range(ck // LANES):
                    delta = q0 - (k0 + j * LANES)
                    bias.append(jnp.where(delta == 0, tile_ref[h, tq:2 * tq, :],
                                          jnp.where(delta == tq, tile_ref[h, 0:tq, :],
                                                    jnp.broadcast_to(far_ref[h], (LANES, tq)))))
                s = s + jnp.concatenate(bias, axis=0)
                mn = jnp.maximum(m, jnp.max(s, axis=0, keepdims=True))
                p = jnp.exp2(s - mn)
            else:
                fb = far_ref[h][:, 0:1]
                mn = jnp.maximum(m, jnp.max(s, axis=0, keepdims=True) + fb)
                p = jnp.exp2(s - (mn - fb))
            a = jnp.exp2(m - mn)
            pb = p.astype(BF16)
            l_scr[h] = a * l_scr[h] + _bdot(ones, pb)
            acc_scr[h] = a * acc_scr[h] + _bdot(vt_ref[0, h, c], pb)
            m_scr[h] = jnp.broadcast_to(mn, (SUBLANES, tq))
        return s_nxt

    s0 = logits(0)
    s1 = lax.fori_loop(0, nfar, lambda c, s: att_chunk(c, s, False), s0)
    lax.fori_loop(nfar, nck, lambda c, s: att_chunk(c, s, True), s1)
    att = jnp.concatenate([(acc_scr[h] / l_scr[h][0:1, :]).T for h in range(AT_HEADS)], axis=1)
    y = _bdot(att.astype(BF16), wout_ref[...])
    o_ref[0] = _ln(alpha * x_ref[0] + y, g_ref[...], b_ref[...])


def dsa_prompt(x, qb, qi, wi, kibf, kbf, vbf, tile, far, w_out, g, b, alpha):
    n, l, d = x.shape
    n_sel = min(TOPK_MAX, l // 4)
    nck_max = (l + DSA_CK - 1) // DSA_CK
    idx_bits = max(1, int(nck_max * DSA_CK - 1).bit_length())
    nqb = l // DSA_TQ
    blocked_t = lambda a, w: a.reshape(n, nqb, DSA_TQ, -1, w).transpose(0, 1, 3, 4, 2)
    qt = blocked_t(qb, AT_DH)
    qit = blocked_t(qi, IDX_DIM)
    wit = wi[:, :, 0:IDX_HEADS].reshape(n, nqb, DSA_TQ, IDX_HEADS).transpose(0, 1, 3, 2)
    vt = vbf.reshape(n, nck_max, DSA_CK, AT_HEADS, AT_DH).transpose(0, 3, 1, 4, 2)
    tok = lambda w: pl.BlockSpec((1, DSA_TQ, w), lambda s, t: (s, t, 0))
    blk = lambda a: pl.BlockSpec((1, 1) + a.shape[2:], lambda s, t: (s, t) + (0,) * (a.ndim - 2))
    seq = lambda a: pl.BlockSpec((1,) + a.shape[1:], lambda s, t: (s,) + (0,) * (a.ndim - 1),
                                 pipeline_mode=pl.Buffered(1))
    return pl.pallas_call(
        functools.partial(_dsa_prompt_kernel, alpha=alpha, n_sel=n_sel, idx_bits=idx_bits),
        grid=(n, nqb),
        in_specs=[tok(d), blk(qt), blk(qit), blk(wit), seq(kibf), seq(kbf), seq(vt),
                  _const(tile.shape), _const(far.shape), _const(w_out.shape), _const((1, d)), _const((1, d))],
        out_specs=tok(d),
        out_shape=jax.ShapeDtypeStruct((n, l, d), F32),
        scratch_shapes=[pltpu.VMEM((nck_max, DSA_CK, DSA_TQ), I32), pltpu.VMEM((SUBLANES, DSA_TQ), I32),
                        pltpu.VMEM((AT_HEADS, SUBLANES, DSA_TQ), F32), pltpu.VMEM((AT_HEADS, SUBLANES, DSA_TQ), F32),
                        pltpu.VMEM((AT_HEADS, AT_DH, DSA_TQ), F32)],
        compiler_params=_params("parallel", "arbitrary"),
    )(x, qt, qit, wit, kibf, kbf, vt, tile, far, w_out, g, b)


def _dsa_sample_score_kernel(pt_ref, qi_ref, wi_ref, *refs):
    o_ref = refs[-1]
    for p, kidx_ref in enumerate(refs[:-1]):
        kp = kidx_ref[0, 0].astype(BF16)
        dots = lax.dot_general(qi_ref[0], kp, NT, preferred_element_type=F32)
        o_ref[0, :, p * PAGE:(p + 1) * PAGE] = jnp.sum(wi_ref[0] * jnp.maximum(dots, 0.0), axis=0, keepdims=True)


def _dsa_sample_select_kernel(sc_ref, qi_ref, ki_ref, wi_ref, sel_ref, key_scr, cut_scr, *, n_sel, idx_bits):
    nb = sc_ref.shape[1]
    pw = PAGE
    npg = sc_ref.shape[0] // pw
    for c in range(npg):
        key_scr[c] = _order_key(sc_ref[c * pw:(c + 1) * pw, :])
    qi = qi_ref[...].astype(F32)
    ki = ki_ref[...].astype(F32)
    wi = wi_ref[...]
    s = jnp.zeros((1, nb), F32)
    for h in range(IDX_HEADS):
        dot = jnp.sum(qi[h * IDX_DIM:(h + 1) * IDX_DIM, :] * ki, axis=0, keepdims=True)
        s = s + wi[h:h + 1, :] * jnp.maximum(dot, 0.0)
    row = lax.broadcasted_iota(I32, (pw, nb), 0)
    key_scr[npg] = _order_key(jnp.where(row == 0, s, NEG_BIG))
    _topk_select(key_scr, cut_scr, npg + 1, n_sel, idx_bits, lambda c, selected: jnp.where(selected, 1, 0))
    for c in range(npg + 1):
        sel_ref[c * pw:(c + 1) * pw, :] = key_scr[c]


def _dsa_sample_attn_kernel(pt_ref, q_ref, sel_ref, b8_ref, sb_ref, kn_ref, vn_ref, *refs):
    npg = (len(refs) - 1) // 2
    k_refs, v_refs, o_ref = refs[:npg], refs[npg:2 * npg], refs[-1]
    rows = PAGE * AT_HEADS
    past = npg * PAGE
    q = q_ref[0]
    s = jnp.concatenate([lax.dot_general(q, kr[0, 0].astype(BF16), NT, preferred_element_type=F32)
                         for kr in k_refs], axis=1)
    ok = jnp.broadcast_to(sel_ref[0, :, 0:npg * rows], s.shape) > 0
    s = jnp.where(ok, s + b8_ref[:, 0:npg * rows], NEG_BIG)
    sn = jnp.sum(q.astype(F32) * kn_ref[0].astype(F32), axis=1, keepdims=True) + sb_ref[:, past:past + 1]
    okn = jnp.broadcast_to(sel_ref[0, :, npg * rows:npg * rows + 1], sn.shape) > 0
    sn = jnp.where(okn, sn, NEG_BIG)
    m = jnp.maximum(jnp.max(s, axis=1, keepdims=True), sn)
    p = jnp.exp2(s - m)
    pn = jnp.exp2(sn - m)
    l = jnp.sum(p, axis=1, keepdims=True) + pn
    acc = pn.astype(BF16).astype(F32) * vn_ref[0].astype(F32)
    for i, vr in enumerate(v_refs):
        acc = acc + _bdot(p[:, i * rows:(i + 1) * rows].astype(BF16), vr[0, 0].astype(BF16))
    o_ref[0] = acc / l


def dsa_sample(x, j, qb, qib, kiln, wi, kbf, vbf, cache_k, cache_v, cache_kidx, page_table, sbias,
               w_out, g, b, alpha):
    n, d = x.shape
    npg = page_table.shape[1]
    past = npg * PAGE
    n_sel = min(TOPK_MAX, (past + 1) // 4)
    idx_bits = int((npg + 1) * PAGE - 1).bit_length()
    pt = page_table.reshape(-1)
    npool = cache_k.shape[1]

    def paged(shape):
        return [pl.BlockSpec((1, 1) + shape, lambda i, t, p=p: (j, t[i * npg + p], 0, 0)) for p in range(npg)]

    scores = pl.pallas_call(
        _dsa_sample_score_kernel,
        grid_spec=pltpu.PrefetchScalarGridSpec(
            num_scalar_prefetch=1, grid=(n,),
            in_specs=[pl.BlockSpec((1, IDX_HEADS, IDX_DIM), lambda i, t: (i, 0, 0)),
                      pl.BlockSpec((1, IDX_HEADS, 1), lambda i, t: (i, 0, 0))] + paged((PAGE, IDX_DIM)),
            out_specs=pl.BlockSpec((1, 1, past), lambda i, t: (i, 0, 0))),
        out_shape=jax.ShapeDtypeStruct((n, 1, past), F32),
        compiler_params=_params("parallel"),
    )(pt, qib.reshape(n, IDX_HEADS, IDX_DIM), wi[:, 0:IDX_HEADS].reshape(n, IDX_HEADS, 1),
      *([cache_kidx] * npg))

    sel = pl.pallas_call(
        functools.partial(_dsa_sample_select_kernel, n_sel=n_sel, idx_bits=idx_bits),
        out_shape=jax.ShapeDtypeStruct((past + PAGE, n), I32),
        scratch_shapes=[pltpu.VMEM((npg + 1, PAGE, n), I32), pltpu.VMEM((SUBLANES, n), I32)],
        compiler_params=pltpu.CompilerParams(vmem_limit_bytes=VMEM_LIMIT),
    )(scores.reshape(n, past).T, qib.T, kiln.astype(BF16).T, wi[:, 0:IDX_HEADS].T).T

    rows = PAGE * AT_HEADS
    sel8 = jnp.repeat(sel, AT_HEADS, axis=1).reshape(n, 1, (npg + 1) * rows)
    rep = jnp.repeat(sbias, AT_HEADS, axis=1)
    own_head = (jnp.arange(rep.shape[1]) % AT_HEADS)[None, :] == jnp.arange(AT_HEADS)[:, None]
    b8 = jnp.where(own_head, rep, NEG_BIG)
    ck4 = cache_k.reshape(cache_k.shape[0], npool, rows, AT_DH)
    cv4 = cache_v.reshape(cache_v.shape[0], npool, rows, AT_DH)
    heads = pl.BlockSpec((1, AT_HEADS, AT_DH), lambda i, t: (i, 0, 0))
    o = pl.pallas_call(
        _dsa_sample_attn_kernel,
        grid_spec=pltpu.PrefetchScalarGridSpec(
            num_scalar_prefetch=1, grid=(n,),
            in_specs=[heads, pl.BlockSpec((1, 1, sel8.shape[2]), lambda i, t: (i, 0, 0)),
                      pl.BlockSpec(b8.shape, lambda i, t: (0, 0), pipeline_mode=pl.Buffered(1)),
                      pl.BlockSpec(sbias.shape, lambda i, t: (0, 0), pipeline_mode=pl.Buffered(1)),
                      heads, heads] + paged((rows, AT_DH)) + paged((rows, AT_DH)),
            out_specs=heads),
        out_shape=jax.ShapeDtypeStruct((n, AT_HEADS, AT_DH), F32),
        compiler_params=_params("parallel"),
    )(pt, qb.reshape(n, AT_HEADS, AT_DH), sel8, b8, sbias, kbf.reshape(n, AT_HEADS, AT_DH),
      vbf.reshape(n, AT_HEADS, AT_DH), *([ck4] * npg), *([cv4] * npg))
    return mm_res_ln(o.reshape(n, d), w_out, x, g, b, alpha, n)


MLP_TM = 512


def kernel(x_prompt, x_sample, state_hgrn, cache_k, cache_v, cache_kidx, page_table, state_s5_re, state_s5_im, hg_w_in, hg_lb, hg_gnorm, hg_w_out, dsa_w_in, dsa_kln_g, dsa_kln_b, dsa_w_out, rel_bias, s5_a_re, s5_a_im, s5_log_dt, s5_b_re, s5_b_im, s5_c_re, s5_c_im, s5_d, s5_w_glu, s5_b_glu, ln_mix_g, ln_mix_b, mlp_w1, mlp_w2, ln_ffn_g, ln_ffn_b):
    n, l, d = x_prompt.shape
    nb = x_sample.shape[0]
    depth = ln_mix_g.shape[0]
    alpha = (2 * depth) ** 0.25
    past = page_table.shape[1] * PAGE
    row = lambda a: a.reshape(1, -1)
    xp = x_prompt
    xs = x_sample.reshape(nb, d)
    hg_p, hg_s = [], []
    k_p, v_p, ki_p, k_s, v_s, ki_s = [], [], [], [], [], []
    s5r_p, s5i_p, s5r_s, s5i_s = [], [], [], []
    for i in range(depth):
        j = i // N_MIXERS
        kind = i % N_MIXERS
        g, b = row(ln_mix_g[i]), row(ln_mix_b[i])
        if kind == 0:
            w_in, w_out, gn = hg_w_in[j].astype(BF16), hg_w_out[j].astype(BF16), row(hg_gnorm[j])
            xp, st_p = hgrn_prompt(xp, w_in, hg_lb, gn, w_out, g, b, alpha, i)
            xs, st_s = hgrn_sample(xs, state_hgrn[j], w_in, hg_lb, gn, w_out, g, b, alpha, i)
            hg_p.append(st_p)
            hg_s.append(st_s)
        elif kind == 1:
            w_out = dsa_w_out[j].astype(BF16)
            tile, far, sbias = dsa_bias(rel_bias, past)
            qb, k, v, kbf, vbf, qib, kiln, kibf, wi = dsa_proj(xp.reshape(n * l, d), dsa_w_in[j],
                                                               dsa_kln_g[j], dsa_kln_b[j], MLP_TM)
            seq = lambda a: a.reshape(n, l, a.shape[1])
            xp = dsa_prompt(xp, seq(qb), seq(qib), seq(wi), seq(kibf), seq(kbf), seq(vbf), tile, far,
                            w_out, g, b, alpha)
            k_p.append(k.reshape(n, l, AT_HEADS, AT_DH))
            v_p.append(v.reshape(n, l, AT_HEADS, AT_DH))
            ki_p.append(kiln.reshape(n, l, IDX_DIM))
            qb, k, v, kbf, vbf, qib, kiln, kibf, wi = dsa_proj(xs, dsa_w_in[j], dsa_kln_g[j], dsa_kln_b[j], nb)
            xs = dsa_sample(xs, j, qb, qib, kiln, wi, kbf, vbf, cache_k, cache_v, cache_kidx, page_table,
                            sbias, w_out, g, b, alpha)
            k_s.append(k.reshape(nb, 1, AT_HEADS, AT_DH))
            v_s.append(v.reshape(nb, 1, AT_HEADS, AT_DH))
            ki_s.append(kiln.reshape(nb, 1, IDX_DIM))
        else:
            prm = s5_params(s5_a_re[j], s5_a_im[j], s5_log_dt[j], s5_b_re[j], s5_b_im[j], s5_c_re[j],
                            s5_c_im[j], s5_d[j], s5_w_glu[j], s5_b_glu[j])
            sg, sp = s5_a_re.shape[1], s5_a_re.shape[2]
            xp, hr, hi = s5_prompt(xp, prm, g, b, alpha)
            xs, hr2, hi2 = s5_sample(xs, state_s5_re[j].reshape(nb, sg * sp), state_s5_im[j].reshape(nb, sg * sp),
                                     prm, g, b, alpha)
            s5r_p.append(hr.reshape(n, sg, sp))
            s5i_p.append(hi.reshape(n, sg, sp))
            s5r_s.append(hr2.reshape(nb, sg, sp))
            s5i_s.append(hi2.reshape(nb, sg, sp))
        w1, w2 = mlp_w1[i].astype(BF16), mlp_w2[i].astype(BF16)
        fg, fb = row(ln_ffn_g[i]), row(ln_ffn_b[i])
        xp = mlp_res_ln(xp.reshape(n * l, d), w1, w2, fg, fb, alpha, MLP_TM).reshape(n, l, d)
        xs = mlp_res_ln(xs, w1, w2, fg, fb, alpha, nb)
    return (xp, xs.reshape(nb, 1, d), jnp.stack(hg_p), jnp.stack(hg_s), jnp.stack(k_p), jnp.stack(v_p),
            jnp.stack(ki_p), jnp.stack(k_s), jnp.stack(v_s), jnp.stack(ki_s), jnp.stack(s5r_p),
            jnp.stack(s5i_p), jnp.stack(s5r_s), jnp.stack(s5i_s))
```

```python
import functools
import math

import numpy as np
import jax
import jax.numpy as jnp
from jax import lax
from jax.experimental import pallas as pl
from jax.experimental.pallas import tpu as pltpu

F32 = jnp.float32
BF16 = jnp.bfloat16
I32 = jnp.int32

LN_EPS = 1e-5
RMS_EPS = 1e-6
NEG_BIG = -1e30
N_MIXERS = 3
HG_HEADS = 8
HG_DK = 128
AT_HEADS = 8
AT_DH = 128
IDX_HEADS = 8
IDX_DIM = 64
TOPK_MAX = 256
REL_BUCKETS = 32
REL_MAX_DIST = 128
S5_GROUP = 16
S5_STATE = 64
PAGE = 128

LANES = 128
SUBLANES = 8
VMEM_LIMIT = 56 * 1024 * 1024

NT = (((1,), (1,)), ((), ()))
TN = (((0,), (0,)), ((), ()))


def _params(*sem):
    return pltpu.CompilerParams(dimension_semantics=sem, vmem_limit_bytes=VMEM_LIMIT)


def _const(shape):
    nd = len(shape)
    return pl.BlockSpec(shape, lambda *_: (0,) * nd, pipeline_mode=pl.Buffered(1))


def _ln(z, g, b):
    mu = jnp.mean(z, axis=-1, keepdims=True)
    zc = z - mu
    var = jnp.mean(zc * zc, axis=-1, keepdims=True)
    return zc * lax.rsqrt(var + LN_EPS) * g + b


def _sigmoid(z):
    return 1.0 / (1.0 + jnp.exp(-z))


def _bdot(a, b):
    return jnp.dot(a, b, preferred_element_type=F32)


def _mlp_kernel(x_ref, w1_ref, w2_ref, g_ref, b_ref, o_ref, *, alpha, hc):
    x = x_ref[...]
    xb = x.astype(BF16)
    acc = jnp.zeros(x.shape, F32)
    for c in range(w1_ref.shape[1] // hc):
        h = _bdot(xb, w1_ref[:, c * hc:(c + 1) * hc])
        h = jnp.maximum(h, 0.0)
        acc = acc + _bdot((h * h).astype(BF16), w2_ref[c * hc:(c + 1) * hc, :])
    o_ref[...] = _ln(alpha * x + acc, g_ref[...], b_ref[...])


def mlp_res_ln(x, w1, w2, g, b, alpha, tm):
    t, d = x.shape
    f = w1.shape[1]
    return pl.pallas_call(
        functools.partial(_mlp_kernel, alpha=alpha, hc=1024),
        grid=(t // tm,),
        in_specs=[pl.BlockSpec((tm, d), lambda i: (i, 0)), _const((d, f)), _const((f, d)),
                  _const((1, d)), _const((1, d))],
        out_specs=pl.BlockSpec((tm, d), lambda i: (i, 0)),
        out_shape=jax.ShapeDtypeStruct((t, d), F32),
        compiler_params=_params("parallel"),
    )(x, w1, w2, g, b)


def _mm_kernel(x_ref, w_ref, o_ref):
    o_ref[...] = _bdot(x_ref[...].astype(BF16), w_ref[...])


def mm(x, w, tm):
    t, d = x.shape
    n = w.shape[1]
    return pl.pallas_call(
        _mm_kernel,
        grid=(t // tm,),
        in_specs=[pl.BlockSpec((tm, d), lambda i: (i, 0)), _const((d, n))],
        out_specs=pl.BlockSpec((tm, n), lambda i: (i, 0)),
        out_shape=jax.ShapeDtypeStruct((t, n), F32),
        compiler_params=_params("parallel"),
    )(x, w)


def _mm_res_ln_kernel(a_ref, w_ref, x_ref, g_ref, b_ref, o_ref, *, alpha):
    y = _bdot(a_ref[...].astype(BF16), w_ref[...])
    o_ref[...] = _ln(alpha * x_ref[...] + y, g_ref[...], b_ref[...])


def mm_res_ln(a, w, x, g, b, alpha, tm):
    t, k = a.shape
    d = w.shape[1]
    return pl.pallas_call(
        functools.partial(_mm_res_ln_kernel, alpha=alpha),
        grid=(t // tm,),
        in_specs=[pl.BlockSpec((tm, k), lambda i: (i, 0)), _const((k, d)),
                  pl.BlockSpec((tm, d), lambda i: (i, 0)), _const((1, d)), _const((1, d))],
        out_specs=pl.BlockSpec((tm, d), lambda i: (i, 0)),
        out_shape=jax.ShapeDtypeStruct((t, d), F32),
        compiler_params=_params("parallel"),
    )(a, w, x, g, b)


HG_C = 128
HG_LEVELS = 7


def _hgrn_constants():
    c = HG_C
    t = np.arange(c)[:, None]
    u = np.arange(c)[None, :]
    mats = [(u <= t), (u > t)]
    for l in range(HG_LEVELS):
        blk = 2 << l
        mid = (t // blk) * blk + blk // 2
        second = t >= mid
        mats.append(np.where(second, (u >= mid) & (u <= t), (u > t) & (u < mid)))
    m = np.concatenate(mats, axis=0).astype(np.float32)
    x = t ^ u
    lv = np.where(u < t, np.floor(np.log2(np.maximum(x, 1))).astype(np.int32),
                  np.where(u == t, HG_LEVELS, -1)).astype(np.int32)
    return np.concatenate([m, m, m], axis=1), lv


def _lower_bound(lb_ref, layer):
    rows = [lb_ref[i] for i in range(lb_ref.shape[0])]
    mx = rows[0]
    for r in rows[1:]:
        mx = jnp.maximum(mx, r)
    ex = [jnp.exp(r - mx) for r in rows]
    tot = ex[0]
    for e in ex[1:]:
        tot = tot + e
    num = jnp.zeros_like(tot)
    for i in range(1, layer + 1):
        num = num + ex[i]
    return num / tot


def _split3(x):
    hi = x.astype(BF16)
    r1 = x - hi.astype(F32)
    mid = r1.astype(BF16)
    lo = (r1 - mid.astype(F32)).astype(BF16)
    return hi, mid, lo


def _hgrn_prompt_kernel(x_ref, win_ref, lb_ref, gn_ref, wout_ref, g_ref, b_ref, mst_ref, lv_ref,
                        o_ref, st_ref, og_scr, *, alpha, layer):
    c = HG_C
    dk = HG_DK

    @pl.when(pl.program_id(1) == 0)
    def _():
        st_ref[...] = jnp.zeros(st_ref.shape, F32)

    x = x_ref[0]
    d = x.shape[1]
    proj = _bdot(x.astype(BF16), win_ref[...])
    qp = proj[:, 0:d]
    fz = proj[:, d:2 * d]
    v = proj[:, 2 * d:3 * d]
    gp = proj[:, 3 * d:4 * d]
    lb = _lower_bound(lb_ref, layer)
    q = qp * _sigmoid(qp)
    sg = _sigmoid(fz)
    f = lb + (1.0 - lb) * sg
    k = (1.0 - lb) * (1.0 - sg)
    logf = jnp.log(f)
    gate = gp * _sigmoid(gp)
    hi, mid, lo = _split3(logf)
    ex = _bdot(mst_ref[...], jnp.concatenate([hi, mid, lo], axis=0))

    lv = lv_ref[...]
    row = lax.broadcasted_iota(I32, (c, dk), 0)
    for h in range(HG_HEADS):
        hs = slice(h * dk, (h + 1) * dk)
        qh = q[:, hs]
        kh = k[:, hs]
        vh = v[:, hs].astype(BF16)
        st = st_ref[0, h]
        eb = jnp.exp(ex[0:c, hs])
        o = lax.dot_general((qh * eb).astype(BF16), st.astype(BF16), NT, preferred_element_type=F32)
        att = jnp.zeros((c, c), F32)
        for l in range(HG_LEVELS):
            e = jnp.exp(ex[(2 + l) * c:(3 + l) * c, hs])
            second = ((row >> l) & 1) == 1
            w = (jnp.where(second, qh, kh) * e).astype(BF16)
            a = lax.dot_general(w, w, NT, preferred_element_type=F32)
            att = jnp.where(lv == l, a, att)
        a0 = lax.dot_general(qh.astype(BF16), kh.astype(BF16), NT, preferred_element_type=F32)
        att = jnp.where(lv == HG_LEVELS, a0, att)
        o = o + _bdot(att.astype(BF16), vh)
        kd = (kh * jnp.exp(ex[c:2 * c, hs])).astype(BF16)
        st_ref[0, h] = eb[c - 1:c, :] * st + lax.dot_general(vh, kd, TN, preferred_element_type=F32)
        ms = jnp.mean(o * o, axis=-1, keepdims=True)
        og_scr[:, hs] = o * lax.rsqrt(ms + RMS_EPS) * gn_ref[:, hs] * gate[:, hs]
    y = _bdot(og_scr[...].astype(BF16), wout_ref[...])
    o_ref[0] = _ln(alpha * x + y, g_ref[...], b_ref[...])


def hgrn_prompt(x, w_in, hg_lb, gnorm, w_out, g, b, alpha, layer):
    n, l, d = x.shape
    mst, lv = _hgrn_constants()
    nl = hg_lb.shape[0]
    out, st = pl.pallas_call(
        functools.partial(_hgrn_prompt_kernel, alpha=alpha, layer=layer),
        grid=(n, l // HG_C),
        in_specs=[pl.BlockSpec((1, HG_C, d), lambda s, c: (s, c, 0)),
                  _const((d, 4 * d)), _const((nl, 1, d)), _const((1, d)), _const((d, d)),
                  _const((1, d)), _const((1, d)), _const(mst.shape), _const(lv.shape)],
        out_specs=[pl.BlockSpec((1, HG_C, d), lambda s, c: (s, c, 0)),
                   pl.BlockSpec((1, HG_HEADS, HG_DK, HG_DK), lambda s, c: (s, 0, 0, 0))],
        out_shape=[jax.ShapeDtypeStruct((n, l, d), F32),
                   jax.ShapeDtypeStruct((n, HG_HEADS, HG_DK, HG_DK), F32)],
        scratch_shapes=[pltpu.VMEM((HG_C, d), F32)],
        compiler_params=_params("parallel", "arbitrary"),
    )(x, w_in, hg_lb.reshape(nl, 1, d), gnorm, w_out, g, b,
      jnp.asarray(mst, BF16), jnp.asarray(lv))
    return out, jnp.swapaxes(st, 2, 3)


def _hgrn_sample_kernel(qt_ref, fzt_ref, lbt_ref, v_ref, gp_ref, gn_ref, s0_ref, o_ref, s1_ref, *, layer):
    dk = HG_DK
    lbt = _lower_bound(lbt_ref, layer)
    qt = qt_ref[0]
    q = qt * _sigmoid(qt)
    sg = _sigmoid(fzt_ref[0])
    f = lbt + (1.0 - lbt) * sg
    k = (1.0 - lbt) * (1.0 - sg)
    gp = gp_ref[0]
    gate = gp * _sigmoid(gp)
    for h in range(HG_HEADS):
        hs = slice(h * dk, (h + 1) * dk)
        s_new = f[:, h:h + 1] * s0_ref[0, h] + k[:, h:h + 1] * v_ref[0, :, hs]
        s1_ref[0, h] = s_new
        o = jnp.sum(s_new * q[:, h:h + 1], axis=0, keepdims=True)
        ms = jnp.mean(o * o, axis=-1, keepdims=True)
        o_ref[0, :, hs] = o * lax.rsqrt(ms + RMS_EPS) * gn_ref[:, hs] * gate[:, hs]


def hgrn_sample(x, s0, w_in, hg_lb, gnorm, w_out, g, b, alpha, layer):
    n, d = x.shape
    nl = hg_lb.shape[0]
    proj = mm(x, w_in, n)
    to_cols = lambda a: a.reshape(n, HG_HEADS, HG_DK).transpose(0, 2, 1)
    lbt = hg_lb.reshape(nl, HG_HEADS, HG_DK).transpose(0, 2, 1)
    row3 = lambda a: a.reshape(n, 1, d)
    o, s1 = pl.pallas_call(
        functools.partial(_hgrn_sample_kernel, layer=layer),
        grid=(n,),
        in_specs=[pl.BlockSpec((1, HG_DK, HG_HEADS), lambda i: (i, 0, 0)),
                  pl.BlockSpec((1, HG_DK, HG_HEADS), lambda i: (i, 0, 0)),
                  _const((nl, HG_DK, HG_HEADS)),
                  pl.BlockSpec((1, 1, d), lambda i: (i, 0, 0)),
                  pl.BlockSpec((1, 1, d), lambda i: (i, 0, 0)),
                  _const((1, d)),
                  pl.BlockSpec((1, HG_HEADS, HG_DK, HG_DK), lambda i: (i, 0, 0, 0))],
        out_specs=[pl.BlockSpec((1, 1, d), lambda i: (i, 0, 0)),
                   pl.BlockSpec((1, HG_HEADS, HG_DK, HG_DK), lambda i: (i, 0, 0, 0))],
        out_shape=[jax.ShapeDtypeStruct((n, 1, d), F32),
                   jax.ShapeDtypeStruct(s0.shape, F32)],
        compiler_params=_params("parallel"),
    )(to_cols(proj[:, 0:d]), to_cols(proj[:, d:2 * d]), lbt, row3(proj[:, 2 * d:3 * d]),
      row3(proj[:, 3 * d:4 * d]), gnorm, s0)
    return mm_res_ln(o.reshape(n, d), w_out, x, g, b, alpha, n), s1


S5_TC = 128
S5_SLABS = 8
S5_KS = 3


def _cmul(ar, ai, br, bi):
    return ar * br - ai * bi, ar * bi + ai * br


def _s5_prep_kernel(are_ref, aim_ref, ldt_ref, are2_ref, aim2_ref, ldt2_ref, bre_ref, bim_ref,
                    pwre_ref, pwim_ref, ckre_ref, ckim_ref, bbre_ref, bbim_ref):
    dt = jnp.exp(ldt_ref[...])
    mag = jnp.exp(dt * are_ref[...])
    a_r = mag * jnp.cos(dt * aim_ref[...])
    a_i = mag * jnp.sin(dt * aim_ref[...])
    pr, pi_ = [a_r], [a_i]
    for _ in range(SUBLANES - 1):
        nr, ni = _cmul(pr[-1], pi_[-1], a_r, a_i)
        pr.append(nr)
        pi_.append(ni)
    pwre_ref[...] = jnp.concatenate(pr, axis=0)
    pwim_ref[...] = jnp.concatenate(pi_, axis=0)
    rows = lax.broadcasted_iota(I32, pwre_ref.shape, 0)
    for k in range(S5_KS):
        s = 1 << k
        ckre_ref[k] = jnp.where(rows >= s, pr[s - 1], 0.0)
        ckim_ref[k] = jnp.where(rows >= s, pi_[s - 1], 0.0)
    ar, ai = are2_ref[...], aim2_ref[...]
    dt2 = jnp.exp(ldt2_ref[...])
    mag2 = jnp.exp(dt2 * ar)
    nr = mag2 * jnp.cos(dt2 * ai) - 1.0
    ni = mag2 * jnp.sin(dt2 * ai)
    den = ar * ar + ai * ai
    z_re = ((nr * ar + ni * ai) / den)[:, None, :]
    z_im = ((ni * ar - nr * ai) / den)[:, None, :]
    br, bi = bre_ref[...], bim_ref[...]
    bbre_ref[...] = z_re * br - z_im * bi
    bbim_ref[...] = z_re * bi + z_im * br


def s5_prep(a_re, a_im, log_dt, b_re, b_im):
    g, p = a_re.shape
    k = b_re.shape[2]
    n = g * p
    ldt2 = jnp.broadcast_to(log_dt[:, None], (g, p))
    flat = lambda a: a.reshape(1, n)
    return pl.pallas_call(
        _s5_prep_kernel,
        out_shape=[jax.ShapeDtypeStruct((SUBLANES, n), F32), jax.ShapeDtypeStruct((SUBLANES, n), F32),
                   jax.ShapeDtypeStruct((S5_KS, SUBLANES, n), F32), jax.ShapeDtypeStruct((S5_KS, SUBLANES, n), F32),
                   jax.ShapeDtypeStruct((g, k, p), F32), jax.ShapeDtypeStruct((g, k, p), F32)],
    )(flat(a_re), flat(a_im), flat(ldt2), a_re, a_im, ldt2,
      b_re.transpose(0, 2, 1), b_im.transpose(0, 2, 1))


def _block_diag_slabs(w_gkp, rows_are_inputs):
    g, k, p = w_gkp.shape
    per = g // S5_SLABS
    w4 = w_gkp.reshape(S5_SLABS, per, k, p)
    eye = jnp.eye(per, dtype=w_gkp.dtype)
    if rows_are_inputs:
        return jnp.einsum('iakp,ab->iakbp', w4, eye).reshape(S5_SLABS, per * k, per * p)
    return jnp.einsum('iakp,ab->ibpak', w4, eye).reshape(S5_SLABS, per * p, per * k)


def _s5_tail(x, y, d_ref, wglu_ref, bglu_ref, g_ref, b_ref, alpha):
    d = x.shape[1]
    gl = jax.nn.gelu(y + d_ref[...] * x)
    u = _bdot(gl.astype(BF16), wglu_ref[...]) + bglu_ref[...]
    out = u[:, 0:d] * _sigmoid(u[:, d:2 * d])
    return _ln(alpha * x + out, g_ref[...], b_ref[...])


def _s5_prompt_kernel(x_ref, wbre_ref, wbim_ref, wcre_ref, wcim_ref, pwre_ref, pwim_ref, ckre_ref, ckim_ref,
                      d_ref, wglu_ref, bglu_ref, g_ref, b_ref,
                      o_ref, hre_ref, him_ref, cre_scr, cim_scr, y_scr, *, alpha):
    tc = x_ref.shape[1]
    ng = tc // SUBLANES
    ci = pl.program_id(1)

    @pl.when(ci == 0)
    def _():
        cre_scr[...] = jnp.zeros(cre_scr.shape, F32)
        cim_scr[...] = jnp.zeros(cim_scr.shape, F32)

    x = x_ref[0]
    xb = x.astype(BF16)
    sw = wbre_ref.shape[2]
    for i in range(S5_SLABS):
        cs = slice(i * sw, (i + 1) * sw)
        xi = xb[:, i * LANES:(i + 1) * LANES]
        br = _bdot(xi, wbre_ref[i]).reshape(ng, SUBLANES, sw)
        bi = _bdot(xi, wbim_ref[i]).reshape(ng, SUBLANES, sw)
        for k in range(S5_KS):
            sr = pltpu.roll(br, 1 << k, axis=1)
            si = pltpu.roll(bi, 1 << k, axis=1)
            tr, ti = _cmul(ckre_ref[k, :, cs], ckim_ref[k, :, cs], sr, si)
            br, bi = br + tr, bi + ti
        pr, pi_ = pwre_ref[:, cs], pwim_ref[:, cs]
        hpr, hpi = cre_scr[:, cs], cim_scr[:, cs]
        hr_all, hi_all = [], []
        for m in range(ng):
            tr, ti = _cmul(pr, pi_, hpr, hpi)
            hr, hi = br[m] + tr, bi[m] + ti
            hr_all.append(hr)
            hi_all.append(hi)
            hpr = jnp.broadcast_to(hr[SUBLANES - 1:SUBLANES, :], hr.shape)
            hpi = jnp.broadcast_to(hi[SUBLANES - 1:SUBLANES, :], hi.shape)
        cre_scr[:, cs] = hpr
        cim_scr[:, cs] = hpi
        hr = jnp.concatenate(hr_all, axis=0).astype(BF16)
        hi = jnp.concatenate(hi_all, axis=0).astype(BF16)
        y_scr[:, i * LANES:(i + 1) * LANES] = _bdot(hr, wcre_ref[i]) - _bdot(hi, wcim_ref[i])
    o_ref[0] = _s5_tail(x, y_scr[...], d_ref, wglu_ref, bglu_ref, g_ref, b_ref, alpha)

    @pl.when(ci == pl.num_programs(1) - 1)
    def _():
        hre_ref[0] = cre_scr[0:1, :]
        him_ref[0] = cim_scr[0:1, :]


def _s5_weight_specs(wb, wc, pw, ck, d, wglu):
    return [_const(wb.shape), _const(wb.shape), _const(wc.shape), _const(wc.shape),
            _const(pw.shape), _const(pw.shape), _const(ck.shape), _const(ck.shape),
            _const((1, d)), _const(wglu.shape), _const((1, wglu.shape[1])), _const((1, d)), _const((1, d))]


def s5_prompt(x, prm, g, b, alpha):
    n, l, d = x.shape
    wb, wc, pw, ck, wglu = prm[0], prm[2], prm[4], prm[6], prm[9]
    ns = pw.shape[1]
    out, hre, him = pl.pallas_call(
        functools.partial(_s5_prompt_kernel, alpha=alpha),
        grid=(n, l // S5_TC),
        in_specs=[pl.BlockSpec((1, S5_TC, d), lambda s, c: (s, c, 0))] + _s5_weight_specs(wb, wc, pw, ck, d, wglu),
        out_specs=[pl.BlockSpec((1, S5_TC, d), lambda s, c: (s, c, 0)),
                   pl.BlockSpec((1, 1, ns), lambda s, c: (s, 0, 0)),
                   pl.BlockSpec((1, 1, ns), lambda s, c: (s, 0, 0))],
        out_shape=[jax.ShapeDtypeStruct((n, l, d), F32),
                   jax.ShapeDtypeStruct((n, 1, ns), F32), jax.ShapeDtypeStruct((n, 1, ns), F32)],
        scratch_shapes=[pltpu.VMEM((SUBLANES, ns), F32), pltpu.VMEM((SUBLANES, ns), F32),
                        pltpu.VMEM((S5_TC, d), F32)],
        compiler_params=_params("parallel", "arbitrary"),
    )(x, *prm, g, b)
    return out, hre, him


def _s5_sample_kernel(x_ref, h0re_ref, h0im_ref, wbre_ref, wbim_ref, wcre_ref, wcim_ref, pwre_ref, pwim_ref,
                      ckre_ref, ckim_ref, d_ref, wglu_ref, bglu_ref, g_ref, b_ref,
                      o_ref, hre_ref, him_ref, y_scr, *, alpha):
    x = x_ref[...]
    xb = x.astype(BF16)
    sw = wbre_ref.shape[2]
    for i in range(S5_SLABS):
        cs = slice(i * sw, (i + 1) * sw)
        xi = xb[:, i * LANES:(i + 1) * LANES]
        tr, ti = _cmul(pwre_ref[0:1, cs], pwim_ref[0:1, cs], h0re_ref[:, cs], h0im_ref[:, cs])
        hr = _bdot(xi, wbre_ref[i]) + tr
        hi = _bdot(xi, wbim_ref[i]) + ti
        hre_ref[:, cs] = hr
        him_ref[:, cs] = hi
        y_scr[:, i * LANES:(i + 1) * LANES] = (_bdot(hr.astype(BF16), wcre_ref[i])
                                               - _bdot(hi.astype(BF16), wcim_ref[i]))
    o_ref[...] = _s5_tail(x, y_scr[...], d_ref, wglu_ref, bglu_ref, g_ref, b_ref, alpha)


def s5_sample(x, h0re, h0im, prm, g, b, alpha):
    n, d = x.shape
    ns = h0re.shape[1]
    return pl.pallas_call(
        functools.partial(_s5_sample_kernel, alpha=alpha),
        out_shape=[jax.ShapeDtypeStruct((n, d), F32),
                   jax.ShapeDtypeStruct((n, ns), F32), jax.ShapeDtypeStruct((n, ns), F32)],
        scratch_shapes=[pltpu.VMEM((n, d), F32)],
        compiler_params=pltpu.CompilerParams(vmem_limit_bytes=VMEM_LIMIT),
    )(x, h0re, h0im, *prm, g, b)


def s5_params(a_re, a_im, log_dt, b_re, b_im, c_re, c_im, d_skip, w_glu, b_glu):
    pwre, pwim, ckre, ckim, bbre, bbim = s5_prep(a_re, a_im, log_dt, b_re, b_im)
    return (_block_diag_slabs(bbre, True).astype(BF16), _block_diag_slabs(bbim, True).astype(BF16),
            _block_diag_slabs(c_re, False).astype(BF16), _block_diag_slabs(c_im, False).astype(BF16),
            pwre, pwim, ckre, ckim, d_skip.reshape(1, -1), w_glu.astype(BF16), b_glu.reshape(1, -1))


DSA_TQ = 256
DSA_SUB = 128
DSA_CK = 512
INT_MIN = -2 ** 31
INT_MAX = 2 ** 31 - 1
LOG2E = math.log2(math.e)


def _dsa_proj_kernel(x_ref, wq_ref, wk_ref, wv_ref, wqi_ref, wki_ref, wwi_ref, kg_ref, kb_ref,
                     qb_ref, k_ref, v_ref, kbf_ref, vbf_ref, qi_ref, kiln_ref, kibf_ref, wi_ref):
    xb = x_ref[...].astype(BF16)
    qb_ref[...] = (_bdot(xb, wq_ref[...]) * (AT_DH ** -0.5 * LOG2E)).astype(BF16)
    k = _bdot(xb, wk_ref[...])
    v = _bdot(xb, wv_ref[...])
    k_ref[...] = k
    v_ref[...] = v
    kbf_ref[...] = k.astype(BF16)
    vbf_ref[...] = v.astype(BF16)
    qi_ref[...] = (_bdot(xb, wqi_ref[...]) * (IDX_DIM ** -0.5)).astype(BF16)
    ki = _ln(_bdot(xb, wki_ref[...])[:, 0:IDX_DIM], kg_ref[...], kb_ref[...])
    kiln_ref[...] = ki
    kibf_ref[...] = ki.astype(BF16)
    wi_ref[...] = _bdot(xb, wwi_ref[...]) * (IDX_HEADS ** -0.5)


def dsa_proj(x, w_in, kln_g, kln_b, tm):
    t, d = x.shape
    hd = AT_HEADS * AT_DH
    qd = IDX_HEADS * IDX_DIM
    wb = w_in.astype(BF16)
    pad = lambda w: jnp.pad(w, ((0, 0), (0, LANES - w.shape[1])))
    ws = [wb[:, 0:hd], wb[:, hd:2 * hd], wb[:, 2 * hd:3 * hd], wb[:, 3 * hd:3 * hd + qd],
          pad(wb[:, 3 * hd + qd:3 * hd + qd + IDX_DIM]), pad(wb[:, 3 * hd + qd + IDX_DIM:])]
    row = lambda n, dt: (pl.BlockSpec((tm, n), lambda i: (i, 0)), jax.ShapeDtypeStruct((t, n), dt))
    outs = [row(hd, BF16), row(hd, F32), row(hd, F32), row(hd, BF16), row(hd, BF16), row(qd, BF16),
            row(IDX_DIM, F32), row(IDX_DIM, BF16), row(LANES, F32)]
    return pl.pallas_call(
        _dsa_proj_kernel,
        grid=(t // tm,),
        in_specs=[pl.BlockSpec((tm, d), lambda i: (i, 0))] + [_const(w.shape) for w in ws]
                 + [_const((1, IDX_DIM)), _const((1, IDX_DIM))],
        out_specs=[o[0] for o in outs],
        out_shape=[o[1] for o in outs],
        compiler_params=_params("parallel"),
    )(x, *ws, kln_g.reshape(1, IDX_DIM), kln_b.reshape(1, IDX_DIM))


def _t5_bucket(rel):
    n = jnp.maximum(rel, 0)
    exact = REL_BUCKETS // 2
    nf = jnp.maximum(n, exact).astype(F32)
    large = exact + (jnp.log(nf / exact) / math.log(REL_MAX_DIST / exact) * (REL_BUCKETS - exact)).astype(I32)
    large = jnp.minimum(large, REL_BUCKETS - 1)
    return jnp.where(n < exact, n, large)


def _bias_lookup(bucket, rb_ref, h):
    out = jnp.zeros(bucket.shape, F32)
    for bk in range(REL_BUCKETS):
        out = jnp.where(bucket == bk, rb_ref[bk, h], out)
    return out


def _dsa_bias_kernel(rb_ref, tile_ref, far_ref, samp_ref, *, past):
    c = lax.broadcasted_iota(I32, tile_ref.shape[1:], 0)
    i = lax.broadcasted_iota(I32, tile_ref.shape[1:], 1)
    bt = _t5_bucket(i - c + DSA_SUB)
    bs = _t5_bucket(past - lax.broadcasted_iota(I32, (1, samp_ref.shape[1]), 1))
    for h in range(AT_HEADS):
        tile_ref[h] = _bias_lookup(bt, rb_ref, h) * LOG2E
        far_ref[h] = jnp.full(far_ref.shape[1:], rb_ref[REL_BUCKETS - 1, h] * LOG2E, F32)
        samp_ref[h:h + 1, :] = _bias_lookup(bs, rb_ref, h) * LOG2E


def dsa_bias(rel_bias, past):
    return pl.pallas_call(
        functools.partial(_dsa_bias_kernel, past=past),
        in_specs=[pl.BlockSpec(memory_space=pltpu.SMEM)],
        out_shape=[jax.ShapeDtypeStruct((AT_HEADS, 2 * DSA_SUB, DSA_SUB), F32),
                   jax.ShapeDtypeStruct((AT_HEADS, 1, LANES), F32),
                   jax.ShapeDtypeStruct((AT_HEADS, past + PAGE), F32)],
    )(rel_bias)


def _order_key(s):
    bits = pltpu.bitcast(s, I32)
    return bits ^ ((bits >> 31) & INT_MAX)


def _count(key_scr, nck, pred):
    cw, nq = key_scr.shape[1], key_scr.shape[2]

    def body(c, acc):
        kk = key_scr[c]
        parts = [jnp.where(pred(kk[g * SUBLANES:(g + 1) * SUBLANES], c, g * SUBLANES), 1.0, 0.0)
                 for g in range(cw // SUBLANES)]
        while len(parts) > 1:
            parts = [parts[i] + parts[i + 1] for i in range(0, len(parts), 2)]
        return acc + parts[0]

    acc = lax.fori_loop(0, nck, body, jnp.zeros((SUBLANES, nq), F32))
    return jnp.sum(acc, axis=0, keepdims=True)


def _topk_select(key_scr, cut_scr, nck, n_sel, idx_bits, emit):
    cw, nq = key_scr.shape[1], key_scr.shape[2]
    nf = float(n_sel)
    kidx = lax.broadcasted_iota(I32, (cw, nq), 0)
    sidx = lax.broadcasted_iota(I32, (SUBLANES, nq), 0)

    cnt = _count(key_scr, nck, lambda kk, c, r: kk >= 0)
    thr = jnp.where(cnt >= nf, 0, INT_MIN).astype(I32)

    def bit_body(i, thr):
        trial = thr | jnp.left_shift(jnp.int32(1), 30 - i)
        cnt = _count(key_scr, nck, lambda kk, c, r: kk >= trial)
        return jnp.where(cnt >= nf, trial, thr)

    thr = lax.fori_loop(0, 31, bit_body, thr)
    cnt_ge = _count(key_scr, nck, lambda kk, c, r: kk >= thr)
    need = nf - _count(key_scr, nck, lambda kk, c, r: kk > thr)

    cut_scr[...] = jnp.full(cut_scr.shape, INT_MAX, I32)

    @pl.when(jnp.max(cnt_ge) > nf)
    def _():
        def tie_body(i, rr):
            t = rr | jnp.left_shift(jnp.int32(1), idx_bits - 1 - i)
            below = _count(key_scr, nck,
                           lambda kk, c, r: jnp.where(kk == thr, c * cw + r + sidx, INT_MAX) < t)
            return jnp.where(below <= need - 1.0, t, rr)

        rr = lax.fori_loop(0, idx_bits, tie_body, jnp.zeros((1, nq), I32))
        cut_scr[...] = jnp.broadcast_to(jnp.where(cnt_ge > nf, rr, INT_MAX), cut_scr.shape)

    cut = cut_scr[0:1, :]

    def sel_body(c, carry):
        kk = key_scr[c]
        tie = jnp.where(c * cw + kidx <= cut, 1, 0)
        key_scr[c] = emit(c, jnp.where(kk > thr, 1, jnp.where(kk == thr, tie, 0)) > 0)
        return carry

    lax.fori_loop(0, nck, sel_body, 0)


def _dsa_prompt_kernel(x_ref, qt_ref, qit_ref, wit_ref, ki_ref, k_ref, vt_ref, tile_ref, far_ref,
                       wout_ref, g_ref, b_ref, o_ref, key_scr, cut_scr, sa, sb, pb_scr, m_scr, l_scr, acc_scr,
                       *, alpha, n_sel, idx_bits):
    tq = DSA_TQ
    ck = DSA_CK
    sub = DSA_SUB
    qb = pl.program_id(1)
    q0 = qb * tq
    nck = (q0 + tq + ck - 1) // ck
    kloc = lax.broadcasted_iota(I32, (ck, tq), 0)
    qpos = q0 + lax.broadcasted_iota(I32, (ck, tq), 1)

    wit = wit_ref[0, 0]

    def score_body(c, carry):
        kic = ki_ref[0, pl.ds(pl.multiple_of(c * ck, ck), ck), :]
        s = None
        for h in range(IDX_HEADS):
            dots = _bdot(kic, qit_ref[0, 0, h])
            t = wit[h:h + 1, :] * jnp.maximum(dots, 0.0)
            s = t if s is None else s + t
        s = jnp.where(c * ck + kloc <= qpos, s, NEG_BIG)
        key_scr[c] = _order_key(s)
        return carry

    lax.fori_loop(0, nck, score_body, 0)

    def mask_bits(c, selected):
        keep = jnp.where(c * ck + kloc <= qpos, 0.0, NEG_BIG)
        return pltpu.bitcast(jnp.where(selected, keep, NEG_BIG), I32)

    _topk_select(key_scr, cut_scr, nck, n_sel, idx_bits, mask_bits)

    nmax = key_scr.shape[0] - 1
    key_scr[nmax] = pltpu.bitcast(jnp.full((ck, tq), NEG_BIG, F32), I32)
    m_scr[...] = jnp.full(m_scr.shape, NEG_BIG, F32)
    l_scr[...] = jnp.zeros(l_scr.shape, F32)
    acc_scr[...] = jnp.zeros(acc_scr.shape, F32)
    hsl = [slice(h * AT_DH, (h + 1) * AT_DH) for h in range(AT_HEADS)]
    ones = jnp.ones((SUBLANES, ck), BF16)

    def masked_logits(kv, mk, h, dst):
        k0 = pl.multiple_of(kv * ck, ck)
        dst[h] = _bdot(k_ref[0, pl.ds(k0, ck), hsl[h]], qt_ref[0, 0, h]) + pltpu.bitcast(key_scr[mk], F32)

    def softmax_pv(kv, h, src, fb):
        m = m_scr[h][0:1, :]
        cm = None
        for j in range(ck // sub):
            t = jnp.max(src[h, j * sub:(j + 1) * sub, :], axis=0, keepdims=True)
            cm = t if cm is None else jnp.maximum(cm, t)
        mn = jnp.maximum(m, cm + fb)
        shift = mn - fb
        for j in range(ck // sub):
            pb_scr[h, j * sub:(j + 1) * sub, :] = jnp.exp2(src[h, j * sub:(j + 1) * sub, :] - shift).astype(BF16)
        a = jnp.exp2(m - mn)
        pb = pb_scr[h]
        l_scr[h] = a * l_scr[h] + _bdot(ones, pb)
        acc_scr[h] = a * acc_scr[h] + _bdot(vt_ref[0, h, kv], pb)
        m_scr[h] = jnp.broadcast_to(mn, (SUBLANES, tq))

    def near_body(c, carry):
        k0 = c * ck
        for h in range(AT_HEADS):
            masked_logits(c, c, h, sa)
            rows = []
            for j in range(ck // sub):
                cols = []
                for qh in range(tq // sub):
                    delta = (q0 + qh * sub) - (k0 + j * sub)
                    cols.append(jnp.where(delta == 0, tile_ref[h, sub:2 * sub, :],
                                          jnp.where(delta == sub, tile_ref[h, 0:sub, :],
                                                    jnp.broadcast_to(far_ref[h], (sub, sub)))))
                rows.append(jnp.concatenate(cols, axis=1))
            sa[h] = sa[h] + jnp.concatenate(rows, axis=0)
            softmax_pv(c, h, sa, 0.0)
        return carry

    nfar = jnp.maximum(nck - 2, 0)
    lax.fori_loop(nfar, nck, near_body, 0)

    def ids(c):
        ok = c < nfar
        return jnp.where(ok, c, 0), jnp.where(ok, c, nmax)

    @pl.when(nfar > 0)
    def _():
        for h in range(AT_HEADS):
            masked_logits(0, 0, h, sa)

    def pair_body(p, carry):
        kv0 = 2 * p
        kv1, mk1 = ids(2 * p + 1)
        kv2, mk2 = ids(2 * p + 2)
        for h in range(AT_HEADS):
            masked_logits(kv1, mk1, h, sb)
        for h in range(AT_HEADS):
            softmax_pv(kv0, h, sa, far_ref[h][:, 0:1])
            masked_logits(kv2, mk2, h, sa)
        for h in range(AT_HEADS):
            softmax_pv(kv1, h, sb, far_ref[h][:, 0:1])
        return carry

    lax.fori_loop(0, (nfar + 1) // 2, pair_body, 0)
    att = jnp.concatenate([(acc_scr[h] / l_scr[h][0:1, :]).T for h in range(AT_HEADS)], axis=1)
    y = _bdot(att.astype(BF16), wout_ref[...])
    o_ref[0] = _ln(alpha * x_ref[0] + y, g_ref[...], b_ref[...])


def dsa_prompt(x, qb, qi, wi, kibf, kbf, vbf, tile, far, w_out, g, b, alpha):
    n, l, d = x.shape
    n_sel = min(TOPK_MAX, l // 4)
    nck_max = (l + DSA_CK - 1) // DSA_CK
    idx_bits = max(1, int(nck_max * DSA_CK - 1).bit_length())
    nqb = l // DSA_TQ
    blocked_t = lambda a, w: a.reshape(n, nqb, DSA_TQ, -1, w).transpose(0, 1, 3, 4, 2)
    qt = blocked_t(qb, AT_DH)
    qit = blocked_t(qi, IDX_DIM)
    wit = wi[:, :, 0:IDX_HEADS].reshape(n, nqb, DSA_TQ, IDX_HEADS).transpose(0, 1, 3, 2)
    vt = vbf.reshape(n, nck_max, DSA_CK, AT_HEADS, AT_DH).transpose(0, 3, 1, 4, 2)
    tok = lambda w: pl.BlockSpec((1, DSA_TQ, w), lambda s, t: (s, t, 0))
    blk = lambda a: pl.BlockSpec((1, 1) + a.shape[2:], lambda s, t: (s, t) + (0,) * (a.ndim - 2))
    seq = lambda a: pl.BlockSpec((1,) + a.shape[1:], lambda s, t: (s,) + (0,) * (a.ndim - 1),
                                 pipeline_mode=pl.Buffered(1))
    return pl.pallas_call(
        functools.partial(_dsa_prompt_kernel, alpha=alpha, n_sel=n_sel, idx_bits=idx_bits),
        grid=(n, nqb),
        in_specs=[tok(d), blk(qt), blk(qit), blk(wit), seq(kibf), seq(kbf), seq(vt),
                  _const(tile.shape), _const(far.shape), _const(w_out.shape), _const((1, d)), _const((1, d))],
        out_specs=tok(d),
        out_shape=jax.ShapeDtypeStruct((n, l, d), F32),
        scratch_shapes=[pltpu.VMEM((nck_max + 1, DSA_CK, DSA_TQ), I32), pltpu.VMEM((SUBLANES, DSA_TQ), I32),
                        pltpu.VMEM((AT_HEADS, DSA_CK, DSA_TQ), F32), pltpu.VMEM((AT_HEADS, DSA_CK, DSA_TQ), F32),
                        pltpu.VMEM((AT_HEADS, DSA_CK, DSA_TQ), BF16),
                        pltpu.VMEM((AT_HEADS, SUBLANES, DSA_TQ), F32), pltpu.VMEM((AT_HEADS, SUBLANES, DSA_TQ), F32),
                        pltpu.VMEM((AT_HEADS, AT_DH, DSA_TQ), F32)],
        compiler_params=_params("parallel", "arbitrary"),
    )(x, qt, qit, wit, kibf, kbf, vt, tile, far, w_out, g, b)


def _dsa_sample_score_kernel(pt_ref, qi_ref, wi_ref, *refs):
    o_ref = refs[-1]
    for p, kidx_ref in enumerate(refs[:-1]):
        kp = kidx_ref[0, 0].astype(BF16)
        dots = lax.dot_general(qi_ref[0], kp, NT, preferred_element_type=F32)
        o_ref[0, :, p * PAGE:(p + 1) * PAGE] = jnp.sum(wi_ref[0] * jnp.maximum(dots, 0.0), axis=0, keepdims=True)


def _dsa_sample_select_kernel(sc_ref, qi_ref, ki_ref, wi_ref, sel_ref, key_scr, cut_scr, *, n_sel, idx_bits):
    nb = sc_ref.shape[1]
    pw = PAGE
    npg = sc_ref.shape[0] // pw
    for c in range(npg):
        key_scr[c] = _order_key(sc_ref[c * pw:(c + 1) * pw, :])
    qi = qi_ref[...].astype(F32)
    ki = ki_ref[...].astype(F32)
    wi = wi_ref[...]
    s = jnp.zeros((1, nb), F32)
    for h in range(IDX_HEADS):
        dot = jnp.sum(qi[h * IDX_DIM:(h + 1) * IDX_DIM, :] * ki, axis=0, keepdims=True)
        s = s + wi[h:h + 1, :] * jnp.maximum(dot, 0.0)
    row = lax.broadcasted_iota(I32, (pw, nb), 0)
    key_scr[npg] = _order_key(jnp.where(row == 0, s, NEG_BIG))
    _topk_select(key_scr, cut_scr, npg + 1, n_sel, idx_bits, lambda c, selected: jnp.where(selected, 1, 0))
    for c in range(npg + 1):
        sel_ref[c * pw:(c + 1) * pw, :] = key_scr[c]


def _dsa_sample_attn_kernel(pt_ref, q_ref, sel_ref, b8_ref, sb_ref, kn_ref, vn_ref, *refs):
    npg = (len(refs) - 1) // 2
    k_refs, v_refs, o_ref = refs[:npg], refs[npg:2 * npg], refs[-1]
    rows = PAGE * AT_HEADS
    past = npg * PAGE
    q = q_ref[0]
    s = jnp.concatenate([lax.dot_general(q, kr[0, 0].astype(BF16), NT, preferred_element_type=F32)
                         for kr in k_refs], axis=1)
    ok = jnp.broadcast_to(sel_ref[0, :, 0:npg * rows], s.shape) > 0
    s = jnp.where(ok, s + b8_ref[:, 0:npg * rows], NEG_BIG)
    sn = jnp.sum(q.astype(F32) * kn_ref[0].astype(F32), axis=1, keepdims=True) + sb_ref[:, past:past + 1]
    okn = jnp.broadcast_to(sel_ref[0, :, npg * rows:npg * rows + 1], sn.shape) > 0
    sn = jnp.where(okn, sn, NEG_BIG)
    m = jnp.maximum(jnp.max(s, axis=1, keepdims=True), sn)
    p = jnp.exp2(s - m)
    pn = jnp.exp2(sn - m)
    l = jnp.sum(p, axis=1, keepdims=True) + pn
    acc = pn.astype(BF16).astype(F32) * vn_ref[0].astype(F32)
    for i, vr in enumerate(v_refs):
        acc = acc + _bdot(p[:, i * rows:(i + 1) * rows].astype(BF16), vr[0, 0].astype(BF16))
    o_ref[0] = acc / l


def dsa_sample(x, j, qb, qib, kiln, wi, kbf, vbf, cache_k, cache_v, cache_kidx, page_table, sbias,
               w_out, g, b, alpha):
    n, d = x.shape
    npg = page_table.shape[1]
    past = npg * PAGE
    n_sel = min(TOPK_MAX, (past + 1) // 4)
    idx_bits = int((npg + 1) * PAGE - 1).bit_length()
    pt = page_table.reshape(-1)
    npool = cache_k.shape[1]

    def paged(shape):
        return [pl.BlockSpec((1, 1) + shape, lambda i, t, p=p: (j, t[i * npg + p], 0, 0)) for p in range(npg)]

    scores = pl.pallas_call(
        _dsa_sample_score_kernel,
        grid_spec=pltpu.PrefetchScalarGridSpec(
            num_scalar_prefetch=1, grid=(n,),
            in_specs=[pl.BlockSpec((1, IDX_HEADS, IDX_DIM), lambda i, t: (i, 0, 0)),
                      pl.BlockSpec((1, IDX_HEADS, 1), lambda i, t: (i, 0, 0))] + paged((PAGE, IDX_DIM)),
            out_specs=pl.BlockSpec((1, 1, past), lambda i, t: (i, 0, 0))),
        out_shape=jax.ShapeDtypeStruct((n, 1, past), F32),
        compiler_params=_params("parallel"),
    )(pt, qib.reshape(n, IDX_HEADS, IDX_DIM), wi[:, 0:IDX_HEADS].reshape(n, IDX_HEADS, 1),
      *([cache_kidx] * npg))

    sel = pl.pallas_call(
        functools.partial(_dsa_sample_select_kernel, n_sel=n_sel, idx_bits=idx_bits),
        out_shape=jax.ShapeDtypeStruct((past + PAGE, n), I32),
        scratch_shapes=[pltpu.VMEM((npg + 1, PAGE, n), I32), pltpu.VMEM((SUBLANES, n), I32)],
        compiler_params=pltpu.CompilerParams(vmem_limit_bytes=VMEM_LIMIT),
    )(scores.reshape(n, past).T, qib.T, kiln.astype(BF16).T, wi[:, 0:IDX_HEADS].T).T

    rows = PAGE * AT_HEADS
    sel8 = jnp.repeat(sel, AT_HEADS, axis=1).reshape(n, 1, (npg + 1) * rows)
    rep = jnp.repeat(sbias, AT_HEADS, axis=1)
    own_head = (jnp.arange(rep.shape[1]) % AT_HEADS)[None, :] == jnp.arange(AT_HEADS)[:, None]
    b8 = jnp.where(own_head, rep, NEG_BIG)
    ck4 = cache_k.reshape(cache_k.shape[0], npool, rows, AT_DH)
    cv4 = cache_v.reshape(cache_v.shape[0], npool, rows, AT_DH)
    heads = pl.BlockSpec((1, AT_HEADS, AT_DH), lambda i, t: (i, 0, 0))
    o = pl.pallas_call(
        _dsa_sample_attn_kernel,
        grid_spec=pltpu.PrefetchScalarGridSpec(
            num_scalar_prefetch=1, grid=(n,),
            in_specs=[heads, pl.BlockSpec((1, 1, sel8.shape[2]), lambda i, t: (i, 0, 0)),
                      pl.BlockSpec(b8.shape, lambda i, t: (0, 0), pipeline_mode=pl.Buffered(1)),
                      pl.BlockSpec(sbias.shape, lambda i, t: (0, 0), pipeline_mode=pl.Buffered(1)),
                      heads, heads] + paged((rows, AT_DH)) + paged((rows, AT_DH)),
            out_specs=heads),
        out_shape=jax.ShapeDtypeStruct((n, AT_HEADS, AT_DH), F32),
        compiler_params=_params("parallel"),
    )(pt, qb.reshape(n, AT_HEADS, AT_DH), sel8, b8, sbias, kbf.reshape(n, AT_HEADS, AT_DH),
      vbf.reshape(n, AT_HEADS, AT_DH), *([ck4] * npg), *([cv4] * npg))
    return mm_res_ln(o.reshape(n, d), w_out, x, g, b, alpha, n)


MLP_TM = 512


def kernel(x_prompt, x_sample, state_hgrn, cache_k, cache_v, cache_kidx, page_table, state_s5_re, state_s5_im, hg_w_in, hg_lb, hg_gnorm, hg_w_out, dsa_w_in, dsa_kln_g, dsa_kln_b, dsa_w_out, rel_bias, s5_a_re, s5_a_im, s5_log_dt, s5_b_re, s5_b_im, s5_c_re, s5_c_im, s5_d, s5_w_glu, s5_b_glu, ln_mix_g, ln_mix_b, mlp_w1, mlp_w2, ln_ffn_g, ln_ffn_b):
    n, l, d = x_prompt.shape
    nb = x_sample.shape[0]
    depth = ln_mix_g.shape[0]
    alpha = (2 * depth) ** 0.25
    past = page_table.shape[1] * PAGE
    row = lambda a: a.reshape(1, -1)
    xp = x_prompt
    xs = x_sample.reshape(nb, d)
    hg_p, hg_s = [], []
    k_p, v_p, ki_p, k_s, v_s, ki_s = [], [], [], [], [], []
    s5r_p, s5i_p, s5r_s, s5i_s = [], [], [], []
    for i in range(depth):
        j = i // N_MIXERS
        kind = i % N_MIXERS
        g, b = row(ln_mix_g[i]), row(ln_mix_b[i])
        if kind == 0:
            w_in, w_out, gn = hg_w_in[j].astype(BF16), hg_w_out[j].astype(BF16), row(hg_gnorm[j])
            xp, st_p = hgrn_prompt(xp, w_in, hg_lb, gn, w_out, g, b, alpha, i)
            xs, st_s = hgrn_sample(xs, state_hgrn[j], w_in, hg_lb, gn, w_out, g, b, alpha, i)
            hg_p.append(st_p)
            hg_s.append(st_s)
        elif kind == 1:
            w_out = dsa_w_out[j].astype(BF16)
            tile, far, sbias = dsa_bias(rel_bias, past)
            qb, k, v, kbf, vbf, qib, kiln, kibf, wi = dsa_proj(xp.reshape(n * l, d), dsa_w_in[j],
                                                               dsa_kln_g[j], dsa_kln_b[j], MLP_TM)
            seq = lambda a: a.reshape(n, l, a.shape[1])
            xp = dsa_prompt(xp, seq(qb), seq(qib), seq(wi), seq(kibf), seq(kbf), seq(vbf), tile, far,
                            w_out, g, b, alpha)
            k_p.append(k.reshape(n, l, AT_HEADS, AT_DH))
            v_p.append(v.reshape(n, l, AT_HEADS, AT_DH))
            ki_p.append(kiln.reshape(n, l, IDX_DIM))
            qb, k, v, kbf, vbf, qib, kiln, kibf, wi = dsa_proj(xs, dsa_w_in[j], dsa_kln_g[j], dsa_kln_b[j], nb)
            xs = dsa_sample(xs, j, qb, qib, kiln, wi, kbf, vbf, cache_k, cache_v, cache_kidx, page_table,
                            sbias, w_out, g, b, alpha)
            k_s.append(k.reshape(nb, 1, AT_HEADS, AT_DH))
            v_s.append(v.reshape(nb, 1, AT_HEADS, AT_DH))
            ki_s.append(kiln.reshape(nb, 1, IDX_DIM))
        else:
            prm = s5_params(s5_a_re[j], s5_a_im[j], s5_log_dt[j], s5_b_re[j], s5_b_im[j], s5_c_re[j],
                            s5_c_im[j], s5_d[j], s5_w_glu[j], s5_b_glu[j])
            sg, sp = s5_a_re.shape[1], s5_a_re.shape[2]
            xp, hr, hi = s5_prompt(xp, prm, g, b, alpha)
            xs, hr2, hi2 = s5_sample(xs, state_s5_re[j].reshape(nb, sg * sp), state_s5_im[j].reshape(nb, sg * sp),
                                     prm, g, b, alpha)
            s5r_p.append(hr.reshape(n, sg, sp))
            s5i_p.append(hi.reshape(n, sg, sp))
            s5r_s.append(hr2.reshape(nb, sg, sp))
            s5i_s.append(hi2.reshape(nb, sg, sp))
        w1, w2 = mlp_w1[i].astype(BF16), mlp_w2[i].astype(BF16)
        fg, fb = row(ln_ffn_g[i]), row(ln_ffn_b[i])
        xp = mlp_res_ln(xp.reshape(n * l, d), w1, w2, fg, fb, alpha, MLP_TM).reshape(n, l, d)
        xs = mlp_res_ln(xs, w1, w2, fg, fb, alpha, nb)
    return (xp, xs.reshape(nb, 1, d), jnp.stack(hg_p), jnp.stack(hg_s), jnp.stack(k_p), jnp.stack(v_p),
            jnp.stack(ki_p), jnp.stack(k_s), jnp.stack(v_s), jnp.stack(ki_s), jnp.stack(s5r_p),
            jnp.stack(s5i_p), jnp.stack(s5r_s), jnp.stack(s5i_s))
```

```python
import functools
import math

import numpy as np
import jax
import jax.numpy as jnp
from jax import lax
from jax.experimental import pallas as pl
from jax.experimental.pallas import tpu as pltpu

F32 = jnp.float32
BF16 = jnp.bfloat16
I32 = jnp.int32

LN_EPS = 1e-5
RMS_EPS = 1e-6
NEG_BIG = -1e30
N_MIXERS = 3
HG_HEADS = 8
HG_DK = 128
AT_HEADS = 8
AT_DH = 128
IDX_HEADS = 8
IDX_DIM = 64
TOPK_MAX = 256
REL_BUCKETS = 32
REL_MAX_DIST = 128
S5_GROUP = 16
S5_STATE = 64
PAGE = 128

LANES = 128
SUBLANES = 8
VMEM_LIMIT = 56 * 1024 * 1024

NT = (((1,), (1,)), ((), ()))
TN = (((0,), (0,)), ((), ()))


def _params(*sem):
    return pltpu.CompilerParams(dimension_semantics=sem, vmem_limit_bytes=VMEM_LIMIT)


def _const(shape):
    nd = len(shape)
    return pl.BlockSpec(shape, lambda *_: (0,) * nd, pipeline_mode=pl.Buffered(1))


def _ln(z, g, b):
    mu = jnp.mean(z, axis=-1, keepdims=True)
    zc = z - mu
    var = jnp.mean(zc * zc, axis=-1, keepdims=True)
    return zc * lax.rsqrt(var + LN_EPS) * g + b


def _sigmoid(z):
    return 1.0 / (1.0 + jnp.exp(-z))


def _bdot(a, b):
    return jnp.dot(a, b, preferred_element_type=F32)


def _mlp_kernel(x_ref, w1_ref, w2_ref, g_ref, b_ref, o_ref, *, alpha, hc):
    x = x_ref[...]
    xb = x.astype(BF16)
    acc = jnp.zeros(x.shape, F32)
    for c in range(w1_ref.shape[2] // hc):
        h = _bdot(xb, w1_ref[0, :, c * hc:(c + 1) * hc])
        h = jnp.maximum(h, 0.0)
        acc = acc + _bdot((h * h).astype(BF16), w2_ref[0, c * hc:(c + 1) * hc, :])
    o_ref[...] = _ln(alpha * x + acc, g_ref[...], b_ref[...])


def mlp_res_ln(x, w1, w2, layer, g, b, alpha, tm):
    t, d = x.shape
    f = w1.shape[2]
    pick = lambda r, c: pl.BlockSpec((1, r, c), lambda i: (layer, 0, 0), pipeline_mode=pl.Buffered(1))
    return pl.pallas_call(
        functools.partial(_mlp_kernel, alpha=alpha, hc=1024),
        grid=(t // tm,),
        in_specs=[pl.BlockSpec((tm, d), lambda i: (i, 0)), pick(d, f), pick(f, d),
                  _const((1, d)), _const((1, d))],
        out_specs=pl.BlockSpec((tm, d), lambda i: (i, 0)),
        out_shape=jax.ShapeDtypeStruct((t, d), F32),
        compiler_params=_params("parallel"),
    )(x, w1, w2, g, b)


def _mm_kernel(x_ref, w_ref, o_ref):
    o_ref[...] = _bdot(x_ref[...].astype(BF16), w_ref[...])


def mm(x, w, tm):
    t, d = x.shape
    n = w.shape[1]
    return pl.pallas_call(
        _mm_kernel,
        grid=(t // tm,),
        in_specs=[pl.BlockSpec((tm, d), lambda i: (i, 0)), _const((d, n))],
        out_specs=pl.BlockSpec((tm, n), lambda i: (i, 0)),
        out_shape=jax.ShapeDtypeStruct((t, n), F32),
        compiler_params=_params("parallel"),
    )(x, w)


def _mm_res_ln_kernel(a_ref, w_ref, x_ref, g_ref, b_ref, o_ref, *, alpha):
    y = _bdot(a_ref[...].astype(BF16), w_ref[...])
    o_ref[...] = _ln(alpha * x_ref[...] + y, g_ref[...], b_ref[...])


def mm_res_ln(a, w, x, g, b, alpha, tm):
    t, k = a.shape
    d = w.shape[1]
    return pl.pallas_call(
        functools.partial(_mm_res_ln_kernel, alpha=alpha),
        grid=(t // tm,),
        in_specs=[pl.BlockSpec((tm, k), lambda i: (i, 0)), _const((k, d)),
                  pl.BlockSpec((tm, d), lambda i: (i, 0)), _const((1, d)), _const((1, d))],
        out_specs=pl.BlockSpec((tm, d), lambda i: (i, 0)),
        out_shape=jax.ShapeDtypeStruct((t, d), F32),
        compiler_params=_params("parallel"),
    )(a, w, x, g, b)


HG_C = 128
HG_LEVELS = 7


def _hgrn_constants():
    c = HG_C
    t = np.arange(c)[:, None]
    u = np.arange(c)[None, :]
    mats = [(u <= t), (u > t)]
    for l in range(HG_LEVELS):
        blk = 2 << l
        mid = (t // blk) * blk + blk // 2
        second = t >= mid
        mats.append(np.where(second, (u >= mid) & (u <= t), (u > t) & (u < mid)))
    m = np.concatenate(mats, axis=0).astype(np.float32)
    x = t ^ u
    lv = np.where(u < t, np.floor(np.log2(np.maximum(x, 1))).astype(np.int32),
                  np.where(u == t, HG_LEVELS, -1)).astype(np.int32)
    return np.concatenate([m, m, m], axis=1), lv


def _lower_bound(lb_ref, layer):
    rows = [lb_ref[i] for i in range(lb_ref.shape[0])]
    mx = rows[0]
    for r in rows[1:]:
        mx = jnp.maximum(mx, r)
    ex = [jnp.exp(r - mx) for r in rows]
    tot = ex[0]
    for e in ex[1:]:
        tot = tot + e
    num = jnp.zeros_like(tot)
    for i in range(1, layer + 1):
        num = num + ex[i]
    return num / tot


def _split3(x):
    hi = x.astype(BF16)
    r1 = x - hi.astype(F32)
    mid = r1.astype(BF16)
    lo = (r1 - mid.astype(F32)).astype(BF16)
    return hi, mid, lo


def _hgrn_prompt_kernel(x_ref, win_ref, lb_ref, gn_ref, wout_ref, g_ref, b_ref, mst_ref, lv_ref,
                        o_ref, st_ref, og_scr, *, alpha, layer):
    c = HG_C
    dk = HG_DK

    @pl.when(pl.program_id(1) == 0)
    def _():
        st_ref[...] = jnp.zeros(st_ref.shape, F32)

    x = x_ref[0]
    d = x.shape[1]
    proj = _bdot(x.astype(BF16), win_ref[...])
    qp = proj[:, 0:d]
    fz = proj[:, d:2 * d]
    v = proj[:, 2 * d:3 * d]
    gp = proj[:, 3 * d:4 * d]
    lb = _lower_bound(lb_ref, layer)
    q = qp * _sigmoid(qp)
    sg = _sigmoid(fz)
    f = lb + (1.0 - lb) * sg
    k = (1.0 - lb) * (1.0 - sg)
    logf = jnp.log(f)
    gate = gp * _sigmoid(gp)
    hi, mid, lo = _split3(logf)
    ex = _bdot(mst_ref[...], jnp.concatenate([hi, mid, lo], axis=0))

    lv = lv_ref[...]
    row = lax.broadcasted_iota(I32, (c, dk), 0)
    for h in range(HG_HEADS):
        hs = slice(h * dk, (h + 1) * dk)
        qh = q[:, hs]
        kh = k[:, hs]
        vh = v[:, hs].astype(BF16)
        st = st_ref[0, h]
        eb = jnp.exp(ex[0:c, hs])
        o = lax.dot_general((qh * eb).astype(BF16), st.astype(BF16), NT, preferred_element_type=F32)
        att = jnp.zeros((c, c), F32)
        for l in range(HG_LEVELS):
            e = jnp.exp(ex[(2 + l) * c:(3 + l) * c, hs])
            second = ((row >> l) & 1) == 1
            w = (jnp.where(second, qh, kh) * e).astype(BF16)
            a = lax.dot_general(w, w, NT, preferred_element_type=F32)
            att = jnp.where(lv == l, a, att)
        a0 = lax.dot_general(qh.astype(BF16), kh.astype(BF16), NT, preferred_element_type=F32)
        att = jnp.where(lv == HG_LEVELS, a0, att)
        o = o + _bdot(att.astype(BF16), vh)
        kd = (kh * jnp.exp(ex[c:2 * c, hs])).astype(BF16)
        st_ref[0, h] = eb[c - 1:c, :] * st + lax.dot_general(vh, kd, TN, preferred_element_type=F32)
        ms = jnp.mean(o * o, axis=-1, keepdims=True)
        og_scr[:, hs] = o * lax.rsqrt(ms + RMS_EPS) * gn_ref[:, hs] * gate[:, hs]
    y = _bdot(og_scr[...].astype(BF16), wout_ref[...])
    o_ref[0] = _ln(alpha * x + y, g_ref[...], b_ref[...])


def hgrn_prompt(x, w_in, hg_lb, gnorm, w_out, g, b, alpha, layer):
    n, l, d = x.shape
    mst, lv = _hgrn_constants()
    nl = hg_lb.shape[0]
    out, st = pl.pallas_call(
        functools.partial(_hgrn_prompt_kernel, alpha=alpha, layer=layer),
        grid=(n, l // HG_C),
        in_specs=[pl.BlockSpec((1, HG_C, d), lambda s, c: (s, c, 0)),
                  _const((d, 4 * d)), _const((nl, 1, d)), _const((1, d)), _const((d, d)),
                  _const((1, d)), _const((1, d)), _const(mst.shape), _const(lv.shape)],
        out_specs=[pl.BlockSpec((1, HG_C, d), lambda s, c: (s, c, 0)),
                   pl.BlockSpec((1, HG_HEADS, HG_DK, HG_DK), lambda s, c: (s, 0, 0, 0))],
        out_shape=[jax.ShapeDtypeStruct((n, l, d), F32),
                   jax.ShapeDtypeStruct((n, HG_HEADS, HG_DK, HG_DK), F32)],
        scratch_shapes=[pltpu.VMEM((HG_C, d), F32)],
        compiler_params=_params("parallel", "arbitrary"),
    )(x, w_in, hg_lb.reshape(nl, 1, d), gnorm, w_out, g, b,
      jnp.asarray(mst, BF16), jnp.asarray(lv))
    return out, jnp.swapaxes(st, 2, 3)


HG_SG = 4


def _hgrn_sample_kernel(qt_ref, fzt_ref, lbt_ref, v_ref, gp_ref, gn_ref, s0_ref, *rest, layer):
    o_ref, s1_ref = rest[-2], rest[-1]
    dk = HG_DK
    lbt = _lower_bound(lbt_ref, layer)
    for i in range(HG_SG):
        qt = qt_ref[i]
        q = qt * _sigmoid(qt)
        sg = _sigmoid(fzt_ref[i])
        f = lbt + (1.0 - lbt) * sg
        k = (1.0 - lbt) * (1.0 - sg)
        gp = gp_ref[i]
        gate = gp * _sigmoid(gp)
        for h in range(HG_HEADS):
            hs = slice(h * dk, (h + 1) * dk)
            s_new = f[:, h:h + 1] * s0_ref[0, i, h] + k[:, h:h + 1] * v_ref[i, :, hs]
            s1_ref[0, i, h] = s_new
            o = jnp.sum(s_new * q[:, h:h + 1], axis=0, keepdims=True)
            ms = jnp.mean(o * o, axis=-1, keepdims=True)
            o_ref[i, :, hs] = o * lax.rsqrt(ms + RMS_EPS) * gn_ref[:, hs] * gate[:, hs]


def hgrn_sample(x, states, j, new_states, w_in, hg_lb, gnorm, w_out, g, b, alpha, layer):
    n, d = x.shape
    nl = hg_lb.shape[0]
    proj = mm(x, w_in, n)
    to_cols = lambda a: a.reshape(n, HG_HEADS, HG_DK).transpose(0, 2, 1)
    lbt = hg_lb.reshape(nl, HG_HEADS, HG_DK).transpose(0, 2, 1)
    row3 = lambda a: a.reshape(n, 1, d)
    cols = pl.BlockSpec((HG_SG, HG_DK, HG_HEADS), lambda i: (i, 0, 0))
    rows = pl.BlockSpec((HG_SG, 1, d), lambda i: (i, 0, 0))
    st = pl.BlockSpec((1, HG_SG, HG_HEADS, HG_DK, HG_DK), lambda i: (j, i, 0, 0, 0))
    carried = [] if new_states is None else [new_states]
    o, s1 = pl.pallas_call(
        functools.partial(_hgrn_sample_kernel, layer=layer),
        grid=(n // HG_SG,),
        in_specs=[cols, cols, _const((nl, HG_DK, HG_HEADS)), rows, rows, _const((1, d)), st]
                 + [pl.BlockSpec(memory_space=pl.ANY)] * len(carried),
        out_specs=[rows, st],
        out_shape=[jax.ShapeDtypeStruct((n, 1, d), F32), jax.ShapeDtypeStruct(states.shape, F32)],
        input_output_aliases={7: 1} if carried else {},
        compiler_params=_params("parallel"),
    )(to_cols(proj[:, 0:d]), to_cols(proj[:, d:2 * d]), lbt, row3(proj[:, 2 * d:3 * d]),
      row3(proj[:, 3 * d:4 * d]), gnorm, states, *carried)
    return mm_res_ln(o.reshape(n, d), w_out, x, g, b, alpha, n), s1


S5_TC = 128
S5_SLABS = 8
S5_KS = 3


def _cmul(ar, ai, br, bi):
    return ar * br - ai * bi, ar * bi + ai * br


def _s5_prep_kernel(are_ref, aim_ref, ldt_ref, are2_ref, aim2_ref, ldt2_ref, bre_ref, bim_ref,
                    pwre_ref, pwim_ref, ckre_ref, ckim_ref, bbre_ref, bbim_ref):
    dt = jnp.exp(ldt_ref[...])
    mag = jnp.exp(dt * are_ref[...])
    a_r = mag * jnp.cos(dt * aim_ref[...])
    a_i = mag * jnp.sin(dt * aim_ref[...])
    pr, pi_ = [a_r], [a_i]
    for _ in range(SUBLANES - 1):
        nr, ni = _cmul(pr[-1], pi_[-1], a_r, a_i)
        pr.append(nr)
        pi_.append(ni)
    pwre_ref[...] = jnp.concatenate(pr, axis=0)
    pwim_ref[...] = jnp.concatenate(pi_, axis=0)
    rows = lax.broadcasted_iota(I32, pwre_ref.shape, 0)
    for k in range(S5_KS):
        s = 1 << k
        ckre_ref[k] = jnp.where(rows >= s, pr[s - 1], 0.0)
        ckim_ref[k] = jnp.where(rows >= s, pi_[s - 1], 0.0)
    ar, ai = are2_ref[...], aim2_ref[...]
    dt2 = jnp.exp(ldt2_ref[...])
    mag2 = jnp.exp(dt2 * ar)
    nr = mag2 * jnp.cos(dt2 * ai) - 1.0
    ni = mag2 * jnp.sin(dt2 * ai)
    den = ar * ar + ai * ai
    z_re = ((nr * ar + ni * ai) / den)[:, None, :]
    z_im = ((ni * ar - nr * ai) / den)[:, None, :]
    br, bi = bre_ref[...], bim_ref[...]
    bbre_ref[...] = z_re * br - z_im * bi
    bbim_ref[...] = z_re * bi + z_im * br


def s5_prep(a_re, a_im, log_dt, b_re, b_im):
    g, p = a_re.shape
    k = b_re.shape[2]
    n = g * p
    ldt2 = jnp.broadcast_to(log_dt[:, None], (g, p))
    flat = lambda a: a.reshape(1, n)
    return pl.pallas_call(
        _s5_prep_kernel,
        out_shape=[jax.ShapeDtypeStruct((SUBLANES, n), F32), jax.ShapeDtypeStruct((SUBLANES, n), F32),
                   jax.ShapeDtypeStruct((S5_KS, SUBLANES, n), F32), jax.ShapeDtypeStruct((S5_KS, SUBLANES, n), F32),
                   jax.ShapeDtypeStruct((g, k, p), F32), jax.ShapeDtypeStruct((g, k, p), F32)],
    )(flat(a_re), flat(a_im), flat(ldt2), a_re, a_im, ldt2,
      b_re.transpose(0, 2, 1), b_im.transpose(0, 2, 1))


def _block_diag_slabs(w_gkp, rows_are_inputs):
    g, k, p = w_gkp.shape
    per = g // S5_SLABS
    w4 = w_gkp.reshape(S5_SLABS, per, k, p)
    eye = jnp.eye(per, dtype=w_gkp.dtype)
    if rows_are_inputs:
        return jnp.einsum('iakp,ab->iakbp', w4, eye).reshape(S5_SLABS, per * k, per * p)
    return jnp.einsum('iakp,ab->ibpak', w4, eye).reshape(S5_SLABS, per * p, per * k)


def _s5_tail(x, y, d_ref, wglu_ref, bglu_ref, g_ref, b_ref, alpha):
    d = x.shape[1]
    gl = jax.nn.gelu(y + d_ref[...] * x)
    u = _bdot(gl.astype(BF16), wglu_ref[...]) + bglu_ref[...]
    out = u[:, 0:d] * _sigmoid(u[:, d:2 * d])
    return _ln(alpha * x + out, g_ref[...], b_ref[...])


def _s5_prompt_kernel(x_ref, wbre_ref, wbim_ref, wcre_ref, wcim_ref, pwre_ref, pwim_ref, ckre_ref, ckim_ref,
                      d_ref, wglu_ref, bglu_ref, g_ref, b_ref,
                      o_ref, hre_ref, him_ref, cre_scr, cim_scr, y_scr, *, alpha):
    tc = x_ref.shape[1]
    ng = tc // SUBLANES
    ci = pl.program_id(1)

    @pl.when(ci == 0)
    def _():
        cre_scr[...] = jnp.zeros(cre_scr.shape, F32)
        cim_scr[...] = jnp.zeros(cim_scr.shape, F32)

    x = x_ref[0]
    xb = x.astype(BF16)
    sw = wbre_ref.shape[2]
    for i in range(S5_SLABS):
        cs = slice(i * sw, (i + 1) * sw)
        xi = xb[:, i * LANES:(i + 1) * LANES]
        br = _bdot(xi, wbre_ref[i]).reshape(ng, SUBLANES, sw)
        bi = _bdot(xi, wbim_ref[i]).reshape(ng, SUBLANES, sw)
        for k in range(S5_KS):
            sr = pltpu.roll(br, 1 << k, axis=1)
            si = pltpu.roll(bi, 1 << k, axis=1)
            tr, ti = _cmul(ckre_ref[k, :, cs], ckim_ref[k, :, cs], sr, si)
            br, bi = br + tr, bi + ti
        pr, pi_ = pwre_ref[:, cs], pwim_ref[:, cs]
        hpr, hpi = cre_scr[:, cs], cim_scr[:, cs]
        hr_all, hi_all = [], []
        for m in range(ng):
            tr, ti = _cmul(pr, pi_, hpr, hpi)
            hr, hi = br[m] + tr, bi[m] + ti
            hr_all.append(hr)
            hi_all.append(hi)
            hpr = jnp.broadcast_to(hr[SUBLANES - 1:SUBLANES, :], hr.shape)
            hpi = jnp.broadcast_to(hi[SUBLANES - 1:SUBLANES, :], hi.shape)
        cre_scr[:, cs] = hpr
        cim_scr[:, cs] = hpi
        hr = jnp.concatenate(hr_all, axis=0).astype(BF16)
        hi = jnp.concatenate(hi_all, axis=0).astype(BF16)
        y_scr[:, i * LANES:(i + 1) * LANES] = _bdot(hr, wcre_ref[i]) - _bdot(hi, wcim_ref[i])
    o_ref[0] = _s5_tail(x, y_scr[...], d_ref, wglu_ref, bglu_ref, g_ref, b_ref, alpha)

    @pl.when(ci == pl.num_programs(1) - 1)
    def _():
        hre_ref[0] = cre_scr[0:1, :]
        him_ref[0] = cim_scr[0:1, :]


def _s5_weight_specs(wb, wc, pw, ck, d, wglu):
    return [_const(wb.shape), _const(wb.shape), _const(wc.shape), _const(wc.shape),
            _const(pw.shape), _const(pw.shape), _const(ck.shape), _const(ck.shape),
            _const((1, d)), _const(wglu.shape), _const((1, wglu.shape[1])), _const((1, d)), _const((1, d))]


def s5_prompt(x, prm, g, b, alpha):
    n, l, d = x.shape
    wb, wc, pw, ck, wglu = prm[0], prm[2], prm[4], prm[6], prm[9]
    ns = pw.shape[1]
    out, hre, him = pl.pallas_call(
        functools.partial(_s5_prompt_kernel, alpha=alpha),
        grid=(n, l // S5_TC),
        in_specs=[pl.BlockSpec((1, S5_TC, d), lambda s, c: (s, c, 0))] + _s5_weight_specs(wb, wc, pw, ck, d, wglu),
        out_specs=[pl.BlockSpec((1, S5_TC, d), lambda s, c: (s, c, 0)),
                   pl.BlockSpec((1, 1, ns), lambda s, c: (s, 0, 0)),
                   pl.BlockSpec((1, 1, ns), lambda s, c: (s, 0, 0))],
        out_shape=[jax.ShapeDtypeStruct((n, l, d), F32),
                   jax.ShapeDtypeStruct((n, 1, ns), F32), jax.ShapeDtypeStruct((n, 1, ns), F32)],
        scratch_shapes=[pltpu.VMEM((SUBLANES, ns), F32), pltpu.VMEM((SUBLANES, ns), F32),
                        pltpu.VMEM((S5_TC, d), F32)],
        compiler_params=_params("parallel", "arbitrary"),
    )(x, *prm, g, b)
    return out, hre, him


def _s5_sample_kernel(x_ref, h0re_ref, h0im_ref, wbre_ref, wbim_ref, wcre_ref, wcim_ref, pwre_ref, pwim_ref,
                      ckre_ref, ckim_ref, d_ref, wglu_ref, bglu_ref, g_ref, b_ref,
                      o_ref, hre_ref, him_ref, y_scr, *, alpha):
    x = x_ref[...]
    xb = x.astype(BF16)
    sw = wbre_ref.shape[2]
    for i in range(S5_SLABS):
        cs = slice(i * sw, (i + 1) * sw)
        xi = xb[:, i * LANES:(i + 1) * LANES]
        tr, ti = _cmul(pwre_ref[0:1, cs], pwim_ref[0:1, cs], h0re_ref[:, cs], h0im_ref[:, cs])
        hr = _bdot(xi, wbre_ref[i]) + tr
        hi = _bdot(xi, wbim_ref[i]) + ti
        hre_ref[:, cs] = hr
        him_ref[:, cs] = hi
        y_scr[:, i * LANES:(i + 1) * LANES] = (_bdot(hr.astype(BF16), wcre_ref[i])
                                               - _bdot(hi.astype(BF16), wcim_ref[i]))
    o_ref[...] = _s5_tail(x, y_scr[...], d_ref, wglu_ref, bglu_ref, g_ref, b_ref, alpha)


def s5_sample(x, h0re, h0im, prm, g, b, alpha):
    n, d = x.shape
    ns = h0re.shape[1]
    return pl.pallas_call(
        functools.partial(_s5_sample_kernel, alpha=alpha),
        out_shape=[jax.ShapeDtypeStruct((n, d), F32),
                   jax.ShapeDtypeStruct((n, ns), F32), jax.ShapeDtypeStruct((n, ns), F32)],
        scratch_shapes=[pltpu.VMEM((n, d), F32)],
        compiler_params=pltpu.CompilerParams(vmem_limit_bytes=VMEM_LIMIT),
    )(x, h0re, h0im, *prm, g, b)


def s5_params(a_re, a_im, log_dt, b_re, b_im, c_re, c_im, d_skip, w_glu, b_glu):
    pwre, pwim, ckre, ckim, bbre, bbim = s5_prep(a_re, a_im, log_dt, b_re, b_im)
    return (_block_diag_slabs(bbre, True).astype(BF16), _block_diag_slabs(bbim, True).astype(BF16),
            _block_diag_slabs(c_re, False).astype(BF16), _block_diag_slabs(c_im, False).astype(BF16),
            pwre, pwim, ckre, ckim, d_skip.reshape(1, -1), w_glu.astype(BF16), b_glu.reshape(1, -1))


DSA_TQ = 256
DSA_SUB = 128
DSA_CK = 512
INT_MIN = -2 ** 31
INT_MAX = 2 ** 31 - 1
LOG2E = math.log2(math.e)


def _dsa_proj_kernel(x_ref, wq_ref, wk_ref, wv_ref, wqi_ref, wki_ref, wwi_ref, kg_ref, kb_ref,
                     qb_ref, k_ref, v_ref, kbf_ref, vbf_ref, qi_ref, kiln_ref, kibf_ref, wi_ref):
    xb = x_ref[...].astype(BF16)
    qb_ref[...] = (_bdot(xb, wq_ref[...]) * (AT_DH ** -0.5 * LOG2E)).astype(BF16)
    k = _bdot(xb, wk_ref[...])
    v = _bdot(xb, wv_ref[...])
    k_ref[...] = k
    v_ref[...] = v
    kbf_ref[...] = k.astype(BF16)
    vbf_ref[...] = v.astype(BF16)
    qi_ref[...] = (_bdot(xb, wqi_ref[...]) * (IDX_DIM ** -0.5)).astype(BF16)
    ki = _ln(_bdot(xb, wki_ref[...])[:, 0:IDX_DIM], kg_ref[...], kb_ref[...])
    kiln_ref[...] = ki
    kibf_ref[...] = ki.astype(BF16)
    wi_ref[...] = _bdot(xb, wwi_ref[...]) * (IDX_HEADS ** -0.5)


def dsa_proj(x, w_in, kln_g, kln_b, tm):
    t, d = x.shape
    hd = AT_HEADS * AT_DH
    qd = IDX_HEADS * IDX_DIM
    wb = w_in.astype(BF16)
    pad = lambda w: jnp.pad(w, ((0, 0), (0, LANES - w.shape[1])))
    ws = [wb[:, 0:hd], wb[:, hd:2 * hd], wb[:, 2 * hd:3 * hd], wb[:, 3 * hd:3 * hd + qd],
          pad(wb[:, 3 * hd + qd:3 * hd + qd + IDX_DIM]), pad(wb[:, 3 * hd + qd + IDX_DIM:])]
    row = lambda n, dt: (pl.BlockSpec((tm, n), lambda i: (i, 0)), jax.ShapeDtypeStruct((t, n), dt))
    outs = [row(hd, BF16), row(hd, F32), row(hd, F32), row(hd, BF16), row(hd, BF16), row(qd, BF16),
            row(IDX_DIM, F32), row(IDX_DIM, BF16), row(LANES, F32)]
    return pl.pallas_call(
        _dsa_proj_kernel,
        grid=(t // tm,),
        in_specs=[pl.BlockSpec((tm, d), lambda i: (i, 0))] + [_const(w.shape) for w in ws]
                 + [_const((1, IDX_DIM)), _const((1, IDX_DIM))],
        out_specs=[o[0] for o in outs],
        out_shape=[o[1] for o in outs],
        compiler_params=_params("parallel"),
    )(x, *ws, kln_g.reshape(1, IDX_DIM), kln_b.reshape(1, IDX_DIM))


def _t5_bucket(rel):
    n = jnp.maximum(rel, 0)
    exact = REL_BUCKETS // 2
    nf = jnp.maximum(n, exact).astype(F32)
    large = exact + (jnp.log(nf / exact) / math.log(REL_MAX_DIST / exact) * (REL_BUCKETS - exact)).astype(I32)
    large = jnp.minimum(large, REL_BUCKETS - 1)
    return jnp.where(n < exact, n, large)


def _bias_lookup(bucket, rb_ref, h):
    out = jnp.zeros(bucket.shape, F32)
    for bk in range(REL_BUCKETS):
        out = jnp.where(bucket == bk, rb_ref[bk, h], out)
    return out


def _dsa_bias_kernel(rb_ref, tile_ref, far_ref, samp_ref, *, past):
    c = lax.broadcasted_iota(I32, tile_ref.shape[1:], 0)
    i = lax.broadcasted_iota(I32, tile_ref.shape[1:], 1)
    bt = _t5_bucket(i - c + DSA_SUB)
    bs = _t5_bucket(past - lax.broadcasted_iota(I32, (1, samp_ref.shape[1]), 1))
    for h in range(AT_HEADS):
        tile_ref[h] = _bias_lookup(bt, rb_ref, h) * LOG2E
        far_ref[h] = jnp.full(far_ref.shape[1:], rb_ref[REL_BUCKETS - 1, h] * LOG2E, F32)
        samp_ref[h:h + 1, :] = _bias_lookup(bs, rb_ref, h) * LOG2E


def dsa_bias(rel_bias, past):
    return pl.pallas_call(
        functools.partial(_dsa_bias_kernel, past=past),
        in_specs=[pl.BlockSpec(memory_space=pltpu.SMEM)],
        out_shape=[jax.ShapeDtypeStruct((AT_HEADS, 2 * DSA_SUB, DSA_SUB), F32),
                   jax.ShapeDtypeStruct((AT_HEADS, 1, LANES), F32),
                   jax.ShapeDtypeStruct((AT_HEADS, past + PAGE), F32)],
    )(rel_bias)


def _order_key(s):
    bits = pltpu.bitcast(s, I32)
    return bits ^ ((bits >> 31) & INT_MAX)


def _count(key_scr, nck, pred):
    cw, nq = key_scr.shape[1], key_scr.shape[2]

    def body(c, acc):
        kk = key_scr[c]
        parts = [jnp.where(pred(kk[g * SUBLANES:(g + 1) * SUBLANES], c, g * SUBLANES), 1.0, 0.0)
                 for g in range(cw // SUBLANES)]
        while len(parts) > 1:
            parts = [parts[i] + parts[i + 1] for i in range(0, len(parts), 2)]
        return acc + parts[0]

    acc = lax.fori_loop(0, nck, body, jnp.zeros((SUBLANES, nq), F32))
    return jnp.sum(acc, axis=0, keepdims=True)


def _topk_select(key_scr, cut_scr, nck, n_sel, idx_bits, emit):
    cw, nq = key_scr.shape[1], key_scr.shape[2]
    nf = float(n_sel)
    kidx = lax.broadcasted_iota(I32, (cw, nq), 0)
    sidx = lax.broadcasted_iota(I32, (SUBLANES, nq), 0)

    cnt = _count(key_scr, nck, lambda kk, c, r: kk >= 0)
    thr = jnp.where(cnt >= nf, 0, INT_MIN).astype(I32)

    def bit_body(i, thr):
        trial = thr | jnp.left_shift(jnp.int32(1), 30 - i)
        cnt = _count(key_scr, nck, lambda kk, c, r: kk >= trial)
        return jnp.where(cnt >= nf, trial, thr)

    thr = lax.fori_loop(0, 31, bit_body, thr)
    cnt_ge = _count(key_scr, nck, lambda kk, c, r: kk >= thr)
    need = nf - _count(key_scr, nck, lambda kk, c, r: kk > thr)

    cut_scr[...] = jnp.full(cut_scr.shape, INT_MAX, I32)

    @pl.when(jnp.max(cnt_ge) > nf)
    def _():
        def tie_body(i, rr):
            t = rr | jnp.left_shift(jnp.int32(1), idx_bits - 1 - i)
            below = _count(key_scr, nck,
                           lambda kk, c, r: jnp.where(kk == thr, c * cw + r + sidx, INT_MAX) < t)
            return jnp.where(below <= need - 1.0, t, rr)

        rr = lax.fori_loop(0, idx_bits, tie_body, jnp.zeros((1, nq), I32))
        cut_scr[...] = jnp.broadcast_to(jnp.where(cnt_ge > nf, rr, INT_MAX), cut_scr.shape)

    cut = cut_scr[0:1, :]

    def sel_body(c, carry):
        kk = key_scr[c]
        tie = jnp.where(c * cw + kidx <= cut, 1, 0)
        key_scr[c] = emit(c, jnp.where(kk > thr, 1, jnp.where(kk == thr, tie, 0)) > 0)
        return carry

    lax.fori_loop(0, nck, sel_body, 0)


def _dsa_prompt_kernel(x_ref, q_ref, qi_ref, wi_ref, ki_ref, k_ref, vt_ref, tile_ref, far_ref,
                       wout_ref, g_ref, b_ref, o_ref, key_scr, cut_scr, sa, sb, pb_scr, cm_scr, m_scr, l_scr, acc_scr,
                       *, alpha, n_sel, idx_bits):
    tq = DSA_TQ
    ck = DSA_CK
    sub = DSA_SUB
    qb = pl.program_id(1)
    q0 = qb * tq
    nck = (q0 + tq + ck - 1) // ck
    kloc = lax.broadcasted_iota(I32, (ck, tq), 0)
    qpos = q0 + lax.broadcasted_iota(I32, (ck, tq), 1)

    wit = wi_ref[0].T
    qis = [qi_ref[0, :, h * IDX_DIM:(h + 1) * IDX_DIM] for h in range(IDX_HEADS)]

    def score_body(c, carry):
        kic = ki_ref[0, pl.ds(pl.multiple_of(c * ck, ck), ck), :]
        s = None
        for h in range(IDX_HEADS):
            dots = lax.dot_general(kic, qis[h], NT, preferred_element_type=F32)
            t = wit[h:h + 1, :] * jnp.maximum(dots, 0.0)
            s = t if s is None else s + t
        s = jnp.where(c * ck + kloc <= qpos, s, NEG_BIG)
        key_scr[c] = _order_key(s)
        return carry

    lax.fori_loop(0, nck, score_body, 0)

    def mask_bits(c, selected):
        keep = jnp.where(c * ck + kloc <= qpos, 0.0, NEG_BIG)
        return pltpu.bitcast(jnp.where(selected, keep, NEG_BIG), I32)

    _topk_select(key_scr, cut_scr, nck, n_sel, idx_bits, mask_bits)

    nmax = key_scr.shape[0] - 1
    key_scr[nmax] = pltpu.bitcast(jnp.full((ck, tq), NEG_BIG, F32), I32)
    m_scr[...] = jnp.full(m_scr.shape, NEG_BIG, F32)
    l_scr[...] = jnp.zeros(l_scr.shape, F32)
    acc_scr[...] = jnp.zeros(acc_scr.shape, F32)
    hsl = [slice(h * AT_DH, (h + 1) * AT_DH) for h in range(AT_HEADS)]
    ones = jnp.ones((SUBLANES, ck), BF16)

    def logits(kv, mk, h):
        k0 = pl.multiple_of(kv * ck, ck)
        return (lax.dot_general(k_ref[0, pl.ds(k0, ck), hsl[h]], q_ref[0, :, hsl[h]], NT,
                                preferred_element_type=F32) + pltpu.bitcast(key_scr[mk], F32))

    def put(buf, h, val):
        ref, slot = buf
        ref[h] = val
        cm_scr[slot, h] = jnp.broadcast_to(jnp.max(val, axis=0, keepdims=True), (SUBLANES, tq))

    buf_a, buf_b = (sa, 0), (sb, 1)

    def softmax_pv(kv, h, buf, fb):
        src, slot = buf
        m = m_scr[h][0:1, :]
        mn = jnp.maximum(m, cm_scr[slot, h][0:1, :] + fb)
        shift = mn - fb
        for j in range(ck // sub):
            pb_scr[h, j * sub:(j + 1) * sub, :] = jnp.exp2(src[h, j * sub:(j + 1) * sub, :] - shift).astype(BF16)
        a = jnp.exp2(m - mn)
        pb = pb_scr[h]
        l_scr[h] = a * l_scr[h] + _bdot(ones, pb)
        acc_scr[h] = a * acc_scr[h] + _bdot(vt_ref[0, h, kv], pb)
        m_scr[h] = jnp.broadcast_to(mn, (SUBLANES, tq))

    def near_body(c, carry):
        k0 = c * ck
        for h in range(AT_HEADS):
            rows = []
            for j in range(ck // sub):
                cols = []
                for qh in range(tq // sub):
                    delta = (q0 + qh * sub) - (k0 + j * sub)
                    cols.append(jnp.where(delta == 0, tile_ref[h, sub:2 * sub, :],
                                          jnp.where(delta == sub, tile_ref[h, 0:sub, :],
                                                    jnp.broadcast_to(far_ref[h], (sub, sub)))))
                rows.append(jnp.concatenate(cols, axis=1))
            put(buf_a, h, logits(c, c, h) + jnp.concatenate(rows, axis=0))
            softmax_pv(c, h, buf_a, 0.0)
        return carry

    nfar = jnp.maximum(nck - 2, 0)
    lax.fori_loop(nfar, nck, near_body, 0)

    def ids(c):
        ok = c < nfar
        return jnp.where(ok, c, 0), jnp.where(ok, c, nmax)

    @pl.when(nfar > 0)
    def _():
        for h in range(AT_HEADS):
            put(buf_a, h, logits(0, 0, h))

    def pair_body(p, carry):
        kv0 = 2 * p
        kv1, mk1 = ids(2 * p + 1)
        kv2, mk2 = ids(2 * p + 2)
        for h in range(AT_HEADS):
            put(buf_b, h, logits(kv1, mk1, h))
        for h in range(AT_HEADS):
            softmax_pv(kv0, h, buf_a, far_ref[h][:, 0:1])
            put(buf_a, h, logits(kv2, mk2, h))
        for h in range(AT_HEADS):
            softmax_pv(kv1, h, buf_b, far_ref[h][:, 0:1])
        return carry

    lax.fori_loop(0, (nfar + 1) // 2, pair_body, 0)
    att = jnp.concatenate([(acc_scr[h] / l_scr[h][0:1, :]).T for h in range(AT_HEADS)], axis=1)
    y = _bdot(att.astype(BF16), wout_ref[...])
    o_ref[0] = _ln(alpha * x_ref[0] + y, g_ref[...], b_ref[...])


def dsa_prompt(x, qb, qi, wi, kibf, kbf, vbf, tile, far, w_out, g, b, alpha):
    n, l, d = x.shape
    n_sel = min(TOPK_MAX, l // 4)
    nck_max = (l + DSA_CK - 1) // DSA_CK
    idx_bits = max(1, int(nck_max * DSA_CK - 1).bit_length())
    nqb = l // DSA_TQ
    vt = vbf.reshape(n, nck_max, DSA_CK, AT_HEADS, AT_DH).transpose(0, 3, 1, 4, 2)
    tok = lambda w: pl.BlockSpec((1, DSA_TQ, w), lambda s, t: (s, t, 0))
    seq = lambda a: pl.BlockSpec((1,) + a.shape[1:], lambda s, t: (s,) + (0,) * (a.ndim - 1),
                                 pipeline_mode=pl.Buffered(1))
    return pl.pallas_call(
        functools.partial(_dsa_prompt_kernel, alpha=alpha, n_sel=n_sel, idx_bits=idx_bits),
        grid=(n, nqb),
        in_specs=[tok(d), tok(d), tok(qi.shape[2]), tok(LANES), seq(kibf), seq(kbf), seq(vt),
                  _const(tile.shape), _const(far.shape), _const(w_out.shape), _const((1, d)), _const((1, d))],
        out_specs=tok(d),
        out_shape=jax.ShapeDtypeStruct((n, l, d), F32),
        scratch_shapes=[pltpu.VMEM((nck_max + 1, DSA_CK, DSA_TQ), I32), pltpu.VMEM((SUBLANES, DSA_TQ), I32),
                        pltpu.VMEM((AT_HEADS, DSA_CK, DSA_TQ), F32), pltpu.VMEM((AT_HEADS, DSA_CK, DSA_TQ), F32),
                        pltpu.VMEM((AT_HEADS, DSA_CK, DSA_TQ), BF16), pltpu.VMEM((2, AT_HEADS, SUBLANES, DSA_TQ), F32),
                        pltpu.VMEM((AT_HEADS, SUBLANES, DSA_TQ), F32), pltpu.VMEM((AT_HEADS, SUBLANES, DSA_TQ), F32),
                        pltpu.VMEM((AT_HEADS, AT_DH, DSA_TQ), F32)],
        compiler_params=_params("parallel", "arbitrary"),
    )(x, qb, qi, wi, kibf, kbf, vt, tile, far, w_out, g, b)


def _dsa_sample_score_kernel(pt_ref, qi_ref, wi_ref, *refs):
    o_ref = refs[-1]
    for p, kidx_ref in enumerate(refs[:-1]):
        kp = kidx_ref[0, 0].astype(BF16)
        dots = _bdot(qi_ref[0], kp)
        o_ref[0, :, p * PAGE:(p + 1) * PAGE] = jnp.sum(wi_ref[0] * jnp.maximum(dots, 0.0), axis=0, keepdims=True)


def _dsa_sample_select_kernel(sc_ref, qi_ref, ki_ref, wi_ref, sel_ref, key_scr, cut_scr, *, n_sel, idx_bits):
    nb = sc_ref.shape[1]
    pw = PAGE
    npg = sc_ref.shape[0] // pw
    for c in range(npg):
        key_scr[c] = _order_key(sc_ref[c * pw:(c + 1) * pw, :])
    qi = qi_ref[...].astype(F32)
    ki = ki_ref[...].astype(F32)
    wi = wi_ref[...]
    s = jnp.zeros((1, nb), F32)
    for h in range(IDX_HEADS):
        dot = jnp.sum(qi[h * IDX_DIM:(h + 1) * IDX_DIM, :] * ki, axis=0, keepdims=True)
        s = s + wi[h:h + 1, :] * jnp.maximum(dot, 0.0)
    row = lax.broadcasted_iota(I32, (pw, nb), 0)
    key_scr[npg] = _order_key(jnp.where(row == 0, s, NEG_BIG))
    _topk_select(key_scr, cut_scr, npg + 1, n_sel, idx_bits, lambda c, selected: jnp.where(selected, 1, 0))
    for c in range(npg + 1):
        sel_ref[c * pw:(c + 1) * pw, :] = key_scr[c]


def _dsa_sample_attn_kernel(pt_ref, q_ref, sel_ref, b8_ref, sb_ref, kn_ref, vn_ref, *refs):
    npg = (len(refs) - 1) // 2
    k_refs, v_refs, o_ref = refs[:npg], refs[npg:2 * npg], refs[-1]
    rows = PAGE * AT_HEADS
    past = npg * PAGE
    q = q_ref[0]
    s = jnp.concatenate([lax.dot_general(q, kr[0, 0].astype(BF16), NT, preferred_element_type=F32)
                         for kr in k_refs], axis=1)
    ok = jnp.broadcast_to(sel_ref[0, :, 0:npg * rows], s.shape) > 0
    s = jnp.where(ok, s + b8_ref[:, 0:npg * rows], NEG_BIG)
    sn = jnp.sum(q.astype(F32) * kn_ref[0].astype(F32), axis=1, keepdims=True) + sb_ref[:, past:past + 1]
    okn = jnp.broadcast_to(sel_ref[0, :, npg * rows:npg * rows + 1], sn.shape) > 0
    sn = jnp.where(okn, sn, NEG_BIG)
    m = jnp.maximum(jnp.max(s, axis=1, keepdims=True), sn)
    p = jnp.exp2(s - m)
    pn = jnp.exp2(sn - m)
    l = jnp.sum(p, axis=1, keepdims=True) + pn
    acc = pn.astype(BF16).astype(F32) * vn_ref[0].astype(F32)
    for i, vr in enumerate(v_refs):
        acc = acc + _bdot(p[:, i * rows:(i + 1) * rows].astype(BF16), vr[0, 0].astype(BF16))
    o_ref[0] = acc / l


def dsa_sample(x, j, qb, qib, kiln, wi, kbf, vbf, cache_k, cache_v, cache_kidx, page_table, sbias,
               w_out, g, b, alpha):
    n, d = x.shape
    npg = page_table.shape[1]
    past = npg * PAGE
    n_sel = min(TOPK_MAX, (past + 1) // 4)
    idx_bits = int((npg + 1) * PAGE - 1).bit_length()
    pt = page_table.reshape(-1)
    npool = cache_k.shape[1]

    def paged(shape):
        return [pl.BlockSpec((1, 1) + shape, lambda i, t, p=p: (j, t[i * npg + p], 0, 0)) for p in range(npg)]

    scores = pl.pallas_call(
        _dsa_sample_score_kernel,
        grid_spec=pltpu.PrefetchScalarGridSpec(
            num_scalar_prefetch=1, grid=(n,),
            in_specs=[pl.BlockSpec((1, IDX_HEADS, IDX_DIM), lambda i, t: (i, 0, 0)),
                      pl.BlockSpec((1, IDX_HEADS, 1), lambda i, t: (i, 0, 0))] + paged((IDX_DIM, PAGE)),
            out_specs=pl.BlockSpec((1, 1, past), lambda i, t: (i, 0, 0))),
        out_shape=jax.ShapeDtypeStruct((n, 1, past), F32),
        compiler_params=_params("parallel"),
    )(pt, qib.reshape(n, IDX_HEADS, IDX_DIM), wi[:, 0:IDX_HEADS].reshape(n, IDX_HEADS, 1),
      *([jnp.swapaxes(cache_kidx, 2, 3)] * npg))

    sel = pl.pallas_call(
        functools.partial(_dsa_sample_select_kernel, n_sel=n_sel, idx_bits=idx_bits),
        out_shape=jax.ShapeDtypeStruct((past + PAGE, n), I32),
        scratch_shapes=[pltpu.VMEM((npg + 1, PAGE, n), I32), pltpu.VMEM((SUBLANES, n), I32)],
        compiler_params=pltpu.CompilerParams(vmem_limit_bytes=VMEM_LIMIT),
    )(scores.reshape(n, past).T, qib.T, kiln.astype(BF16).T, wi[:, 0:IDX_HEADS].T).T

    rows = PAGE * AT_HEADS
    sel8 = jnp.repeat(sel, AT_HEADS, axis=1).reshape(n, 1, (npg + 1) * rows)
    rep = jnp.repeat(sbias, AT_HEADS, axis=1)
    own_head = (jnp.arange(rep.shape[1]) % AT_HEADS)[None, :] == jnp.arange(AT_HEADS)[:, None]
    b8 = jnp.where(own_head, rep, NEG_BIG)
    ck4 = cache_k.reshape(cache_k.shape[0], npool, rows, AT_DH)
    cv4 = cache_v.reshape(cache_v.shape[0], npool, rows, AT_DH)
    heads = pl.BlockSpec((1, AT_HEADS, AT_DH), lambda i, t: (i, 0, 0))
    o = pl.pallas_call(
        _dsa_sample_attn_kernel,
        grid_spec=pltpu.PrefetchScalarGridSpec(
            num_scalar_prefetch=1, grid=(n,),
            in_specs=[heads, pl.BlockSpec((1, 1, sel8.shape[2]), lambda i, t: (i, 0, 0)),
                      pl.BlockSpec(b8.shape, lambda i, t: (0, 0), pipeline_mode=pl.Buffered(1)),
                      pl.BlockSpec(sbias.shape, lambda i, t: (0, 0), pipeline_mode=pl.Buffered(1)),
                      heads, heads] + paged((rows, AT_DH)) + paged((rows, AT_DH)),
            out_specs=heads),
        out_shape=jax.ShapeDtypeStruct((n, AT_HEADS, AT_DH), F32),
        compiler_params=_params("parallel"),
    )(pt, qb.reshape(n, AT_HEADS, AT_DH), sel8, b8, sbias, kbf.reshape(n, AT_HEADS, AT_DH),
      vbf.reshape(n, AT_HEADS, AT_DH), *([ck4] * npg), *([cv4] * npg))
    return mm_res_ln(o.reshape(n, d), w_out, x, g, b, alpha, n)


MLP_TM = 512


def kernel(x_prompt, x_sample, state_hgrn, cache_k, cache_v, cache_kidx, page_table, state_s5_re, state_s5_im, hg_w_in, hg_lb, hg_gnorm, hg_w_out, dsa_w_in, dsa_kln_g, dsa_kln_b, dsa_w_out, rel_bias, s5_a_re, s5_a_im, s5_log_dt, s5_b_re, s5_b_im, s5_c_re, s5_c_im, s5_d, s5_w_glu, s5_b_glu, ln_mix_g, ln_mix_b, mlp_w1, mlp_w2, ln_ffn_g, ln_ffn_b):
    n, l, d = x_prompt.shape
    nb = x_sample.shape[0]
    depth = ln_mix_g.shape[0]
    alpha = (2 * depth) ** 0.25
    past = page_table.shape[1] * PAGE
    row = lambda a: a.reshape(1, -1)
    xp = x_prompt
    xs = x_sample.reshape(nb, d)
    hg_p, hg_s = [], None
    w1_all, w2_all = mlp_w1.astype(BF16), mlp_w2.astype(BF16)
    k_p, v_p, ki_p, k_s, v_s, ki_s = [], [], [], [], [], []
    s5r_p, s5i_p, s5r_s, s5i_s = [], [], [], []
    for i in range(depth):
        j = i // N_MIXERS
        kind = i % N_MIXERS
        g, b = row(ln_mix_g[i]), row(ln_mix_b[i])
        if kind == 0:
            w_in, w_out, gn = hg_w_in[j].astype(BF16), hg_w_out[j].astype(BF16), row(hg_gnorm[j])
            xp, st_p = hgrn_prompt(xp, w_in, hg_lb, gn, w_out, g, b, alpha, i)
            xs, hg_s = hgrn_sample(xs, state_hgrn, j, hg_s, w_in, hg_lb, gn, w_out, g, b, alpha, i)
            hg_p.append(st_p)
        elif kind == 1:
            w_out = dsa_w_out[j].astype(BF16)
            tile, far, sbias = dsa_bias(rel_bias, past)
            qb, k, v, kbf, vbf, qib, kiln, kibf, wi = dsa_proj(xp.reshape(n * l, d), dsa_w_in[j],
                                                               dsa_kln_g[j], dsa_kln_b[j], MLP_TM)
            seq = lambda a: a.reshape(n, l, a.shape[1])
            xp = dsa_prompt(xp, seq(qb), seq(qib), seq(wi), seq(kibf), seq(kbf), seq(vbf), tile, far,
                            w_out, g, b, alpha)
            k_p.append(k.reshape(n, l, AT_HEADS, AT_DH))
            v_p.append(v.reshape(n, l, AT_HEADS, AT_DH))
            ki_p.append(kiln.reshape(n, l, IDX_DIM))
            qb, k, v, kbf, vbf, qib, kiln, kibf, wi = dsa_proj(xs, dsa_w_in[j], dsa_kln_g[j], dsa_kln_b[j], nb)
            xs = dsa_sample(xs, j, qb, qib, kiln, wi, kbf, vbf, cache_k, cache_v, cache_kidx, page_table,
                            sbias, w_out, g, b, alpha)
            k_s.append(k.reshape(nb, 1, AT_HEADS, AT_DH))
            v_s.append(v.reshape(nb, 1, AT_HEADS, AT_DH))
            ki_s.append(kiln.reshape(nb, 1, IDX_DIM))
        else:
            prm = s5_params(s5_a_re[j], s5_a_im[j], s5_log_dt[j], s5_b_re[j], s5_b_im[j], s5_c_re[j],
                            s5_c_im[j], s5_d[j], s5_w_glu[j], s5_b_glu[j])
            sg, sp = s5_a_re.shape[1], s5_a_re.shape[2]
            xp, hr, hi = s5_prompt(xp, prm, g, b, alpha)
            xs, hr2, hi2 = s5_sample(xs, state_s5_re[j].reshape(nb, sg * sp), state_s5_im[j].reshape(nb, sg * sp),
                                     prm, g, b, alpha)
            s5r_p.append(hr.reshape(n, sg, sp))
            s5i_p.append(hi.reshape(n, sg, sp))
            s5r_s.append(hr2.reshape(nb, sg, sp))
            s5i_s.append(hi2.reshape(nb, sg, sp))
        fg, fb = row(ln_ffn_g[i]), row(ln_ffn_b[i])
        xp = mlp_res_ln(xp.reshape(n * l, d), w1_all, w2_all, i, fg, fb, alpha, MLP_TM).reshape(n, l, d)
        xs = mlp_res_ln(xs, w1_all, w2_all, i, fg, fb, alpha, nb)
    return (xp, xs.reshape(nb, 1, d), jnp.stack(hg_p), hg_s, jnp.stack(k_p), jnp.stack(v_p),
            jnp.stack(ki_p), jnp.stack(k_s), jnp.stack(v_s), jnp.stack(ki_s), jnp.stack(s5r_p),
            jnp.stack(s5i_p), jnp.stack(s5r_s), jnp.stack(s5i_s))
```

```python
import functools
import math

import numpy as np
import jax
import jax.numpy as jnp
from jax import lax
from jax.experimental import pallas as pl
from jax.experimental.pallas import tpu as pltpu

F32 = jnp.float32
BF16 = jnp.bfloat16
I32 = jnp.int32

LN_EPS = 1e-5
RMS_EPS = 1e-6
NEG_BIG = -1e30
N_MIXERS = 3
HG_HEADS = 8
HG_DK = 128
AT_HEADS = 8
AT_DH = 128
IDX_HEADS = 8
IDX_DIM = 64
TOPK_MAX = 256
REL_BUCKETS = 32
REL_MAX_DIST = 128
S5_GROUP = 16
S5_STATE = 64
PAGE = 128

LANES = 128
SUBLANES = 8
VMEM_LIMIT = 56 * 1024 * 1024

NT = (((1,), (1,)), ((), ()))
TN = (((0,), (0,)), ((), ()))


def _params(*sem):
    return pltpu.CompilerParams(dimension_semantics=sem, vmem_limit_bytes=VMEM_LIMIT)


def _const(shape):
    nd = len(shape)
    return pl.BlockSpec(shape, lambda *_: (0,) * nd, pipeline_mode=pl.Buffered(1))


def _ln(z, g, b):
    mu = jnp.mean(z, axis=-1, keepdims=True)
    zc = z - mu
    var = jnp.mean(zc * zc, axis=-1, keepdims=True)
    return zc * lax.rsqrt(var + LN_EPS) * g + b


def _sigmoid(z):
    return 1.0 / (1.0 + jnp.exp(-z))


def _bdot(a, b):
    return jnp.dot(a, b, preferred_element_type=F32)


def _mlp_kernel(x_ref, w1_ref, w2_ref, g_ref, b_ref, o_ref, *, alpha, hc):
    x = x_ref[...]
    xb = x.astype(BF16)
    acc = jnp.zeros(x.shape, F32)
    for c in range(w1_ref.shape[2] // hc):
        h = _bdot(xb, w1_ref[0, :, c * hc:(c + 1) * hc])
        h = jnp.maximum(h, 0.0)
        acc = acc + _bdot((h * h).astype(BF16), w2_ref[0, c * hc:(c + 1) * hc, :])
    o_ref[...] = _ln(alpha * x + acc, g_ref[...], b_ref[...])


def mlp_res_ln(x, w1, w2, layer, g, b, alpha, tm):
    t, d = x.shape
    f = w1.shape[2]
    pick = lambda r, c: pl.BlockSpec((1, r, c), lambda i: (layer, 0, 0), pipeline_mode=pl.Buffered(1))
    return pl.pallas_call(
        functools.partial(_mlp_kernel, alpha=alpha, hc=1024),
        grid=(t // tm,),
        in_specs=[pl.BlockSpec((tm, d), lambda i: (i, 0)), pick(d, f), pick(f, d),
                  _const((1, d)), _const((1, d))],
        out_specs=pl.BlockSpec((tm, d), lambda i: (i, 0)),
        out_shape=jax.ShapeDtypeStruct((t, d), F32),
        compiler_params=_params("parallel"),
    )(x, w1, w2, g, b)


def _mm_kernel(x_ref, w_ref, o_ref):
    o_ref[...] = _bdot(x_ref[...].astype(BF16), w_ref[...])


def mm(x, w, tm):
    t, d = x.shape
    n = w.shape[1]
    return pl.pallas_call(
        _mm_kernel,
        grid=(t // tm,),
        in_specs=[pl.BlockSpec((tm, d), lambda i: (i, 0)), _const((d, n))],
        out_specs=pl.BlockSpec((tm, n), lambda i: (i, 0)),
        out_shape=jax.ShapeDtypeStruct((t, n), F32),
        compiler_params=_params("parallel"),
    )(x, w)


def _mm_res_ln_kernel(a_ref, w_ref, x_ref, g_ref, b_ref, o_ref, *, alpha):
    y = _bdot(a_ref[...].astype(BF16), w_ref[...])
    o_ref[...] = _ln(alpha * x_ref[...] + y, g_ref[...], b_ref[...])


def mm_res_ln(a, w, x, g, b, alpha, tm):
    t, k = a.shape
    d = w.shape[1]
    return pl.pallas_call(
        functools.partial(_mm_res_ln_kernel, alpha=alpha),
        grid=(t // tm,),
        in_specs=[pl.BlockSpec((tm, k), lambda i: (i, 0)), _const((k, d)),
                  pl.BlockSpec((tm, d), lambda i: (i, 0)), _const((1, d)), _const((1, d))],
        out_specs=pl.BlockSpec((tm, d), lambda i: (i, 0)),
        out_shape=jax.ShapeDtypeStruct((t, d), F32),
        compiler_params=_params("parallel"),
    )(a, w, x, g, b)


HG_C = 128
HG_LEVELS = 7


def _hgrn_constants():
    c = HG_C
    t = np.arange(c)[:, None]
    u = np.arange(c)[None, :]
    m = (u <= t).astype(np.float32)
    x = t ^ u
    lv = np.where(u < t, np.floor(np.log2(np.maximum(x, 1))).astype(np.int32),
                  np.where(u == t, HG_LEVELS, -1)).astype(np.int32)
    return np.concatenate([m, m, m], axis=1), lv


def _lower_bound(lb_ref, layer):
    rows = [lb_ref[i] for i in range(lb_ref.shape[0])]
    mx = rows[0]
    for r in rows[1:]:
        mx = jnp.maximum(mx, r)
    ex = [jnp.exp(r - mx) for r in rows]
    tot = ex[0]
    for e in ex[1:]:
        tot = tot + e
    num = jnp.zeros_like(tot)
    for i in range(1, layer + 1):
        num = num + ex[i]
    return num / tot


def _split3(x):
    hi = x.astype(BF16)
    r1 = x - hi.astype(F32)
    mid = r1.astype(BF16)
    lo = (r1 - mid.astype(F32)).astype(BF16)
    return hi, mid, lo


def _hgrn_prompt_kernel(x_ref, win_ref, lb_ref, gn_ref, wout_ref, g_ref, b_ref, mst_ref, lv_ref,
                        o_ref, st_ref, og_scr, *, alpha, layer):
    c = HG_C
    dk = HG_DK

    @pl.when(pl.program_id(1) == 0)
    def _():
        st_ref[...] = jnp.zeros(st_ref.shape, F32)

    x = x_ref[0]
    d = x.shape[1]
    proj = _bdot(x.astype(BF16), win_ref[...])
    qp = proj[:, 0:d]
    fz = proj[:, d:2 * d]
    v = proj[:, 2 * d:3 * d]
    gp = proj[:, 3 * d:4 * d]
    lb = _lower_bound(lb_ref, layer)
    q = qp * _sigmoid(qp)
    sg = _sigmoid(fz)
    f = lb + (1.0 - lb) * sg
    k = (1.0 - lb) * (1.0 - sg)
    logf = jnp.log(f)
    gate = gp * _sigmoid(gp)
    hi, mid, lo = _split3(logf)
    bsum = _bdot(mst_ref[...], jnp.concatenate([hi, mid, lo], axis=0))

    lv = lv_ref[...]
    row = lax.broadcasted_iota(I32, (c, dk), 0)
    second = [((row >> l) & 1) == 1 for l in range(HG_LEVELS)]
    sub = lax.broadcasted_iota(I32, (c // SUBLANES, SUBLANES, dk), 1)

    def level_exponent(l, bh, lfh):
        half = 1 << l
        blk = 2 * half
        if l == 0:
            return jnp.where(second[0], lfh, 0.0)
        if half >= SUBLANES:
            parts = []
            for i in range(c // blk):
                r = i * blk + half - 1
                parts.append(bh[i * blk:(i + 1) * blk] - jnp.broadcast_to(bh[r:r + 1, :], (blk, dk)))
            dlt = jnp.concatenate(parts, axis=0)
        else:
            b3 = bh.reshape(c // SUBLANES, SUBLANES, dk)
            ref_row = jnp.broadcast_to(b3[:, half - 1:half, :], b3.shape)
            for i in range(1, SUBLANES // blk):
                nxt = jnp.broadcast_to(b3[:, i * blk + half - 1:i * blk + half, :], b3.shape)
                ref_row = jnp.where(sub >= i * blk, nxt, ref_row)
            dlt = (b3 - ref_row).reshape(c, dk)
        return jnp.where(second[l], dlt, -dlt)

    for h in range(HG_HEADS):
        hs = slice(h * dk, (h + 1) * dk)
        qh = q[:, hs]
        kh = k[:, hs]
        vh = v[:, hs].astype(BF16)
        bh = bsum[:, hs]
        lfh = logf[:, hs]
        st = st_ref[0, h]
        eb = jnp.exp(bh)
        o = lax.dot_general((qh * eb).astype(BF16), st.astype(BF16), NT, preferred_element_type=F32)
        att = jnp.zeros((c, c), F32)
        for l in range(HG_LEVELS):
            e = jnp.exp(level_exponent(l, bh, lfh))
            w = (jnp.where(second[l], qh, kh) * e).astype(BF16)
            a = lax.dot_general(w, w, NT, preferred_element_type=F32)
            att = jnp.where(lv == l, a, att)
        a0 = lax.dot_general(qh.astype(BF16), kh.astype(BF16), NT, preferred_element_type=F32)
        att = jnp.where(lv == HG_LEVELS, a0, att)
        o = o + _bdot(att.astype(BF16), vh)
        kd = (kh * jnp.exp(jnp.broadcast_to(bh[c - 1:c, :], (c, dk)) - bh)).astype(BF16)
        st_ref[0, h] = eb[c - 1:c, :] * st + lax.dot_general(vh, kd, TN, preferred_element_type=F32)
        ms = jnp.mean(o * o, axis=-1, keepdims=True)
        og_scr[:, hs] = o * lax.rsqrt(ms + RMS_EPS) * gn_ref[:, hs] * gate[:, hs]
    y = _bdot(og_scr[...].astype(BF16), wout_ref[...])
    o_ref[0] = _ln(alpha * x + y, g_ref[...], b_ref[...])


def hgrn_prompt(x, w_in, hg_lb, gnorm, w_out, g, b, alpha, layer):
    n, l, d = x.shape
    mst, lv = _hgrn_constants()
    nl = hg_lb.shape[0]
    out, st = pl.pallas_call(
        functools.partial(_hgrn_prompt_kernel, alpha=alpha, layer=layer),
        grid=(n, l // HG_C),
        in_specs=[pl.BlockSpec((1, HG_C, d), lambda s, c: (s, c, 0)),
                  _const((d, 4 * d)), _const((nl, 1, d)), _const((1, d)), _const((d, d)),
                  _const((1, d)), _const((1, d)), _const(mst.shape), _const(lv.shape)],
        out_specs=[pl.BlockSpec((1, HG_C, d), lambda s, c: (s, c, 0)),
                   pl.BlockSpec((1, HG_HEADS, HG_DK, HG_DK), lambda s, c: (s, 0, 0, 0))],
        out_shape=[jax.ShapeDtypeStruct((n, l, d), F32),
                   jax.ShapeDtypeStruct((n, HG_HEADS, HG_DK, HG_DK), F32)],
        scratch_shapes=[pltpu.VMEM((HG_C, d), F32)],
        compiler_params=_params("parallel", "arbitrary"),
    )(x, w_in, hg_lb.reshape(nl, 1, d), gnorm, w_out, g, b,
      jnp.asarray(mst, BF16), jnp.asarray(lv))
    return out, jnp.swapaxes(st, 2, 3)


HG_SG = 4


def _hgrn_sample_kernel(qf_ref, lbt_ref, v_ref, gp_ref, gn_ref, s0_ref, *rest, layer):
    o_ref, s1_ref = rest[-2], rest[-1]
    dk = HG_DK
    lbt = _lower_bound(lbt_ref, layer)
    for i in range(HG_SG):
        qf = qf_ref[i]
        qt = qf[:, 0:HG_HEADS]
        q = qt * _sigmoid(qt)
        sg = _sigmoid(qf[:, HG_HEADS:2 * HG_HEADS])
        f = lbt + (1.0 - lbt) * sg
        k = (1.0 - lbt) * (1.0 - sg)
        gp = gp_ref[i]
        gate = gp * _sigmoid(gp)
        for h in range(HG_HEADS):
            hs = slice(h * dk, (h + 1) * dk)
            s_new = f[:, h:h + 1] * s0_ref[0, i, h] + k[:, h:h + 1] * v_ref[i, :, hs]
            s1_ref[0, i, h] = s_new
            o = jnp.sum(s_new * q[:, h:h + 1], axis=0, keepdims=True)
            ms = jnp.mean(o * o, axis=-1, keepdims=True)
            o_ref[i, :, hs] = o * lax.rsqrt(ms + RMS_EPS) * gn_ref[:, hs] * gate[:, hs]


def hgrn_sample(x, states, j, new_states, w_in, hg_lb, gnorm, w_out, g, b, alpha, layer):
    n, d = x.shape
    nl = hg_lb.shape[0]
    proj = mm(x, w_in, n)
    to_cols = lambda a: a.reshape(n, HG_HEADS, HG_DK).transpose(0, 2, 1)
    lbt = hg_lb.reshape(nl, HG_HEADS, HG_DK).transpose(0, 2, 1)
    row3 = lambda a: a.reshape(n, 1, d)
    qf = jnp.concatenate([to_cols(proj[:, 0:d]), to_cols(proj[:, d:2 * d]),
                          jnp.zeros((n, HG_DK, LANES - 2 * HG_HEADS), F32)], axis=2)
    cols = pl.BlockSpec((HG_SG, HG_DK, LANES), lambda i: (i, 0, 0))
    rows = pl.BlockSpec((HG_SG, 1, d), lambda i: (i, 0, 0))
    st = pl.BlockSpec((1, HG_SG, HG_HEADS, HG_DK, HG_DK), lambda i: (j, i, 0, 0, 0))
    carried = [] if new_states is None else [new_states]
    o, s1 = pl.pallas_call(
        functools.partial(_hgrn_sample_kernel, layer=layer),
        grid=(n // HG_SG,),
        in_specs=[cols, _const((nl, HG_DK, HG_HEADS)), rows, rows, _const((1, d)), st]
                 + [pl.BlockSpec(memory_space=pl.ANY)] * len(carried),
        out_specs=[rows, st],
        out_shape=[jax.ShapeDtypeStruct((n, 1, d), F32), jax.ShapeDtypeStruct(states.shape, F32)],
        input_output_aliases={6: 1} if carried else {},
        compiler_params=_params("parallel"),
    )(qf, lbt, row3(proj[:, 2 * d:3 * d]), row3(proj[:, 3 * d:4 * d]), gnorm, states, *carried)
    return mm_res_ln(o.reshape(n, d), w_out, x, g, b, alpha, n), s1


S5_TC = 128
S5_SLABS = 8
S5_FOLD = 4


def _cmul(ar, ai, br, bi):
    return ar * br - ai * bi, ar * bi + ai * br


def _s5_prep_kernel(are_ref, aim_ref, ldt_ref, are2_ref, aim2_ref, ldt2_ref, bre_ref, bim_ref,
                    pwre_ref, pwim_ref, ckre_ref, ckim_ref, bbre_ref, bbim_ref):
    dt = jnp.exp(ldt_ref[...])
    mag = jnp.exp(dt * are_ref[...])
    a_r = mag * jnp.cos(dt * aim_ref[...])
    a_i = mag * jnp.sin(dt * aim_ref[...])
    pr, pi_ = [a_r], [a_i]
    for _ in range(SUBLANES - 1):
        nr, ni = _cmul(pr[-1], pi_[-1], a_r, a_i)
        pr.append(nr)
        pi_.append(ni)
    pwre_ref[...] = jnp.concatenate(pr, axis=0)
    pwim_ref[...] = jnp.concatenate(pi_, axis=0)
    zero = jnp.zeros_like(a_r)
    ckre_ref[...] = jnp.concatenate([zero] * S5_FOLD + pr[0:SUBLANES - S5_FOLD], axis=0)
    ckim_ref[...] = jnp.concatenate([zero] * S5_FOLD + pi_[0:SUBLANES - S5_FOLD], axis=0)
    ar, ai = are2_ref[...], aim2_ref[...]
    dt2 = jnp.exp(ldt2_ref[...])
    mag2 = jnp.exp(dt2 * ar)
    a2r = mag2 * jnp.cos(dt2 * ai)
    a2i = mag2 * jnp.sin(dt2 * ai)
    nr = a2r - 1.0
    ni = a2i
    den = ar * ar + ai * ai
    z_re = (nr * ar + ni * ai) / den
    z_im = (ni * ar - nr * ai) / den
    br, bi = bre_ref[...], bim_ref[...]
    for j in range(S5_FOLD):
        bbre_ref[j] = z_re[:, None, :] * br - z_im[:, None, :] * bi
        bbim_ref[j] = z_re[:, None, :] * bi + z_im[:, None, :] * br
        z_re, z_im = _cmul(z_re, z_im, a2r, a2i)


def s5_prep(a_re, a_im, log_dt, b_re, b_im):
    g, p = a_re.shape
    k = b_re.shape[2]
    n = g * p
    ldt2 = jnp.broadcast_to(log_dt[:, None], (g, p))
    flat = lambda a: a.reshape(1, n)
    return pl.pallas_call(
        _s5_prep_kernel,
        out_shape=[jax.ShapeDtypeStruct((SUBLANES, n), F32), jax.ShapeDtypeStruct((SUBLANES, n), F32),
                   jax.ShapeDtypeStruct((SUBLANES, n), F32), jax.ShapeDtypeStruct((SUBLANES, n), F32),
                   jax.ShapeDtypeStruct((S5_FOLD, g, k, p), F32), jax.ShapeDtypeStruct((S5_FOLD, g, k, p), F32)],
    )(flat(a_re), flat(a_im), flat(ldt2), a_re, a_im, ldt2,
      b_re.transpose(0, 2, 1), b_im.transpose(0, 2, 1))


def _block_diag_slabs(w_gkp, rows_are_inputs):
    g, k, p = w_gkp.shape
    per = g // S5_SLABS
    w4 = w_gkp.reshape(S5_SLABS, per, k, p)
    eye = jnp.eye(per, dtype=w_gkp.dtype)
    if rows_are_inputs:
        return jnp.einsum('iakp,ab->iakbp', w4, eye).reshape(S5_SLABS, per * k, per * p)
    return jnp.einsum('iakp,ab->ibpak', w4, eye).reshape(S5_SLABS, per * p, per * k)


def _s5_tail(x, y, d_ref, wglu_ref, bglu_ref, g_ref, b_ref, alpha):
    d = x.shape[1]
    gl = jax.nn.gelu(y + d_ref[...] * x)
    u = _bdot(gl.astype(BF16), wglu_ref[...]) + bglu_ref[...]
    out = u[:, 0:d] * _sigmoid(u[:, d:2 * d])
    return _ln(alpha * x + out, g_ref[...], b_ref[...])


def _s5_prompt_kernel(x_ref, wbre_ref, wbim_ref, wcre_ref, wcim_ref, pwre_ref, pwim_ref, ckre_ref, ckim_ref,
                      d_ref, wglu_ref, bglu_ref, g_ref, b_ref,
                      o_ref, hre_ref, him_ref, cre_scr, cim_scr, y_scr, *, alpha):
    tc = x_ref.shape[1]
    ng = tc // SUBLANES
    ci = pl.program_id(1)

    @pl.when(ci == 0)
    def _():
        cre_scr[...] = jnp.zeros(cre_scr.shape, F32)
        cim_scr[...] = jnp.zeros(cim_scr.shape, F32)

    x = x_ref[0]
    rin = lax.broadcasted_iota(I32, x.shape, 0) & (S5_FOLD - 1)
    lags = [x.astype(BF16)] + [jnp.where(rin >= j, pltpu.roll(x, j, axis=0), 0.0).astype(BF16)
                               for j in range(1, S5_FOLD)]
    sw = wbre_ref.shape[2]
    for i in range(S5_SLABS):
        cs = slice(i * sw, (i + 1) * sw)
        xi = jnp.concatenate([xl[:, i * LANES:(i + 1) * LANES] for xl in lags], axis=1)
        br = _bdot(xi, wbre_ref[i]).reshape(ng, SUBLANES, sw)
        bi = _bdot(xi, wbim_ref[i]).reshape(ng, SUBLANES, sw)
        sr = jnp.broadcast_to(br[:, S5_FOLD - 1:S5_FOLD, :], br.shape)
        si = jnp.broadcast_to(bi[:, S5_FOLD - 1:S5_FOLD, :], bi.shape)
        tr, ti = _cmul(ckre_ref[:, cs], ckim_ref[:, cs], sr, si)
        br, bi = br + tr, bi + ti
        pr, pi_ = pwre_ref[:, cs], pwim_ref[:, cs]
        hpr, hpi = cre_scr[:, cs], cim_scr[:, cs]
        hr_all, hi_all = [], []
        for m in range(ng):
            tr, ti = _cmul(pr, pi_, hpr, hpi)
            hr, hi = br[m] + tr, bi[m] + ti
            hr_all.append(hr)
            hi_all.append(hi)
            hpr = jnp.broadcast_to(hr[SUBLANES - 1:SUBLANES, :], hr.shape)
            hpi = jnp.broadcast_to(hi[SUBLANES - 1:SUBLANES, :], hi.shape)
        cre_scr[:, cs] = hpr
        cim_scr[:, cs] = hpi
        hr = jnp.concatenate(hr_all, axis=0).astype(BF16)
        hi = jnp.concatenate(hi_all, axis=0).astype(BF16)
        y_scr[:, i * LANES:(i + 1) * LANES] = _bdot(hr, wcre_ref[i]) - _bdot(hi, wcim_ref[i])
    o_ref[0] = _s5_tail(x, y_scr[...], d_ref, wglu_ref, bglu_ref, g_ref, b_ref, alpha)

    @pl.when(ci == pl.num_programs(1) - 1)
    def _():
        hre_ref[0] = cre_scr[0:1, :]
        him_ref[0] = cim_scr[0:1, :]


def _s5_weight_specs(wb, wc, pw, ck, d, wglu):
    return [_const(wb.shape), _const(wb.shape), _const(wc.shape), _const(wc.shape),
            _const(pw.shape), _const(pw.shape), _const(ck.shape), _const(ck.shape),
            _const((1, d)), _const(wglu.shape), _const((1, wglu.shape[1])), _const((1, d)), _const((1, d))]


def s5_prompt(x, prm, g, b, alpha):
    n, l, d = x.shape
    wb, wc, pw, ck, wglu = prm[0], prm[2], prm[4], prm[6], prm[9]
    ns = pw.shape[1]
    out, hre, him = pl.pallas_call(
        functools.partial(_s5_prompt_kernel, alpha=alpha),
        grid=(n, l // S5_TC),
        in_specs=[pl.BlockSpec((1, S5_TC, d), lambda s, c: (s, c, 0))] + _s5_weight_specs(wb, wc, pw, ck, d, wglu),
        out_specs=[pl.BlockSpec((1, S5_TC, d), lambda s, c: (s, c, 0)),
                   pl.BlockSpec((1, 1, ns), lambda s, c: (s, 0, 0)),
                   pl.BlockSpec((1, 1, ns), lambda s, c: (s, 0, 0))],
        out_shape=[jax.ShapeDtypeStruct((n, l, d), F32),
                   jax.ShapeDtypeStruct((n, 1, ns), F32), jax.ShapeDtypeStruct((n, 1, ns), F32)],
        scratch_shapes=[pltpu.VMEM((SUBLANES, ns), F32), pltpu.VMEM((SUBLANES, ns), F32),
                        pltpu.VMEM((S5_TC, d), F32)],
        compiler_params=_params("parallel", "arbitrary"),
    )(x, *prm, g, b)
    return out, hre, him


def _s5_sample_kernel(x_ref, h0re_ref, h0im_ref, wbre_ref, wbim_ref, wcre_ref, wcim_ref, pwre_ref, pwim_ref,
                      ckre_ref, ckim_ref, d_ref, wglu_ref, bglu_ref, g_ref, b_ref,
                      o_ref, hre_ref, him_ref, y_scr, *, alpha):
    x = x_ref[...]
    xb = x.astype(BF16)
    sw = wbre_ref.shape[2]
    for i in range(S5_SLABS):
        cs = slice(i * sw, (i + 1) * sw)
        xi = xb[:, i * LANES:(i + 1) * LANES]
        tr, ti = _cmul(pwre_ref[0:1, cs], pwim_ref[0:1, cs], h0re_ref[:, cs], h0im_ref[:, cs])
        hr = _bdot(xi, wbre_ref[i, 0:LANES, :]) + tr
        hi = _bdot(xi, wbim_ref[i, 0:LANES, :]) + ti
        hre_ref[:, cs] = hr
        him_ref[:, cs] = hi
        y_scr[:, i * LANES:(i + 1) * LANES] = (_bdot(hr.astype(BF16), wcre_ref[i])
                                               - _bdot(hi.astype(BF16), wcim_ref[i]))
    o_ref[...] = _s5_tail(x, y_scr[...], d_ref, wglu_ref, bglu_ref, g_ref, b_ref, alpha)


def s5_sample(x, h0re, h0im, prm, g, b, alpha):
    n, d = x.shape
    ns = h0re.shape[1]
    return pl.pallas_call(
        functools.partial(_s5_sample_kernel, alpha=alpha),
        out_shape=[jax.ShapeDtypeStruct((n, d), F32),
                   jax.ShapeDtypeStruct((n, ns), F32), jax.ShapeDtypeStruct((n, ns), F32)],
        scratch_shapes=[pltpu.VMEM((n, d), F32)],
        compiler_params=pltpu.CompilerParams(vmem_limit_bytes=VMEM_LIMIT),
    )(x, h0re, h0im, *prm, g, b)


def s5_params(a_re, a_im, log_dt, b_re, b_im, c_re, c_im, d_skip, w_glu, b_glu):
    pwre, pwim, ckre, ckim, bbre, bbim = s5_prep(a_re, a_im, log_dt, b_re, b_im)
    fold = lambda bb: jnp.concatenate([_block_diag_slabs(bb[j], True) for j in range(S5_FOLD)], axis=1)
    return (fold(bbre).astype(BF16), fold(bbim).astype(BF16),
            _block_diag_slabs(c_re, False).astype(BF16), _block_diag_slabs(c_im, False).astype(BF16),
            pwre, pwim, ckre, ckim, d_skip.reshape(1, -1), w_glu.astype(BF16), b_glu.reshape(1, -1))


DSA_TQ = 256
DSA_SUB = 128
DSA_CK = 512
INT_MIN = -2 ** 31
INT_MAX = 2 ** 31 - 1
LOG2E = math.log2(math.e)


def _dsa_proj_kernel(x_ref, wq_ref, wk_ref, wv_ref, wqi_ref, wki_ref, wwi_ref, kg_ref, kb_ref,
                     qb_ref, k_ref, v_ref, kbf_ref, vbf_ref, qi_ref, kiln_ref, kibf_ref, wi_ref):
    xb = x_ref[...].astype(BF16)
    qb_ref[...] = (_bdot(xb, wq_ref[...]) * (AT_DH ** -0.5 * LOG2E)).astype(BF16)
    k = _bdot(xb, wk_ref[...])
    v = _bdot(xb, wv_ref[...])
    k_ref[...] = k
    v_ref[...] = v
    kbf_ref[...] = k.astype(BF16)
    vbf_ref[...] = v.astype(BF16)
    qi_ref[...] = (_bdot(xb, wqi_ref[...]) * (IDX_DIM ** -0.5)).astype(BF16)
    ki = _ln(_bdot(xb, wki_ref[...])[:, 0:IDX_DIM], kg_ref[...], kb_ref[...])
    kiln_ref[...] = ki
    kibf_ref[...] = ki.astype(BF16)
    wi_ref[...] = _bdot(xb, wwi_ref[...]) * (IDX_HEADS ** -0.5)


def dsa_proj(x, w_in, kln_g, kln_b, tm):
    t, d = x.shape
    hd = AT_HEADS * AT_DH
    qd = IDX_HEADS * IDX_DIM
    wb = w_in.astype(BF16)
    pad = lambda w: jnp.pad(w, ((0, 0), (0, LANES - w.shape[1])))
    ws = [wb[:, 0:hd], wb[:, hd:2 * hd], wb[:, 2 * hd:3 * hd], wb[:, 3 * hd:3 * hd + qd],
          pad(wb[:, 3 * hd + qd:3 * hd + qd + IDX_DIM]), pad(wb[:, 3 * hd + qd + IDX_DIM:])]
    row = lambda n, dt: (pl.BlockSpec((tm, n), lambda i: (i, 0)), jax.ShapeDtypeStruct((t, n), dt))
    outs = [row(hd, BF16), row(hd, F32), row(hd, F32), row(hd, BF16), row(hd, BF16), row(qd, BF16),
            row(IDX_DIM, F32), row(IDX_DIM, BF16), row(LANES, F32)]
    return pl.pallas_call(
        _dsa_proj_kernel,
        grid=(t // tm,),
        in_specs=[pl.BlockSpec((tm, d), lambda i: (i, 0))] + [_const(w.shape) for w in ws]
                 + [_const((1, IDX_DIM)), _const((1, IDX_DIM))],
        out_specs=[o[0] for o in outs],
        out_shape=[o[1] for o in outs],
        compiler_params=_params("parallel"),
    )(x, *ws, kln_g.reshape(1, IDX_DIM), kln_b.reshape(1, IDX_DIM))


def _t5_bucket(rel):
    n = jnp.maximum(rel, 0)
    exact = REL_BUCKETS // 2
    nf = jnp.maximum(n, exact).astype(F32)
    large = exact + (jnp.log(nf / exact) / math.log(REL_MAX_DIST / exact) * (REL_BUCKETS - exact)).astype(I32)
    large = jnp.minimum(large, REL_BUCKETS - 1)
    return jnp.where(n < exact, n, large)


def _bias_lookup(bucket, rb_ref, h):
    out = jnp.zeros(bucket.shape, F32)
    for bk in range(REL_BUCKETS):
        out = jnp.where(bucket == bk, rb_ref[bk, h], out)
    return out


def _dsa_bias_kernel(rb_ref, tile_ref, far_ref, samp_ref, *, past):
    c = lax.broadcasted_iota(I32, tile_ref.shape[1:], 0)
    i = lax.broadcasted_iota(I32, tile_ref.shape[1:], 1)
    bt = _t5_bucket(i - c + DSA_SUB)
    bs = _t5_bucket(past - lax.broadcasted_iota(I32, (1, samp_ref.shape[1]), 1))
    for h in range(AT_HEADS):
        tile_ref[h] = _bias_lookup(bt, rb_ref, h) * LOG2E
        far_ref[h] = jnp.full(far_ref.shape[1:], rb_ref[REL_BUCKETS - 1, h] * LOG2E, F32)
        samp_ref[h:h + 1, :] = _bias_lookup(bs, rb_ref, h) * LOG2E


def dsa_bias(rel_bias, past):
    return pl.pallas_call(
        functools.partial(_dsa_bias_kernel, past=past),
        in_specs=[pl.BlockSpec(memory_space=pltpu.SMEM)],
        out_shape=[jax.ShapeDtypeStruct((AT_HEADS, 2 * DSA_SUB, DSA_SUB), F32),
                   jax.ShapeDtypeStruct((AT_HEADS, 1, LANES), F32),
                   jax.ShapeDtypeStruct((AT_HEADS, past + PAGE), F32)],
    )(rel_bias)


def _order_key(s):
    bits = pltpu.bitcast(s, I32)
    return bits ^ ((bits >> 31) & INT_MAX)


def _count(key_scr, nck, pred):
    cw, nq = key_scr.shape[1], key_scr.shape[2]

    def body(c, acc):
        kk = key_scr[c]
        parts = [jnp.where(pred(kk[g * SUBLANES:(g + 1) * SUBLANES], c, g * SUBLANES), 1.0, 0.0)
                 for g in range(cw // SUBLANES)]
        while len(parts) > 1:
            parts = [parts[i] + parts[i + 1] for i in range(0, len(parts), 2)]
        return acc + parts[0]

    acc = lax.fori_loop(0, nck, body, jnp.zeros((SUBLANES, nq), F32))
    return jnp.sum(acc, axis=0, keepdims=True)


def _topk_select(key_scr, cut_scr, nck, n_sel, idx_bits, emit):
    cw, nq = key_scr.shape[1], key_scr.shape[2]
    nf = float(n_sel)
    kidx = lax.broadcasted_iota(I32, (cw, nq), 0)
    sidx = lax.broadcasted_iota(I32, (SUBLANES, nq), 0)

    cnt = _count(key_scr, nck, lambda kk, c, r: kk >= 0)
    thr = jnp.where(cnt >= nf, 0, INT_MIN).astype(I32)

    def bit_body(i, thr):
        trial = thr | jnp.left_shift(jnp.int32(1), 30 - i)
        cnt = _count(key_scr, nck, lambda kk, c, r: kk >= trial)
        return jnp.where(cnt >= nf, trial, thr)

    thr = lax.fori_loop(0, 31, bit_body, thr)
    cnt_ge = _count(key_scr, nck, lambda kk, c, r: kk >= thr)
    need = nf - _count(key_scr, nck, lambda kk, c, r: kk > thr)

    cut_scr[...] = jnp.full(cut_scr.shape, INT_MAX, I32)

    @pl.when(jnp.max(cnt_ge) > nf)
    def _():
        def tie_body(i, rr):
            t = rr | jnp.left_shift(jnp.int32(1), idx_bits - 1 - i)
            below = _count(key_scr, nck,
                           lambda kk, c, r: jnp.where(kk == thr, c * cw + r + sidx, INT_MAX) < t)
            return jnp.where(below <= need - 1.0, t, rr)

        rr = lax.fori_loop(0, idx_bits, tie_body, jnp.zeros((1, nq), I32))
        cut_scr[...] = jnp.broadcast_to(jnp.where(cnt_ge > nf, rr, INT_MAX), cut_scr.shape)

    cut = cut_scr[0:1, :]

    def sel_body(c, carry):
        kk = key_scr[c]
        tie = jnp.where(c * cw + kidx <= cut, 1, 0)
        key_scr[c] = emit(c, jnp.where(kk > thr, 1, jnp.where(kk == thr, tie, 0)) > 0)
        return carry

    lax.fori_loop(0, nck, sel_body, 0)


def _dsa_prompt_kernel(x_ref, q_ref, qi_ref, wi_ref, ki_ref, k_ref, vt_ref, tile_ref, far_ref,
                       wout_ref, g_ref, b_ref, o_ref, key_scr, cut_scr, sa, sb, pb_scr, cm_scr, m_scr, l_scr, acc_scr,
                       *, alpha, n_sel, idx_bits):
    tq = DSA_TQ
    ck = DSA_CK
    sub = DSA_SUB
    qb = pl.program_id(1)
    q0 = qb * tq
    nck = (q0 + tq + ck - 1) // ck
    kloc = lax.broadcasted_iota(I32, (ck, tq), 0)
    qpos = q0 + lax.broadcasted_iota(I32, (ck, tq), 1)

    wit = wi_ref[0].T
    qis = [qi_ref[0, :, h * IDX_DIM:(h + 1) * IDX_DIM] for h in range(IDX_HEADS)]

    def score_body(c, carry):
        kic = ki_ref[0, pl.ds(pl.multiple_of(c * ck, ck), ck), :]
        s = None
        for h in range(IDX_HEADS):
            dots = lax.dot_general(kic, qis[h], NT, preferred_element_type=F32)
            t = wit[h:h + 1, :] * jnp.maximum(dots, 0.0)
            s = t if s is None else s + t
        s = jnp.where(c * ck + kloc <= qpos, s, NEG_BIG)
        key_scr[c] = _order_key(s)
        return carry

    lax.fori_loop(0, nck, score_body, 0)

    def mask_bits(c, selected):
        keep = jnp.where(c * ck + kloc <= qpos, 0.0, NEG_BIG)
        return pltpu.bitcast(jnp.where(selected, keep, NEG_BIG), I32)

    _topk_select(key_scr, cut_scr, nck, n_sel, idx_bits, mask_bits)

    nmax = key_scr.shape[0] - 1
    key_scr[nmax] = pltpu.bitcast(jnp.full((ck, tq), NEG_BIG, F32), I32)
    m_scr[...] = jnp.full(m_scr.shape, NEG_BIG, F32)
    l_scr[...] = jnp.zeros(l_scr.shape, F32)
    acc_scr[...] = jnp.zeros(acc_scr.shape, F32)
    hsl = [slice(h * AT_DH, (h + 1) * AT_DH) for h in range(AT_HEADS)]
    ones = jnp.ones((SUBLANES, ck), BF16)

    def logits(kv, mk, h):
        k0 = pl.multiple_of(kv * ck, ck)
        return (lax.dot_general(k_ref[0, pl.ds(k0, ck), hsl[h]], q_ref[0, :, hsl[h]], NT,
                                preferred_element_type=F32) + pltpu.bitcast(key_scr[mk], F32))

    def put(buf, h, val):
        ref, slot = buf
        ref[h] = val
        cm_scr[slot, h] = jnp.broadcast_to(jnp.max(val, axis=0, keepdims=True), (SUBLANES, tq))

    buf_a, buf_b = (sa, 0), (sb, 1)

    def softmax_pv(kv, h, buf, fb):
        src, slot = buf
        m = m_scr[h][0:1, :]
        mn = jnp.maximum(m, cm_scr[slot, h][0:1, :] + fb)
        shift = mn - fb
        for j in range(ck // sub):
            pb_scr[h, j * sub:(j + 1) * sub, :] = jnp.exp2(src[h, j * sub:(j + 1) * sub, :] - shift).astype(BF16)
        a = jnp.exp2(m - mn)
        pb = pb_scr[h]
        l_scr[h] = a * l_scr[h] + _bdot(ones, pb)
        acc_scr[h] = a * acc_scr[h] + _bdot(vt_ref[0, h, kv], pb)
        m_scr[h] = jnp.broadcast_to(mn, (SUBLANES, tq))

    def near_body(c, carry):
        k0 = c * ck
        for h in range(AT_HEADS):
            rows = []
            for j in range(ck // sub):
                cols = []
                for qh in range(tq // sub):
                    delta = (q0 + qh * sub) - (k0 + j * sub)
                    cols.append(jnp.where(delta == 0, tile_ref[h, sub:2 * sub, :],
                                          jnp.where(delta == sub, tile_ref[h, 0:sub, :],
                                                    jnp.broadcast_to(far_ref[h], (sub, sub)))))
                rows.append(jnp.concatenate(cols, axis=1))
            put(buf_a, h, logits(c, c, h) + jnp.concatenate(rows, axis=0))
            softmax_pv(c, h, buf_a, 0.0)
        return carry

    nfar = jnp.maximum(nck - 2, 0)
    lax.fori_loop(nfar, nck, near_body, 0)

    def ids(c):
        ok = c < nfar
        return jnp.where(ok, c, 0), jnp.where(ok, c, nmax)

    @pl.when(nfar > 0)
    def _():
        for h in range(AT_HEADS):
            put(buf_a, h, logits(0, 0, h))

    def pair_body(p, carry):
        kv0 = 2 * p
        kv1, mk1 = ids(2 * p + 1)
        kv2, mk2 = ids(2 * p + 2)
        for h in range(AT_HEADS):
            put(buf_b, h, logits(kv1, mk1, h))
        for h in range(AT_HEADS):
            softmax_pv(kv0, h, buf_a, far_ref[h][:, 0:1])
            put(buf_a, h, logits(kv2, mk2, h))
        for h in range(AT_HEADS):
            softmax_pv(kv1, h, buf_b, far_ref[h][:, 0:1])
        return carry

    lax.fori_loop(0, (nfar + 1) // 2, pair_body, 0)
    att = jnp.concatenate([(acc_scr[h] / l_scr[h][0:1, :]).T for h in range(AT_HEADS)], axis=1)
    y = _bdot(att.astype(BF16), wout_ref[...])
    o_ref[0] = _ln(alpha * x_ref[0] + y, g_ref[...], b_ref[...])


def dsa_prompt(x, qb, qi, wi, kibf, kbf, vbf, tile, far, w_out, g, b, alpha):
    n, l, d = x.shape
    n_sel = min(TOPK_MAX, l // 4)
    nck_max = (l + DSA_CK - 1) // DSA_CK
    idx_bits = max(1, int(nck_max * DSA_CK - 1).bit_length())
    nqb = l // DSA_TQ
    vt = vbf.reshape(n, nck_max, DSA_CK, AT_HEADS, AT_DH).transpose(0, 3, 1, 4, 2)
    tok = lambda w: pl.BlockSpec((1, DSA_TQ, w), lambda s, t: (s, t, 0))
    seq = lambda a: pl.BlockSpec((1,) + a.shape[1:], lambda s, t: (s,) + (0,) * (a.ndim - 1),
                                 pipeline_mode=pl.Buffered(1))
    return pl.pallas_call(
        functools.partial(_dsa_prompt_kernel, alpha=alpha, n_sel=n_sel, idx_bits=idx_bits),
        grid=(n, nqb),
        in_specs=[tok(d), tok(d), tok(qi.shape[2]), tok(LANES), seq(kibf), seq(kbf), seq(vt),
                  _const(tile.shape), _const(far.shape), _const(w_out.shape), _const((1, d)), _const((1, d))],
        out_specs=tok(d),
        out_shape=jax.ShapeDtypeStruct((n, l, d), F32),
        scratch_shapes=[pltpu.VMEM((nck_max + 1, DSA_CK, DSA_TQ), I32), pltpu.VMEM((SUBLANES, DSA_TQ), I32),
                        pltpu.VMEM((AT_HEADS, DSA_CK, DSA_TQ), F32), pltpu.VMEM((AT_HEADS, DSA_CK, DSA_TQ), F32),
                        pltpu.VMEM((AT_HEADS, DSA_CK, DSA_TQ), BF16), pltpu.VMEM((2, AT_HEADS, SUBLANES, DSA_TQ), F32),
                        pltpu.VMEM((AT_HEADS, SUBLANES, DSA_TQ), F32), pltpu.VMEM((AT_HEADS, SUBLANES, DSA_TQ), F32),
                        pltpu.VMEM((AT_HEADS, AT_DH, DSA_TQ), F32)],
        compiler_params=_params("parallel", "arbitrary"),
    )(x, qb, qi, wi, kibf, kbf, vt, tile, far, w_out, g, b)


def _dsa_sample_score_kernel(pt_ref, qi_ref, wi_ref, *refs):
    o_ref = refs[-1]
    for p, kidx_ref in enumerate(refs[:-1]):
        kp = kidx_ref[0, 0].astype(BF16)
        dots = _bdot(qi_ref[0], kp)
        o_ref[0, :, p * PAGE:(p + 1) * PAGE] = jnp.sum(wi_ref[0] * jnp.maximum(dots, 0.0), axis=0, keepdims=True)


def _dsa_sample_select_kernel(sc_ref, qi_ref, ki_ref, wi_ref, sel_ref, key_scr, cut_scr, *, n_sel, idx_bits):
    nb = sc_ref.shape[1]
    pw = PAGE
    npg = sc_ref.shape[0] // pw
    for c in range(npg):
        key_scr[c] = _order_key(sc_ref[c * pw:(c + 1) * pw, :])
    qi = qi_ref[...].astype(F32)
    ki = ki_ref[...].astype(F32)
    wi = wi_ref[...]
    s = jnp.zeros((1, nb), F32)
    for h in range(IDX_HEADS):
        dot = jnp.sum(qi[h * IDX_DIM:(h + 1) * IDX_DIM, :] * ki, axis=0, keepdims=True)
        s = s + wi[h:h + 1, :] * jnp.maximum(dot, 0.0)
    row = lax.broadcasted_iota(I32, (pw, nb), 0)
    key_scr[npg] = _order_key(jnp.where(row == 0, s, NEG_BIG))
    _topk_select(key_scr, cut_scr, npg + 1, n_sel, idx_bits, lambda c, selected: jnp.where(selected, 1, 0))
    for c in range(npg + 1):
        sel_ref[c * pw:(c + 1) * pw, :] = key_scr[c]


def _dsa_sample_attn_kernel(pt_ref, q_ref, sel_ref, b8_ref, sb_ref, kn_ref, vn_ref, *refs):
    npg = (len(refs) - 1) // 2
    k_refs, v_refs, o_ref = refs[:npg], refs[npg:2 * npg], refs[-1]
    rows = PAGE * AT_HEADS
    past = npg * PAGE
    q = q_ref[0]
    s = jnp.concatenate([lax.dot_general(q, kr[0, 0].astype(BF16), NT, preferred_element_type=F32)
                         for kr in k_refs], axis=1)
    ok = jnp.broadcast_to(sel_ref[0, :, 0:npg * rows], s.shape) > 0
    s = jnp.where(ok, s + b8_ref[:, 0:npg * rows], NEG_BIG)
    sn = jnp.sum(q.astype(F32) * kn_ref[0].astype(F32), axis=1, keepdims=True) + sb_ref[:, past:past + 1]
    okn = jnp.broadcast_to(sel_ref[0, :, npg * rows:npg * rows + 1], sn.shape) > 0
    sn = jnp.where(okn, sn, NEG_BIG)
    m = jnp.maximum(jnp.max(s, axis=1, keepdims=True), sn)
    p = jnp.exp2(s - m)
    pn = jnp.exp2(sn - m)
    l = jnp.sum(p, axis=1, keepdims=True) + pn
    acc = pn.astype(BF16).astype(F32) * vn_ref[0].astype(F32)
    for i, vr in enumerate(v_refs):
        acc = acc + _bdot(p[:, i * rows:(i + 1) * rows].astype(BF16), vr[0, 0].astype(BF16))
    o_ref[0] = acc / l


def dsa_sample(x, j, qb, qib, kiln, wi, kbf, vbf, cache_k, cache_v, cache_kidx, page_table, sbias,
               w_out, g, b, alpha):
    n, d = x.shape
    npg = page_table.shape[1]
    past = npg * PAGE
    n_sel = min(TOPK_MAX, (past + 1) // 4)
    idx_bits = int((npg + 1) * PAGE - 1).bit_length()
    pt = page_table.reshape(-1)
    npool = cache_k.shape[1]

    def paged(shape):
        return [pl.BlockSpec((1, 1) + shape, lambda i, t, p=p: (j, t[i * npg + p], 0, 0)) for p in range(npg)]

    scores = pl.pallas_call(
        _dsa_sample_score_kernel,
        grid_spec=pltpu.PrefetchScalarGridSpec(
            num_scalar_prefetch=1, grid=(n,),
            in_specs=[pl.BlockSpec((1, IDX_HEADS, IDX_DIM), lambda i, t: (i, 0, 0)),
                      pl.BlockSpec((1, IDX_HEADS, 1), lambda i, t: (i, 0, 0))] + paged((IDX_DIM, PAGE)),
            out_specs=pl.BlockSpec((1, 1, past), lambda i, t: (i, 0, 0))),
        out_shape=jax.ShapeDtypeStruct((n, 1, past), F32),
        compiler_params=_params("parallel"),
    )(pt, qib.reshape(n, IDX_HEADS, IDX_DIM), wi[:, 0:IDX_HEADS].reshape(n, IDX_HEADS, 1),
      *([jnp.swapaxes(cache_kidx, 2, 3)] * npg))

    sel = pl.pallas_call(
        functools.partial(_dsa_sample_select_kernel, n_sel=n_sel, idx_bits=idx_bits),
        out_shape=jax.ShapeDtypeStruct((past + PAGE, n), I32),
        scratch_shapes=[pltpu.VMEM((npg + 1, PAGE, n), I32), pltpu.VMEM((SUBLANES, n), I32)],
        compiler_params=pltpu.CompilerParams(vmem_limit_bytes=VMEM_LIMIT),
    )(scores.reshape(n, past).T, qib.T, kiln.astype(BF16).T, wi[:, 0:IDX_HEADS].T).T

    rows = PAGE * AT_HEADS
    sel8 = jnp.repeat(sel, AT_HEADS, axis=1).reshape(n, 1, (npg + 1) * rows)
    rep = jnp.repeat(sbias, AT_HEADS, axis=1)
    own_head = (jnp.arange(rep.shape[1]) % AT_HEADS)[None, :] == jnp.arange(AT_HEADS)[:, None]
    b8 = jnp.where(own_head, rep, NEG_BIG)
    ck4 = cache_k.reshape(cache_k.shape[0], npool, rows, AT_DH)
    cv4 = cache_v.reshape(cache_v.shape[0], npool, rows, AT_DH)
    heads = pl.BlockSpec((1, AT_HEADS, AT_DH), lambda i, t: (i, 0, 0))
    o = pl.pallas_call(
        _dsa_sample_attn_kernel,
        grid_spec=pltpu.PrefetchScalarGridSpec(
            num_scalar_prefetch=1, grid=(n,),
            in_specs=[heads, pl.BlockSpec((1, 1, sel8.shape[2]), lambda i, t: (i, 0, 0)),
                      pl.BlockSpec(b8.shape, lambda i, t: (0, 0), pipeline_mode=pl.Buffered(1)),
                      pl.BlockSpec(sbias.shape, lambda i, t: (0, 0), pipeline_mode=pl.Buffered(1)),
                      heads, heads] + paged((rows, AT_DH)) + paged((rows, AT_DH)),
            out_specs=heads),
        out_shape=jax.ShapeDtypeStruct((n, AT_HEADS, AT_DH), F32),
        compiler_params=_params("parallel"),
    )(pt, qb.reshape(n, AT_HEADS, AT_DH), sel8, b8, sbias, kbf.reshape(n, AT_HEADS, AT_DH),
      vbf.reshape(n, AT_HEADS, AT_DH), *([ck4] * npg), *([cv4] * npg))
    return mm_res_ln(o.reshape(n, d), w_out, x, g, b, alpha, n)


MLP_TM = 512


def kernel(x_prompt, x_sample, state_hgrn, cache_k, cache_v, cache_kidx, page_table, state_s5_re, state_s5_im, hg_w_in, hg_lb, hg_gnorm, hg_w_out, dsa_w_in, dsa_kln_g, dsa_kln_b, dsa_w_out, rel_bias, s5_a_re, s5_a_im, s5_log_dt, s5_b_re, s5_b_im, s5_c_re, s5_c_im, s5_d, s5_w_glu, s5_b_glu, ln_mix_g, ln_mix_b, mlp_w1, mlp_w2, ln_ffn_g, ln_ffn_b):
    n, l, d = x_prompt.shape
    nb = x_sample.shape[0]
    depth = ln_mix_g.shape[0]
    alpha = (2 * depth) ** 0.25
    past = page_table.shape[1] * PAGE
    row = lambda a: a.reshape(1, -1)
    xp = x_prompt
    xs = x_sample.reshape(nb, d)
    hg_p, hg_s = [], None
    w1_all, w2_all = mlp_w1.astype(BF16), mlp_w2.astype(BF16)
    k_p, v_p, ki_p, k_s, v_s, ki_s = [], [], [], [], [], []
    s5r_p, s5i_p, s5r_s, s5i_s = [], [], [], []
    for i in range(depth):
        j = i // N_MIXERS
        kind = i % N_MIXERS
        g, b = row(ln_mix_g[i]), row(ln_mix_b[i])
        if kind == 0:
            w_in, w_out, gn = hg_w_in[j].astype(BF16), hg_w_out[j].astype(BF16), row(hg_gnorm[j])
            xp, st_p = hgrn_prompt(xp, w_in, hg_lb, gn, w_out, g, b, alpha, i)
            xs, hg_s = hgrn_sample(xs, state_hgrn, j, hg_s, w_in, hg_lb, gn, w_out, g, b, alpha, i)
            hg_p.append(st_p)
        elif kind == 1:
            w_out = dsa_w_out[j].astype(BF16)
            tile, far, sbias = dsa_bias(rel_bias, past)
            qb, k, v, kbf, vbf, qib, kiln, kibf, wi = dsa_proj(xp.reshape(n * l, d), dsa_w_in[j],
                                                               dsa_kln_g[j], dsa_kln_b[j], MLP_TM)
            seq = lambda a: a.reshape(n, l, a.shape[1])
            xp = dsa_prompt(xp, seq(qb), seq(qib), seq(wi), seq(kibf), seq(kbf), seq(vbf), tile, far,
                            w_out, g, b, alpha)
            k_p.append(k.reshape(n, l, AT_HEADS, AT_DH))
            v_p.append(v.reshape(n, l, AT_HEADS, AT_DH))
            ki_p.append(kiln.reshape(n, l, IDX_DIM))
            qb, k, v, kbf, vbf, qib, kiln, kibf, wi = dsa_proj(xs, dsa_w_in[j], dsa_kln_g[j], dsa_kln_b[j], nb)
            xs = dsa_sample(xs, j, qb, qib, kiln, wi, kbf, vbf, cache_k, cache_v, cache_kidx, page_table,
                            sbias, w_out, g, b, alpha)
            k_s.append(k.reshape(nb, 1, AT_HEADS, AT_DH))
            v_s.append(v.reshape(nb, 1, AT_HEADS, AT_DH))
            ki_s.append(kiln.reshape(nb, 1, IDX_DIM))
        else:
            prm = s5_params(s5_a_re[j], s5_a_im[j], s5_log_dt[j], s5_b_re[j], s5_b_im[j], s5_c_re[j],
                            s5_c_im[j], s5_d[j], s5_w_glu[j], s5_b_glu[j])
            sg, sp = s5_a_re.shape[1], s5_a_re.shape[2]
            xp, hr, hi = s5_prompt(xp, prm, g, b, alpha)
            xs, hr2, hi2 = s5_sample(xs, state_s5_re[j].reshape(nb, sg * sp), state_s5_im[j].reshape(nb, sg * sp),
                                     prm, g, b, alpha)
            s5r_p.append(hr.reshape(n, sg, sp))
            s5i_p.append(hi.reshape(n, sg, sp))
            s5r_s.append(hr2.reshape(nb, sg, sp))
            s5i_s.append(hi2.reshape(nb, sg, sp))
        fg, fb = row(ln_ffn_g[i]), row(ln_ffn_b[i])
        xp = mlp_res_ln(xp.reshape(n * l, d), w1_all, w2_all, i, fg, fb, alpha, MLP_TM).reshape(n, l, d)
        xs = mlp_res_ln(xs, w1_all, w2_all, i, fg, fb, alpha, nb)
    return (xp, xs.reshape(nb, 1, d), jnp.stack(hg_p), hg_s, jnp.stack(k_p), jnp.stack(v_p),
            jnp.stack(ki_p), jnp.stack(k_s), jnp.stack(v_s), jnp.stack(ki_s), jnp.stack(s5r_p),
            jnp.stack(s5i_p), jnp.stack(s5r_s), jnp.stack(s5i_s))
```

```python
import functools
import math

import numpy as np
import jax
import jax.numpy as jnp
from jax import lax
from jax.experimental import pallas as pl
from jax.experimental.pallas import tpu as pltpu

F32 = jnp.float32
BF16 = jnp.bfloat16
I32 = jnp.int32

LN_EPS = 1e-5
RMS_EPS = 1e-6
NEG_BIG = -1e30
N_MIXERS = 3
HG_HEADS = 8
HG_DK = 128
AT_HEADS = 8
AT_DH = 128
IDX_HEADS = 8
IDX_DIM = 64
TOPK_MAX = 256
REL_BUCKETS = 32
REL_MAX_DIST = 128
S5_GROUP = 16
S5_STATE = 64
PAGE = 128

LANES = 128
SUBLANES = 8
VMEM_LIMIT = 56 * 1024 * 1024

NT = (((1,), (1,)), ((), ()))
TN = (((0,), (0,)), ((), ()))


def _params(*sem):
    return pltpu.CompilerParams(dimension_semantics=sem, vmem_limit_bytes=VMEM_LIMIT)


def _const(shape):
    nd = len(shape)
    return pl.BlockSpec(shape, lambda *_: (0,) * nd, pipeline_mode=pl.Buffered(1))


def _ln(z, g, b):
    mu = jnp.mean(z, axis=-1, keepdims=True)
    zc = z - mu
    var = jnp.mean(zc * zc, axis=-1, keepdims=True)
    return zc * lax.rsqrt(var + LN_EPS) * g + b


def _sigmoid(z):
    return 1.0 / (1.0 + jnp.exp(-z))


def _bdot(a, b):
    return jnp.dot(a, b, preferred_element_type=F32)


def _mlp_kernel(x_ref, w1_ref, w2_ref, g_ref, b_ref, o_ref, *, alpha, hc):
    x = x_ref[...]
    xb = x.astype(BF16)
    acc = jnp.zeros(x.shape, F32)
    for c in range(w1_ref.shape[2] // hc):
        h = _bdot(xb, w1_ref[0, :, c * hc:(c + 1) * hc])
        h = jnp.maximum(h, 0.0)
        acc = acc + _bdot((h * h).astype(BF16), w2_ref[0, c * hc:(c + 1) * hc, :])
    o_ref[...] = _ln(alpha * x + acc, g_ref[...], b_ref[...])


def mlp_res_ln(x, w1, w2, layer, g, b, alpha, tm):
    t, d = x.shape
    f = w1.shape[2]
    pick = lambda r, c: pl.BlockSpec((1, r, c), lambda i: (layer, 0, 0), pipeline_mode=pl.Buffered(1))
    return pl.pallas_call(
        functools.partial(_mlp_kernel, alpha=alpha, hc=1024),
        grid=(t // tm,),
        in_specs=[pl.BlockSpec((tm, d), lambda i: (i, 0)), pick(d, f), pick(f, d),
                  _const((1, d)), _const((1, d))],
        out_specs=pl.BlockSpec((tm, d), lambda i: (i, 0)),
        out_shape=jax.ShapeDtypeStruct((t, d), F32),
        compiler_params=_params("parallel"),
    )(x, w1, w2, g, b)


def _mm_kernel(x_ref, w_ref, o_ref):
    o_ref[...] = _bdot(x_ref[...].astype(BF16), w_ref[...])


def mm(x, w, tm):
    t, d = x.shape
    n = w.shape[1]
    return pl.pallas_call(
        _mm_kernel,
        grid=(t // tm,),
        in_specs=[pl.BlockSpec((tm, d), lambda i: (i, 0)), _const((d, n))],
        out_specs=pl.BlockSpec((tm, n), lambda i: (i, 0)),
        out_shape=jax.ShapeDtypeStruct((t, n), F32),
        compiler_params=_params("parallel"),
    )(x, w)


def _mm_res_ln_kernel(a_ref, w_ref, x_ref, g_ref, b_ref, o_ref, *, alpha):
    y = _bdot(a_ref[...].astype(BF16), w_ref[...])
    o_ref[...] = _ln(alpha * x_ref[...] + y, g_ref[...], b_ref[...])


def mm_res_ln(a, w, x, g, b, alpha, tm):
    t, k = a.shape
    d = w.shape[1]
    return pl.pallas_call(
        functools.partial(_mm_res_ln_kernel, alpha=alpha),
        grid=(t // tm,),
        in_specs=[pl.BlockSpec((tm, k), lambda i: (i, 0)), _const((k, d)),
                  pl.BlockSpec((tm, d), lambda i: (i, 0)), _const((1, d)), _const((1, d))],
        out_specs=pl.BlockSpec((tm, d), lambda i: (i, 0)),
        out_shape=jax.ShapeDtypeStruct((t, d), F32),
        compiler_params=_params("parallel"),
    )(a, w, x, g, b)


HG_C = 128
HG_LEVELS = 7


def _hgrn_constants():
    c = HG_C
    t = np.arange(c)[:, None]
    u = np.arange(c)[None, :]
    m = (u <= t).astype(np.float32)
    x = t ^ u
    lv = np.where(u < t, np.floor(np.log2(np.maximum(x, 1))).astype(np.int32),
                  np.where(u == t, HG_LEVELS, -1)).astype(np.int32)
    return np.concatenate([m, m, m], axis=1), lv


def _lower_bound(lb_ref, layer):
    rows = [lb_ref[i] for i in range(lb_ref.shape[0])]
    mx = rows[0]
    for r in rows[1:]:
        mx = jnp.maximum(mx, r)
    ex = [jnp.exp(r - mx) for r in rows]
    tot = ex[0]
    for e in ex[1:]:
        tot = tot + e
    num = jnp.zeros_like(tot)
    for i in range(1, layer + 1):
        num = num + ex[i]
    return num / tot


def _split3(x):
    hi = x.astype(BF16)
    r1 = x - hi.astype(F32)
    mid = r1.astype(BF16)
    lo = (r1 - mid.astype(F32)).astype(BF16)
    return hi, mid, lo


def _hgrn_prompt_kernel(x_ref, win_ref, lb_ref, gn_ref, wout_ref, g_ref, b_ref, mst_ref, lv_ref,
                        o_ref, st_ref, og_scr, *, alpha, layer):
    c = HG_C
    dk = HG_DK

    @pl.when(pl.program_id(1) == 0)
    def _():
        st_ref[...] = jnp.zeros(st_ref.shape, F32)

    x = x_ref[0]
    d = x.shape[1]
    proj = _bdot(x.astype(BF16), win_ref[...])
    qp = proj[:, 0:d]
    fz = proj[:, d:2 * d]
    v = proj[:, 2 * d:3 * d]
    gp = proj[:, 3 * d:4 * d]
    lb = _lower_bound(lb_ref, layer)
    q = qp * _sigmoid(qp)
    sg = _sigmoid(fz)
    f = lb + (1.0 - lb) * sg
    k = (1.0 - lb) * (1.0 - sg)
    logf = jnp.log(f)
    gate = gp * _sigmoid(gp)
    hi, mid, lo = _split3(logf)
    bsum = _bdot(mst_ref[...], jnp.concatenate([hi, mid, lo], axis=0))

    lv = lv_ref[...]
    row = lax.broadcasted_iota(I32, (c, dk), 0)
    second = [((row >> l) & 1) == 1 for l in range(HG_LEVELS)]
    sub = lax.broadcasted_iota(I32, (c // SUBLANES, SUBLANES, dk), 1)

    def level_exponent(l, bh, lfh):
        half = 1 << l
        blk = 2 * half
        if l == 0:
            return jnp.where(second[0], lfh, 0.0)
        if half >= SUBLANES:
            parts = []
            for i in range(c // blk):
                r = i * blk + half - 1
                parts.append(bh[i * blk:(i + 1) * blk] - jnp.broadcast_to(bh[r:r + 1, :], (blk, dk)))
            dlt = jnp.concatenate(parts, axis=0)
        else:
            b3 = bh.reshape(c // SUBLANES, SUBLANES, dk)
            ref_row = jnp.broadcast_to(b3[:, half - 1:half, :], b3.shape)
            for i in range(1, SUBLANES // blk):
                nxt = jnp.broadcast_to(b3[:, i * blk + half - 1:i * blk + half, :], b3.shape)
                ref_row = jnp.where(sub >= i * blk, nxt, ref_row)
            dlt = (b3 - ref_row).reshape(c, dk)
        return jnp.where(second[l], dlt, -dlt)

    for h in range(HG_HEADS):
        hs = slice(h * dk, (h + 1) * dk)
        qh = q[:, hs]
        kh = k[:, hs]
        vh = v[:, hs].astype(BF16)
        bh = bsum[:, hs]
        lfh = logf[:, hs]
        st = st_ref[0, h]
        eb = jnp.exp(bh)
        o = lax.dot_general((qh * eb).astype(BF16), st.astype(BF16), NT, preferred_element_type=F32)
        att = jnp.zeros((c, c), F32)
        for l in range(HG_LEVELS):
            e = jnp.exp(level_exponent(l, bh, lfh))
            w = (jnp.where(second[l], qh, kh) * e).astype(BF16)
            a = lax.dot_general(w, w, NT, preferred_element_type=F32)
            att = jnp.where(lv == l, a, att)
        a0 = lax.dot_general(qh.astype(BF16), kh.astype(BF16), NT, preferred_element_type=F32)
        att = jnp.where(lv == HG_LEVELS, a0, att)
        o = o + _bdot(att.astype(BF16), vh)
        kd = (kh * jnp.exp(jnp.broadcast_to(bh[c - 1:c, :], (c, dk)) - bh)).astype(BF16)
        st_ref[0, h] = eb[c - 1:c, :] * st + lax.dot_general(vh, kd, TN, preferred_element_type=F32)
        ms = jnp.mean(o * o, axis=-1, keepdims=True)
        og_scr[:, hs] = o * lax.rsqrt(ms + RMS_EPS) * gn_ref[:, hs] * gate[:, hs]
    y = _bdot(og_scr[...].astype(BF16), wout_ref[...])
    o_ref[0] = _ln(alpha * x + y, g_ref[...], b_ref[...])


def hgrn_prompt(x, w_in, hg_lb, gnorm, w_out, g, b, alpha, layer):
    n, l, d = x.shape
    mst, lv = _hgrn_constants()
    nl = hg_lb.shape[0]
    out, st = pl.pallas_call(
        functools.partial(_hgrn_prompt_kernel, alpha=alpha, layer=layer),
        grid=(n, l // HG_C),
        in_specs=[pl.BlockSpec((1, HG_C, d), lambda s, c: (s, c, 0)),
                  _const((d, 4 * d)), _const((nl, 1, d)), _const((1, d)), _const((d, d)),
                  _const((1, d)), _const((1, d)), _const(mst.shape), _const(lv.shape)],
        out_specs=[pl.BlockSpec((1, HG_C, d), lambda s, c: (s, c, 0)),
                   pl.BlockSpec((1, HG_HEADS, HG_DK, HG_DK), lambda s, c: (s, 0, 0, 0))],
        out_shape=[jax.ShapeDtypeStruct((n, l, d), F32),
                   jax.ShapeDtypeStruct((n, HG_HEADS, HG_DK, HG_DK), F32)],
        scratch_shapes=[pltpu.VMEM((HG_C, d), F32)],
        compiler_params=_params("parallel", "arbitrary"),
    )(x, w_in, hg_lb.reshape(nl, 1, d), gnorm, w_out, g, b,
      jnp.asarray(mst, BF16), jnp.asarray(lv))
    return out, jnp.swapaxes(st, 2, 3)


HG_SG = 4


def _hgrn_sample_kernel(fzc_ref, lbt_ref, q_ref, fz_ref, v_ref, gp_ref, lb_ref, gn_ref, s0_ref, *rest, layer):
    o_ref, s1_ref = rest[-2], rest[-1]
    dk = HG_DK
    lbt = _lower_bound(lbt_ref, layer)
    lbr = _lower_bound(lb_ref, layer)
    rid = lax.broadcasted_iota(I32, (dk, dk), 0)
    for i in range(HG_SG):
        sgc = _sigmoid(fzc_ref[i])
        fcol = lbt + (1.0 - lbt) * sgc
        kcol = ((1.0 - lbt) * (1.0 - sgc)).astype(BF16)
        qr = q_ref[i]
        q = qr * _sigmoid(qr)
        sgr = _sigmoid(fz_ref[i])
        f = lbr + (1.0 - lbr) * sgr
        k = (1.0 - lbr) * (1.0 - sgr)
        v = v_ref[i]
        gp = gp_ref[i]
        gate = gp * _sigmoid(gp)
        for h in range(HG_HEADS):
            hs = slice(h * dk, (h + 1) * dk)
            s0 = s0_ref[0, i, h]
            vh = v[:, hs]
            vrow = jnp.where(rid == h, jnp.broadcast_to(vh, (dk, dk)), 0.0).astype(BF16)
            s1_ref[0, i, h] = fcol[:, h:h + 1] * s0 + _bdot(kcol, vrow)
            qf = jnp.broadcast_to((q[:, hs] * f[:, hs]).astype(BF16), (2 * SUBLANES, dk))
            qk = jnp.sum(q[:, hs] * k[:, hs], axis=1, keepdims=True)
            o = _bdot(qf, s0.astype(BF16))[0:1, :] + qk * vh
            ms = jnp.mean(o * o, axis=-1, keepdims=True)
            o_ref[i, :, hs] = o * lax.rsqrt(ms + RMS_EPS) * gn_ref[:, hs] * gate[:, hs]


def hgrn_sample(x, states, j, new_states, w_in, hg_lb, gnorm, w_out, g, b, alpha, layer):
    n, d = x.shape
    nl = hg_lb.shape[0]
    proj = mm(x, w_in, n)
    to_cols = lambda a, lead: jnp.pad(a.reshape(lead, HG_HEADS, HG_DK).transpose(0, 2, 1),
                                      ((0, 0), (0, 0), (0, LANES - HG_HEADS)))
    row3 = lambda a: a.reshape(n, 1, d)
    cols = pl.BlockSpec((HG_SG, HG_DK, LANES), lambda i: (i, 0, 0))
    rows = pl.BlockSpec((HG_SG, 1, d), lambda i: (i, 0, 0))
    st = pl.BlockSpec((1, HG_SG, HG_HEADS, HG_DK, HG_DK), lambda i: (j, i, 0, 0, 0))
    carried = [] if new_states is None else [new_states]
    o, s1 = pl.pallas_call(
        functools.partial(_hgrn_sample_kernel, layer=layer),
        grid=(n // HG_SG,),
        in_specs=[cols, _const((nl, HG_DK, LANES)), rows, rows, rows, rows, _const((nl, 1, d)), _const((1, d)), st]
                 + [pl.BlockSpec(memory_space=pl.ANY)] * len(carried),
        out_specs=[rows, st],
        out_shape=[jax.ShapeDtypeStruct((n, 1, d), F32), jax.ShapeDtypeStruct(states.shape, F32)],
        input_output_aliases={9: 1} if carried else {},
        compiler_params=_params("parallel"),
    )(to_cols(proj[:, d:2 * d], n), to_cols(hg_lb, nl), row3(proj[:, 0:d]), row3(proj[:, d:2 * d]),
      row3(proj[:, 2 * d:3 * d]), row3(proj[:, 3 * d:4 * d]), hg_lb.reshape(nl, 1, d), gnorm, states, *carried)
    return mm_res_ln(o.reshape(n, d), w_out, x, g, b, alpha, n), s1


S5_TC = 128
S5_SLABS = 8
S5_FOLD = 4


def _cmul(ar, ai, br, bi):
    return ar * br - ai * bi, ar * bi + ai * br


def _s5_prep_kernel(are_ref, aim_ref, ldt_ref, are2_ref, aim2_ref, ldt2_ref, bre_ref, bim_ref,
                    pwre_ref, pwim_ref, ckre_ref, ckim_ref, bbre_ref, bbim_ref):
    dt = jnp.exp(ldt_ref[...])
    mag = jnp.exp(dt * are_ref[...])
    a_r = mag * jnp.cos(dt * aim_ref[...])
    a_i = mag * jnp.sin(dt * aim_ref[...])
    pr, pi_ = [a_r], [a_i]
    for _ in range(SUBLANES - 1):
        nr, ni = _cmul(pr[-1], pi_[-1], a_r, a_i)
        pr.append(nr)
        pi_.append(ni)
    pwre_ref[...] = jnp.concatenate(pr, axis=0)
    pwim_ref[...] = jnp.concatenate(pi_, axis=0)
    zero = jnp.zeros_like(a_r)
    ckre_ref[...] = jnp.concatenate([zero] * S5_FOLD + pr[0:SUBLANES - S5_FOLD], axis=0)
    ckim_ref[...] = jnp.concatenate([zero] * S5_FOLD + pi_[0:SUBLANES - S5_FOLD], axis=0)
    ar, ai = are2_ref[...], aim2_ref[...]
    dt2 = jnp.exp(ldt2_ref[...])
    mag2 = jnp.exp(dt2 * ar)
    a2r = mag2 * jnp.cos(dt2 * ai)
    a2i = mag2 * jnp.sin(dt2 * ai)
    nr = a2r - 1.0
    ni = a2i
    den = ar * ar + ai * ai
    z_re = (nr * ar + ni * ai) / den
    z_im = (ni * ar - nr * ai) / den
    br, bi = bre_ref[...], bim_ref[...]
    for j in range(S5_FOLD):
        bbre_ref[j] = z_re[:, None, :] * br - z_im[:, None, :] * bi
        bbim_ref[j] = z_re[:, None, :] * bi + z_im[:, None, :] * br
        z_re, z_im = _cmul(z_re, z_im, a2r, a2i)


def s5_prep(a_re, a_im, log_dt, b_re, b_im):
    g, p = a_re.shape
    k = b_re.shape[2]
    n = g * p
    ldt2 = jnp.broadcast_to(log_dt[:, None], (g, p))
    flat = lambda a: a.reshape(1, n)
    return pl.pallas_call(
        _s5_prep_kernel,
        out_shape=[jax.ShapeDtypeStruct((SUBLANES, n), F32), jax.ShapeDtypeStruct((SUBLANES, n), F32),
                   jax.ShapeDtypeStruct((SUBLANES, n), F32), jax.ShapeDtypeStruct((SUBLANES, n), F32),
                   jax.ShapeDtypeStruct((S5_FOLD, g, k, p), F32), jax.ShapeDtypeStruct((S5_FOLD, g, k, p), F32)],
    )(flat(a_re), flat(a_im), flat(ldt2), a_re, a_im, ldt2,
      b_re.transpose(0, 2, 1), b_im.transpose(0, 2, 1))


def _block_diag_slabs(w_gkp, rows_are_inputs):
    g, k, p = w_gkp.shape
    per = g // S5_SLABS
    w4 = w_gkp.reshape(S5_SLABS, per, k, p)
    eye = jnp.eye(per, dtype=w_gkp.dtype)
    if rows_are_inputs:
        return jnp.einsum('iakp,ab->iakbp', w4, eye).reshape(S5_SLABS, per * k, per * p)
    return jnp.einsum('iakp,ab->ibpak', w4, eye).reshape(S5_SLABS, per * p, per * k)


def _s5_tail(x, y, d_ref, wglu_ref, bglu_ref, g_ref, b_ref, alpha):
    d = x.shape[1]
    gl = jax.nn.gelu(y + d_ref[...] * x)
    u = _bdot(gl.astype(BF16), wglu_ref[...]) + bglu_ref[...]
    out = u[:, 0:d] * _sigmoid(u[:, d:2 * d])
    return _ln(alpha * x + out, g_ref[...], b_ref[...])


def _s5_prompt_kernel(x_ref, wbre_ref, wbim_ref, wcre_ref, wcim_ref, pwre_ref, pwim_ref, ckre_ref, ckim_ref,
                      d_ref, wglu_ref, bglu_ref, g_ref, b_ref,
                      o_ref, hre_ref, him_ref, cre_scr, cim_scr, y_scr, *, alpha):
    tc = x_ref.shape[1]
    ng = tc // SUBLANES
    ci = pl.program_id(1)

    @pl.when(ci == 0)
    def _():
        cre_scr[...] = jnp.zeros(cre_scr.shape, F32)
        cim_scr[...] = jnp.zeros(cim_scr.shape, F32)

    x = x_ref[0]
    rin = lax.broadcasted_iota(I32, x.shape, 0) & (S5_FOLD - 1)
    lags = [x.astype(BF16)] + [jnp.where(rin >= j, pltpu.roll(x, j, axis=0), 0.0).astype(BF16)
                               for j in range(1, S5_FOLD)]
    sw = wbre_ref.shape[2]
    for i in range(S5_SLABS):
        cs = slice(i * sw, (i + 1) * sw)
        xi = jnp.concatenate([xl[:, i * LANES:(i + 1) * LANES] for xl in lags], axis=1)
        br = _bdot(xi, wbre_ref[i]).reshape(ng, SUBLANES, sw)
        bi = _bdot(xi, wbim_ref[i]).reshape(ng, SUBLANES, sw)
        sr = jnp.broadcast_to(br[:, S5_FOLD - 1:S5_FOLD, :], br.shape)
        si = jnp.broadcast_to(bi[:, S5_FOLD - 1:S5_FOLD, :], bi.shape)
        tr, ti = _cmul(ckre_ref[:, cs], ckim_ref[:, cs], sr, si)
        br, bi = br + tr, bi + ti
        pr, pi_ = pwre_ref[:, cs], pwim_ref[:, cs]
        hpr, hpi = cre_scr[:, cs], cim_scr[:, cs]
        hr_all, hi_all = [], []
        for m in range(ng):
            tr, ti = _cmul(pr, pi_, hpr, hpi)
            hr, hi = br[m] + tr, bi[m] + ti
            hr_all.append(hr)
            hi_all.append(hi)
            hpr = jnp.broadcast_to(hr[SUBLANES - 1:SUBLANES, :], hr.shape)
            hpi = jnp.broadcast_to(hi[SUBLANES - 1:SUBLANES, :], hi.shape)
        cre_scr[:, cs] = hpr
        cim_scr[:, cs] = hpi
        hr = jnp.concatenate(hr_all, axis=0).astype(BF16)
        hi = jnp.concatenate(hi_all, axis=0).astype(BF16)
        y_scr[:, i * LANES:(i + 1) * LANES] = _bdot(hr, wcre_ref[i]) - _bdot(hi, wcim_ref[i])
    o_ref[0] = _s5_tail(x, y_scr[...], d_ref, wglu_ref, bglu_ref, g_ref, b_ref, alpha)

    @pl.when(ci == pl.num_programs(1) - 1)
    def _():
        hre_ref[0] = cre_scr[0:1, :]
        him_ref[0] = cim_scr[0:1, :]


def _s5_weight_specs(wb, wc, pw, ck, d, wglu):
    return [_const(wb.shape), _const(wb.shape), _const(wc.shape), _const(wc.shape),
            _const(pw.shape), _const(pw.shape), _const(ck.shape), _const(ck.shape),
            _const((1, d)), _const(wglu.shape), _const((1, wglu.shape[1])), _const((1, d)), _const((1, d))]


def s5_prompt(x, prm, g, b, alpha):
    n, l, d = x.shape
    wb, wc, pw, ck, wglu = prm[0], prm[2], prm[4], prm[6], prm[9]
    ns = pw.shape[1]
    out, hre, him = pl.pallas_call(
        functools.partial(_s5_prompt_kernel, alpha=alpha),
        grid=(n, l // S5_TC),
        in_specs=[pl.BlockSpec((1, S5_TC, d), lambda s, c: (s, c, 0))] + _s5_weight_specs(wb, wc, pw, ck, d, wglu),
        out_specs=[pl.BlockSpec((1, S5_TC, d), lambda s, c: (s, c, 0)),
                   pl.BlockSpec((1, 1, ns), lambda s, c: (s, 0, 0)),
                   pl.BlockSpec((1, 1, ns), lambda s, c: (s, 0, 0))],
        out_shape=[jax.ShapeDtypeStruct((n, l, d), F32),
                   jax.ShapeDtypeStruct((n, 1, ns), F32), jax.ShapeDtypeStruct((n, 1, ns), F32)],
        scratch_shapes=[pltpu.VMEM((SUBLANES, ns), F32), pltpu.VMEM((SUBLANES, ns), F32),
                        pltpu.VMEM((S5_TC, d), F32)],
        compiler_params=_params("parallel", "arbitrary"),
    )(x, *prm, g, b)
    return out, hre, him


def _s5_sample_kernel(x_ref, h0re_ref, h0im_ref, wbre_ref, wbim_ref, wcre_ref, wcim_ref, pwre_ref, pwim_ref,
                      ckre_ref, ckim_ref, d_ref, wglu_ref, bglu_ref, g_ref, b_ref,
                      o_ref, hre_ref, him_ref, y_scr, *, alpha):
    x = x_ref[...]
    xb = x.astype(BF16)
    sw = wbre_ref.shape[2]
    for i in range(S5_SLABS):
        cs = slice(i * sw, (i + 1) * sw)
        xi = xb[:, i * LANES:(i + 1) * LANES]
        tr, ti = _cmul(pwre_ref[0:1, cs], pwim_ref[0:1, cs], h0re_ref[:, cs], h0im_ref[:, cs])
        hr = _bdot(xi, wbre_ref[i, 0:LANES, :]) + tr
        hi = _bdot(xi, wbim_ref[i, 0:LANES, :]) + ti
        hre_ref[:, cs] = hr
        him_ref[:, cs] = hi
        y_scr[:, i * LANES:(i + 1) * LANES] = (_bdot(hr.astype(BF16), wcre_ref[i])
                                               - _bdot(hi.astype(BF16), wcim_ref[i]))
    o_ref[...] = _s5_tail(x, y_scr[...], d_ref, wglu_ref, bglu_ref, g_ref, b_ref, alpha)


def s5_sample(x, h0re, h0im, prm, g, b, alpha):
    n, d = x.shape
    ns = h0re.shape[1]
    return pl.pallas_call(
        functools.partial(_s5_sample_kernel, alpha=alpha),
        out_shape=[jax.ShapeDtypeStruct((n, d), F32),
                   jax.ShapeDtypeStruct((n, ns), F32), jax.ShapeDtypeStruct((n, ns), F32)],
        scratch_shapes=[pltpu.VMEM((n, d), F32)],
        compiler_params=pltpu.CompilerParams(vmem_limit_bytes=VMEM_LIMIT),
    )(x, h0re, h0im, *prm, g, b)


def s5_params(a_re, a_im, log_dt, b_re, b_im, c_re, c_im, d_skip, w_glu, b_glu):
    pwre, pwim, ckre, ckim, bbre, bbim = s5_prep(a_re, a_im, log_dt, b_re, b_im)
    fold = lambda bb: jnp.concatenate([_block_diag_slabs(bb[j], True) for j in range(S5_FOLD)], axis=1)
    return (fold(bbre).astype(BF16), fold(bbim).astype(BF16),
            _block_diag_slabs(c_re, False).astype(BF16), _block_diag_slabs(c_im, False).astype(BF16),
            pwre, pwim, ckre, ckim, d_skip.reshape(1, -1), w_glu.astype(BF16), b_glu.reshape(1, -1))


DSA_TQ = 256
DSA_SUB = 128
DSA_CK = 512
INT_MIN = -2 ** 31
INT_MAX = 2 ** 31 - 1
LOG2E = math.log2(math.e)


def _dsa_proj_kernel(x_ref, wq_ref, wk_ref, wv_ref, wqi_ref, wki_ref, wwi_ref, kg_ref, kb_ref,
                     qb_ref, k_ref, v_ref, kbf_ref, vbf_ref, qi_ref, kiln_ref, kibf_ref, wi_ref):
    xb = x_ref[...].astype(BF16)
    qb_ref[...] = (_bdot(xb, wq_ref[...]) * (AT_DH ** -0.5 * LOG2E)).astype(BF16)
    k = _bdot(xb, wk_ref[...])
    v = _bdot(xb, wv_ref[...])
    k_ref[...] = k
    v_ref[...] = v
    kbf_ref[...] = k.astype(BF16)
    vbf_ref[...] = v.astype(BF16)
    qi_ref[...] = (_bdot(xb, wqi_ref[...]) * (IDX_DIM ** -0.5)).astype(BF16)
    ki = _ln(_bdot(xb, wki_ref[...])[:, 0:IDX_DIM], kg_ref[...], kb_ref[...])
    kiln_ref[...] = ki
    kibf_ref[...] = ki.astype(BF16)
    wi_ref[...] = _bdot(xb, wwi_ref[...]) * (IDX_HEADS ** -0.5)


def dsa_proj(x, w_in, kln_g, kln_b, tm):
    t, d = x.shape
    hd = AT_HEADS * AT_DH
    qd = IDX_HEADS * IDX_DIM
    wb = w_in.astype(BF16)
    pad = lambda w: jnp.pad(w, ((0, 0), (0, LANES - w.shape[1])))
    ws = [wb[:, 0:hd], wb[:, hd:2 * hd], wb[:, 2 * hd:3 * hd], wb[:, 3 * hd:3 * hd + qd],
          pad(wb[:, 3 * hd + qd:3 * hd + qd + IDX_DIM]), pad(wb[:, 3 * hd + qd + IDX_DIM:])]
    row = lambda n, dt: (pl.BlockSpec((tm, n), lambda i: (i, 0)), jax.ShapeDtypeStruct((t, n), dt))
    outs = [row(hd, BF16), row(hd, F32), row(hd, F32), row(hd, BF16), row(hd, BF16), row(qd, BF16),
            row(IDX_DIM, F32), row(IDX_DIM, BF16), row(LANES, F32)]
    return pl.pallas_call(
        _dsa_proj_kernel,
        grid=(t // tm,),
        in_specs=[pl.BlockSpec((tm, d), lambda i: (i, 0))] + [_const(w.shape) for w in ws]
                 + [_const((1, IDX_DIM)), _const((1, IDX_DIM))],
        out_specs=[o[0] for o in outs],
        out_shape=[o[1] for o in outs],
        compiler_params=_params("parallel"),
    )(x, *ws, kln_g.reshape(1, IDX_DIM), kln_b.reshape(1, IDX_DIM))


def _t5_bucket(rel):
    n = jnp.maximum(rel, 0)
    exact = REL_BUCKETS // 2
    nf = jnp.maximum(n, exact).astype(F32)
    large = exact + (jnp.log(nf / exact) / math.log(REL_MAX_DIST / exact) * (REL_BUCKETS - exact)).astype(I32)
    large = jnp.minimum(large, REL_BUCKETS - 1)
    return jnp.where(n < exact, n, large)


def _bias_lookup(bucket, rb_ref, h):
    out = jnp.zeros(bucket.shape, F32)
    for bk in range(REL_BUCKETS):
        out = jnp.where(bucket == bk, rb_ref[bk, h], out)
    return out


def _dsa_bias_kernel(rb_ref, tile_ref, far_ref, samp_ref, *, past):
    c = lax.broadcasted_iota(I32, tile_ref.shape[1:], 0)
    i = lax.broadcasted_iota(I32, tile_ref.shape[1:], 1)
    bt = _t5_bucket(i - c + DSA_SUB)
    bs = _t5_bucket(past - lax.broadcasted_iota(I32, (1, samp_ref.shape[1]), 1))
    for h in range(AT_HEADS):
        tile_ref[h] = _bias_lookup(bt, rb_ref, h) * LOG2E
        far_ref[h] = jnp.full(far_ref.shape[1:], rb_ref[REL_BUCKETS - 1, h] * LOG2E, F32)
        samp_ref[h:h + 1, :] = _bias_lookup(bs, rb_ref, h) * LOG2E


def dsa_bias(rel_bias, past):
    return pl.pallas_call(
        functools.partial(_dsa_bias_kernel, past=past),
        in_specs=[pl.BlockSpec(memory_space=pltpu.SMEM)],
        out_shape=[jax.ShapeDtypeStruct((AT_HEADS, 2 * DSA_SUB, DSA_SUB), F32),
                   jax.ShapeDtypeStruct((AT_HEADS, 1, LANES), F32),
                   jax.ShapeDtypeStruct((AT_HEADS, past + PAGE), F32)],
    )(rel_bias)


def _order_key(s):
    bits = pltpu.bitcast(s, I32)
    return bits ^ ((bits >> 31) & INT_MAX)


def _count(key_scr, nck, pred):
    cw, nq = key_scr.shape[1], key_scr.shape[2]

    def body(c, acc):
        kk = key_scr[c]
        parts = [jnp.where(pred(kk[g * SUBLANES:(g + 1) * SUBLANES], c, g * SUBLANES), 1.0, 0.0)
                 for g in range(cw // SUBLANES)]
        while len(parts) > 1:
            parts = [parts[i] + parts[i + 1] for i in range(0, len(parts), 2)]
        return acc + parts[0]

    acc = lax.fori_loop(0, nck, body, jnp.zeros((SUBLANES, nq), F32))
    return jnp.sum(acc, axis=0, keepdims=True)


def _topk_select(key_scr, cut_scr, nck, n_sel, idx_bits, emit):
    cw, nq = key_scr.shape[1], key_scr.shape[2]
    nf = float(n_sel)
    kidx = lax.broadcasted_iota(I32, (cw, nq), 0)
    sidx = lax.broadcasted_iota(I32, (SUBLANES, nq), 0)

    cnt = _count(key_scr, nck, lambda kk, c, r: kk >= 0)
    thr = jnp.where(cnt >= nf, 0, INT_MIN).astype(I32)

    def bit_body(i, thr):
        trial = thr | jnp.left_shift(jnp.int32(1), 30 - i)
        cnt = _count(key_scr, nck, lambda kk, c, r: kk >= trial)
        return jnp.where(cnt >= nf, trial, thr)

    thr = lax.fori_loop(0, 31, bit_body, thr)
    cnt_ge = _count(key_scr, nck, lambda kk, c, r: kk >= thr)
    need = nf - _count(key_scr, nck, lambda kk, c, r: kk > thr)

    cut_scr[...] = jnp.full(cut_scr.shape, INT_MAX, I32)

    @pl.when(jnp.max(cnt_ge) > nf)
    def _():
        def tie_body(i, rr):
            t = rr | jnp.left_shift(jnp.int32(1), idx_bits - 1 - i)
            below = _count(key_scr, nck,
                           lambda kk, c, r: jnp.where(kk == thr, c * cw + r + sidx, INT_MAX) < t)
            return jnp.where(below <= need - 1.0, t, rr)

        rr = lax.fori_loop(0, idx_bits, tie_body, jnp.zeros((1, nq), I32))
        cut_scr[...] = jnp.broadcast_to(jnp.where(cnt_ge > nf, rr, INT_MAX), cut_scr.shape)

    cut = cut_scr[0:1, :]

    def sel_body(c, carry):
        kk = key_scr[c]
        tie = jnp.where(c * cw + kidx <= cut, 1, 0)
        key_scr[c] = emit(c, jnp.where(kk > thr, 1, jnp.where(kk == thr, tie, 0)) > 0)
        return carry

    lax.fori_loop(0, nck, sel_body, 0)


def _dsa_prompt_kernel(x_ref, q_ref, qi_ref, wi_ref, ki_ref, k_ref, vt_ref, tile_ref, far_ref,
                       wout_ref, g_ref, b_ref, o_ref, key_scr, cut_scr, sa, sb, pb_scr, cm_scr, m_scr, l_scr, acc_scr,
                       *, alpha, n_sel, idx_bits):
    tq = DSA_TQ
    ck = DSA_CK
    sub = DSA_SUB
    qb = pl.program_id(1)
    q0 = qb * tq
    nck = (q0 + tq + ck - 1) // ck
    kloc = lax.broadcasted_iota(I32, (ck, tq), 0)
    qpos = q0 + lax.broadcasted_iota(I32, (ck, tq), 1)

    wit = wi_ref[0].T
    qis = [qi_ref[0, :, h * IDX_DIM:(h + 1) * IDX_DIM] for h in range(IDX_HEADS)]

    def score_body(c, carry):
        kic = ki_ref[0, pl.ds(pl.multiple_of(c * ck, ck), ck), :]
        s = None
        for h in range(IDX_HEADS):
            dots = lax.dot_general(kic, qis[h], NT, preferred_element_type=F32)
            t = wit[h:h + 1, :] * jnp.maximum(dots, 0.0)
            s = t if s is None else s + t
        s = jnp.where(c * ck + kloc <= qpos, s, NEG_BIG)
        key_scr[c] = _order_key(s)
        return carry

    lax.fori_loop(0, nck, score_body, 0)

    def mask_bits(c, selected):
        keep = jnp.where(c * ck + kloc <= qpos, 0.0, NEG_BIG)
        return pltpu.bitcast(jnp.where(selected, keep, NEG_BIG), I32)

    _topk_select(key_scr, cut_scr, nck, n_sel, idx_bits, mask_bits)

    nmax = key_scr.shape[0] - 1
    key_scr[nmax] = pltpu.bitcast(jnp.full((ck, tq), NEG_BIG, F32), I32)
    m_scr[...] = jnp.full(m_scr.shape, NEG_BIG, F32)
    l_scr[...] = jnp.zeros(l_scr.shape, F32)
    acc_scr[...] = jnp.zeros(acc_scr.shape, F32)
    hsl = [slice(h * AT_DH, (h + 1) * AT_DH) for h in range(AT_HEADS)]
    ones = jnp.ones((SUBLANES, ck), BF16)

    def logits(kv, mk, h):
        k0 = pl.multiple_of(kv * ck, ck)
        return (lax.dot_general(k_ref[0, pl.ds(k0, ck), hsl[h]], q_ref[0, :, hsl[h]], NT,
                                preferred_element_type=F32) + pltpu.bitcast(key_scr[mk], F32))

    def put(buf, h, val):
        ref, slot = buf
        ref[h] = val
        cm_scr[slot, h] = jnp.broadcast_to(jnp.max(val, axis=0, keepdims=True), (SUBLANES, tq))

    buf_a, buf_b = (sa, 0), (sb, 1)

    def softmax_pv(kv, h, buf, fb):
        src, slot = buf
        m = m_scr[h][0:1, :]
        mn = jnp.maximum(m, cm_scr[slot, h][0:1, :] + fb)
        shift = mn - fb
        for j in range(ck // sub):
            pb_scr[h, j * sub:(j + 1) * sub, :] = jnp.exp2(src[h, j * sub:(j + 1) * sub, :] - shift).astype(BF16)
        a = jnp.exp2(m - mn)
        pb = pb_scr[h]
        l_scr[h] = a * l_scr[h] + _bdot(ones, pb)
        acc_scr[h] = a * acc_scr[h] + _bdot(vt_ref[0, h, kv], pb)
        m_scr[h] = jnp.broadcast_to(mn, (SUBLANES, tq))

    def near_logits(kv, mk, h):
        k0 = kv * ck
        rows = []
        for j in range(ck // sub):
            cols = []
            for qh in range(tq // sub):
                delta = (q0 + qh * sub) - (k0 + j * sub)
                cols.append(jnp.where(delta == 0, tile_ref[h, sub:2 * sub, :],
                                      jnp.where(delta == sub, tile_ref[h, 0:sub, :],
                                                jnp.broadcast_to(far_ref[h], (sub, sub)))))
            rows.append(jnp.concatenate(cols, axis=1))
        return logits(kv, mk, h) + jnp.concatenate(rows, axis=0)

    c_last = nck - 1
    has_prev = nck >= 2
    kv_prev = jnp.where(has_prev, nck - 2, 0)
    mk_prev = jnp.where(has_prev, nck - 2, nmax)
    for h in range(AT_HEADS):
        put(buf_a, h, near_logits(c_last, c_last, h))
    for h in range(AT_HEADS):
        put(buf_b, h, near_logits(kv_prev, mk_prev, h))
        softmax_pv(c_last, h, buf_a, 0.0)
    for h in range(AT_HEADS):
        softmax_pv(kv_prev, h, buf_b, 0.0)

    nfar = jnp.maximum(nck - 2, 0)

    def ids(c):
        ok = c < nfar
        return jnp.where(ok, c, 0), jnp.where(ok, c, nmax)

    @pl.when(nfar > 0)
    def _():
        for h in range(AT_HEADS):
            put(buf_a, h, logits(0, 0, h))

    def pair_body(p, carry):
        kv0 = 2 * p
        kv1, mk1 = ids(2 * p + 1)
        kv2, mk2 = ids(2 * p + 2)
        for h in range(AT_HEADS):
            put(buf_b, h, logits(kv1, mk1, h))
        for h in range(AT_HEADS):
            softmax_pv(kv0, h, buf_a, far_ref[h][:, 0:1])
            put(buf_a, h, logits(kv2, mk2, h))
        for h in range(AT_HEADS):
            softmax_pv(kv1, h, buf_b, far_ref[h][:, 0:1])
        return carry

    lax.fori_loop(0, (nfar + 1) // 2, pair_body, 0)
    att = jnp.concatenate([(acc_scr[h] / l_scr[h][0:1, :]).T for h in range(AT_HEADS)], axis=1)
    y = _bdot(att.astype(BF16), wout_ref[...])
    o_ref[0] = _ln(alpha * x_ref[0] + y, g_ref[...], b_ref[...])


def dsa_prompt(x, qb, qi, wi, kibf, kbf, vbf, tile, far, w_out, g, b, alpha):
    n, l, d = x.shape
    n_sel = min(TOPK_MAX, l // 4)
    nck_max = (l + DSA_CK - 1) // DSA_CK
    idx_bits = max(1, int(nck_max * DSA_CK - 1).bit_length())
    nqb = l // DSA_TQ
    vt = vbf.reshape(n, nck_max, DSA_CK, AT_HEADS, AT_DH).transpose(0, 3, 1, 4, 2)
    tok = lambda w: pl.BlockSpec((1, DSA_TQ, w), lambda s, t: (s, t, 0))
    seq = lambda a: pl.BlockSpec((1,) + a.shape[1:], lambda s, t: (s,) + (0,) * (a.ndim - 1),
                                 pipeline_mode=pl.Buffered(1))
    return pl.pallas_call(
        functools.partial(_dsa_prompt_kernel, alpha=alpha, n_sel=n_sel, idx_bits=idx_bits),
        grid=(n, nqb),
        in_specs=[tok(d), tok(d), tok(qi.shape[2]), tok(LANES), seq(kibf), seq(kbf), seq(vt),
                  _const(tile.shape), _const(far.shape), _const(w_out.shape), _const((1, d)), _const((1, d))],
        out_specs=tok(d),
        out_shape=jax.ShapeDtypeStruct((n, l, d), F32),
        scratch_shapes=[pltpu.VMEM((nck_max + 1, DSA_CK, DSA_TQ), I32), pltpu.VMEM((SUBLANES, DSA_TQ), I32),
                        pltpu.VMEM((AT_HEADS, DSA_CK, DSA_TQ), F32), pltpu.VMEM((AT_HEADS, DSA_CK, DSA_TQ), F32),
                        pltpu.VMEM((AT_HEADS, DSA_CK, DSA_TQ), BF16), pltpu.VMEM((2, AT_HEADS, SUBLANES, DSA_TQ), F32),
                        pltpu.VMEM((AT_HEADS, SUBLANES, DSA_TQ), F32), pltpu.VMEM((AT_HEADS, SUBLANES, DSA_TQ), F32),
                        pltpu.VMEM((AT_HEADS, AT_DH, DSA_TQ), F32)],
        compiler_params=_params("parallel", "arbitrary"),
    )(x, qb, qi, wi, kibf, kbf, vt, tile, far, w_out, g, b)


def _dsa_sample_score_kernel(pt_ref, qi_ref, wi_ref, *refs):
    o_ref = refs[-1]
    for p, kidx_ref in enumerate(refs[:-1]):
        kp = kidx_ref[0, 0].astype(BF16)
        dots = _bdot(qi_ref[0], kp)
        o_ref[0, :, p * PAGE:(p + 1) * PAGE] = jnp.sum(wi_ref[0] * jnp.maximum(dots, 0.0), axis=0, keepdims=True)


def _dsa_sample_select_kernel(sc_ref, qi_ref, ki_ref, wi_ref, sel_ref, key_scr, cut_scr, *, n_sel, idx_bits):
    nb = sc_ref.shape[1]
    pw = PAGE
    npg = sc_ref.shape[0] // pw
    for c in range(npg):
        key_scr[c] = _order_key(sc_ref[c * pw:(c + 1) * pw, :])
    qi = qi_ref[...].astype(F32)
    ki = ki_ref[...].astype(F32)
    wi = wi_ref[...]
    s = jnp.zeros((1, nb), F32)
    for h in range(IDX_HEADS):
        dot = jnp.sum(qi[h * IDX_DIM:(h + 1) * IDX_DIM, :] * ki, axis=0, keepdims=True)
        s = s + wi[h:h + 1, :] * jnp.maximum(dot, 0.0)
    row = lax.broadcasted_iota(I32, (pw, nb), 0)
    key_scr[npg] = _order_key(jnp.where(row == 0, s, NEG_BIG))
    _topk_select(key_scr, cut_scr, npg + 1, n_sel, idx_bits, lambda c, selected: jnp.where(selected, 1, 0))
    for c in range(npg + 1):
        sel_ref[c * pw:(c + 1) * pw, :] = key_scr[c]


def _dsa_sample_attn_kernel(pt_ref, q_ref, sel_ref, b8_ref, sb_ref, kn_ref, vn_ref, *refs):
    npg = (len(refs) - 1) // 2
    k_refs, v_refs, o_ref = refs[:npg], refs[npg:2 * npg], refs[-1]
    rows = PAGE * AT_HEADS
    past = npg * PAGE
    q = q_ref[0]
    s = jnp.concatenate([lax.dot_general(q, kr[0, 0].astype(BF16), NT, preferred_element_type=F32)
                         for kr in k_refs], axis=1)
    ok = jnp.broadcast_to(sel_ref[0, :, 0:npg * rows], s.shape) > 0
    s = jnp.where(ok, s + b8_ref[:, 0:npg * rows], NEG_BIG)
    sn = jnp.sum(q.astype(F32) * kn_ref[0].astype(F32), axis=1, keepdims=True) + sb_ref[:, past:past + 1]
    okn = jnp.broadcast_to(sel_ref[0, :, npg * rows:npg * rows + 1], sn.shape) > 0
    sn = jnp.where(okn, sn, NEG_BIG)
    m = jnp.maximum(jnp.max(s, axis=1, keepdims=True), sn)
    p = jnp.exp2(s - m)
    pn = jnp.exp2(sn - m)
    l = jnp.sum(p, axis=1, keepdims=True) + pn
    acc = pn.astype(BF16).astype(F32) * vn_ref[0].astype(F32)
    for i, vr in enumerate(v_refs):
        acc = acc + _bdot(p[:, i * rows:(i + 1) * rows].astype(BF16), vr[0, 0].astype(BF16))
    o_ref[0] = acc / l


def dsa_sample(x, j, qb, qib, kiln, wi, kbf, vbf, cache_k, cache_v, cache_kidx, page_table, sbias,
               w_out, g, b, alpha):
    n, d = x.shape
    npg = page_table.shape[1]
    past = npg * PAGE
    n_sel = min(TOPK_MAX, (past + 1) // 4)
    idx_bits = int((npg + 1) * PAGE - 1).bit_length()
    pt = page_table.reshape(-1)
    npool = cache_k.shape[1]

    def paged(shape):
        return [pl.BlockSpec((1, 1) + shape, lambda i, t, p=p: (j, t[i * npg + p], 0, 0)) for p in range(npg)]

    scores = pl.pallas_call(
        _dsa_sample_score_kernel,
        grid_spec=pltpu.PrefetchScalarGridSpec(
            num_scalar_prefetch=1, grid=(n,),
            in_specs=[pl.BlockSpec((1, IDX_HEADS, IDX_DIM), lambda i, t: (i, 0, 0)),
                      pl.BlockSpec((1, IDX_HEADS, 1), lambda i, t: (i, 0, 0))] + paged((IDX_DIM, PAGE)),
            out_specs=pl.BlockSpec((1, 1, past), lambda i, t: (i, 0, 0))),
        out_shape=jax.ShapeDtypeStruct((n, 1, past), F32),
        compiler_params=_params("parallel"),
    )(pt, qib.reshape(n, IDX_HEADS, IDX_DIM), wi[:, 0:IDX_HEADS].reshape(n, IDX_HEADS, 1),
      *([jnp.swapaxes(cache_kidx, 2, 3)] * npg))

    sel = pl.pallas_call(
        functools.partial(_dsa_sample_select_kernel, n_sel=n_sel, idx_bits=idx_bits),
        out_shape=jax.ShapeDtypeStruct((past + PAGE, n), I32),
        scratch_shapes=[pltpu.VMEM((npg + 1, PAGE, n), I32), pltpu.VMEM((SUBLANES, n), I32)],
        compiler_params=pltpu.CompilerParams(vmem_limit_bytes=VMEM_LIMIT),
    )(scores.reshape(n, past).T, qib.T, kiln.astype(BF16).T, wi[:, 0:IDX_HEADS].T).T

    rows = PAGE * AT_HEADS
    sel8 = jnp.repeat(sel, AT_HEADS, axis=1).reshape(n, 1, (npg + 1) * rows)
    rep = jnp.repeat(sbias, AT_HEADS, axis=1)
    own_head = (jnp.arange(rep.shape[1]) % AT_HEADS)[None, :] == jnp.arange(AT_HEADS)[:, None]
    b8 = jnp.where(own_head, rep, NEG_BIG)
    ck4 = cache_k.reshape(cache_k.shape[0], npool, rows, AT_DH)
    cv4 = cache_v.reshape(cache_v.shape[0], npool, rows, AT_DH)
    heads = pl.BlockSpec((1, AT_HEADS, AT_DH), lambda i, t: (i, 0, 0))
    o = pl.pallas_call(
        _dsa_sample_attn_kernel,
        grid_spec=pltpu.PrefetchScalarGridSpec(
            num_scalar_prefetch=1, grid=(n,),
            in_specs=[heads, pl.BlockSpec((1, 1, sel8.shape[2]), lambda i, t: (i, 0, 0)),
                      pl.BlockSpec(b8.shape, lambda i, t: (0, 0), pipeline_mode=pl.Buffered(1)),
                      pl.BlockSpec(sbias.shape, lambda i, t: (0, 0), pipeline_mode=pl.Buffered(1)),
                      heads, heads] + paged((rows, AT_DH)) + paged((rows, AT_DH)),
            out_specs=heads),
        out_shape=jax.ShapeDtypeStruct((n, AT_HEADS, AT_DH), F32),
        compiler_params=_params("parallel"),
    )(pt, qb.reshape(n, AT_HEADS, AT_DH), sel8, b8, sbias, kbf.reshape(n, AT_HEADS, AT_DH),
      vbf.reshape(n, AT_HEADS, AT_DH), *([ck4] * npg), *([cv4] * npg))
    return mm_res_ln(o.reshape(n, d), w_out, x, g, b, alpha, n)


MLP_TM = 512


def kernel(x_prompt, x_sample, state_hgrn, cache_k, cache_v, cache_kidx, page_table, state_s5_re, state_s5_im, hg_w_in, hg_lb, hg_gnorm, hg_w_out, dsa_w_in, dsa_kln_g, dsa_kln_b, dsa_w_out, rel_bias, s5_a_re, s5_a_im, s5_log_dt, s5_b_re, s5_b_im, s5_c_re, s5_c_im, s5_d, s5_w_glu, s5_b_glu, ln_mix_g, ln_mix_b, mlp_w1, mlp_w2, ln_ffn_g, ln_ffn_b):
    n, l, d = x_prompt.shape
    nb = x_sample.shape[0]
    depth = ln_mix_g.shape[0]
    alpha = (2 * depth) ** 0.25
    past = page_table.shape[1] * PAGE
    row = lambda a: a.reshape(1, -1)
    xp = x_prompt
    xs = x_sample.reshape(nb, d)
    hg_p, hg_s = [], None
    w1_all, w2_all = mlp_w1.astype(BF16), mlp_w2.astype(BF16)
    k_p, v_p, ki_p, k_s, v_s, ki_s = [], [], [], [], [], []
    s5r_p, s5i_p, s5r_s, s5i_s = [], [], [], []
    for i in range(depth):
        j = i // N_MIXERS
        kind = i % N_MIXERS
        g, b = row(ln_mix_g[i]), row(ln_mix_b[i])
        if kind == 0:
            w_in, w_out, gn = hg_w_in[j].astype(BF16), hg_w_out[j].astype(BF16), row(hg_gnorm[j])
            xp, st_p = hgrn_prompt(xp, w_in, hg_lb, gn, w_out, g, b, alpha, i)
            xs, hg_s = hgrn_sample(xs, state_hgrn, j, hg_s, w_in, hg_lb, gn, w_out, g, b, alpha, i)
            hg_p.append(st_p)
        elif kind == 1:
            w_out = dsa_w_out[j].astype(BF16)
            tile, far, sbias = dsa_bias(rel_bias, past)
            qb, k, v, kbf, vbf, qib, kiln, kibf, wi = dsa_proj(xp.reshape(n * l, d), dsa_w_in[j],
                                                               dsa_kln_g[j], dsa_kln_b[j], MLP_TM)
            seq = lambda a: a.reshape(n, l, a.shape[1])
            xp = dsa_prompt(xp, seq(qb), seq(qib), seq(wi), seq(kibf), seq(kbf), seq(vbf), tile, far,
                            w_out, g, b, alpha)
            k_p.append(k.reshape(n, l, AT_HEADS, AT_DH))
            v_p.append(v.reshape(n, l, AT_HEADS, AT_DH))
            ki_p.append(kiln.reshape(n, l, IDX_DIM))
            qb, k, v, kbf, vbf, qib, kiln, kibf, wi = dsa_proj(xs, dsa_w_in[j], dsa_kln_g[j], dsa_kln_b[j], nb)
            xs = dsa_sample(xs, j, qb, qib, kiln, wi, kbf, vbf, cache_k, cache_v, cache_kidx, page_table,
                            sbias, w_out, g, b, alpha)
            k_s.append(k.reshape(nb, 1, AT_HEADS, AT_DH))
            v_s.append(v.reshape(nb, 1, AT_HEADS, AT_DH))
            ki_s.append(kiln.reshape(nb, 1, IDX_DIM))
        else:
            prm = s5_params(s5_a_re[j], s5_a_im[j], s5_log_dt[j], s5_b_re[j], s5_b_im[j], s5_c_re[j],
                            s5_c_im[j], s5_d[j], s5_w_glu[j], s5_b_glu[j])
            sg, sp = s5_a_re.shape[1], s5_a_re.shape[2]
            xp, hr, hi = s5_prompt(xp, prm, g, b, alpha)
            xs, hr2, hi2 = s5_sample(xs, state_s5_re[j].reshape(nb, sg * sp), state_s5_im[j].reshape(nb, sg * sp),
                                     prm, g, b, alpha)
            s5r_p.append(hr.reshape(n, sg, sp))
            s5i_p.append(hi.reshape(n, sg, sp))
            s5r_s.append(hr2.reshape(nb, sg, sp))
            s5i_s.append(hi2.reshape(nb, sg, sp))
        fg, fb = row(ln_ffn_g[i]), row(ln_ffn_b[i])
        xp = mlp_res_ln(xp.reshape(n * l, d), w1_all, w2_all, i, fg, fb, alpha, MLP_TM).reshape(n, l, d)
        xs = mlp_res_ln(xs, w1_all, w2_all, i, fg, fb, alpha, nb)
    return (xp, xs.reshape(nb, 1, d), jnp.stack(hg_p), hg_s, jnp.stack(k_p), jnp.stack(v_p),
            jnp.stack(ki_p), jnp.stack(k_s), jnp.stack(v_s), jnp.stack(ki_s), jnp.stack(s5r_p),
            jnp.stack(s5i_p), jnp.stack(s5r_s), jnp.stack(s5i_s))
```

```python
import functools
import math

import numpy as np
import jax
import jax.numpy as jnp
from jax import lax
from jax.experimental import pallas as pl
from jax.experimental.pallas import tpu as pltpu

F32 = jnp.float32
BF16 = jnp.bfloat16
I32 = jnp.int32

LN_EPS = 1e-5
RMS_EPS = 1e-6
NEG_BIG = -1e30
N_MIXERS = 3
HG_HEADS = 8
HG_DK = 128
AT_HEADS = 8
AT_DH = 128
IDX_HEADS = 8
IDX_DIM = 64
TOPK_MAX = 256
REL_BUCKETS = 32
REL_MAX_DIST = 128
S5_GROUP = 16
S5_STATE = 64
PAGE = 128

LANES = 128
SUBLANES = 8
VMEM_LIMIT = 56 * 1024 * 1024

NT = (((1,), (1,)), ((), ()))
TN = (((0,), (0,)), ((), ()))


def _params(*sem):
    return pltpu.CompilerParams(dimension_semantics=sem, vmem_limit_bytes=VMEM_LIMIT)


def _const(shape):
    nd = len(shape)
    return pl.BlockSpec(shape, lambda *_: (0,) * nd, pipeline_mode=pl.Buffered(1))


def _ln(z, g, b):
    mu = jnp.mean(z, axis=-1, keepdims=True)
    zc = z - mu
    var = jnp.mean(zc * zc, axis=-1, keepdims=True)
    return zc * lax.rsqrt(var + LN_EPS) * g + b


def _sigmoid(z):
    return 1.0 / (1.0 + jnp.exp(-z))


def _bdot(a, b):
    return jnp.dot(a, b, preferred_element_type=F32)


def _mlp_kernel(x_ref, w1_ref, w2_ref, g_ref, b_ref, o_ref, *, alpha, hc):
    x = x_ref[...]
    xb = x.astype(BF16)
    acc = jnp.zeros(x.shape, F32)
    for c in range(w1_ref.shape[2] // hc):
        h = _bdot(xb, w1_ref[0, :, c * hc:(c + 1) * hc])
        h = jnp.maximum(h, 0.0)
        acc = acc + _bdot((h * h).astype(BF16), w2_ref[0, c * hc:(c + 1) * hc, :])
    o_ref[...] = _ln(alpha * x + acc, g_ref[...], b_ref[...])


def mlp_res_ln(x, w1, w2, layer, g, b, alpha, tm):
    t, d = x.shape
    f = w1.shape[2]
    pick = lambda r, c: pl.BlockSpec((1, r, c), lambda i: (layer, 0, 0), pipeline_mode=pl.Buffered(1))
    return pl.pallas_call(
        functools.partial(_mlp_kernel, alpha=alpha, hc=1024),
        grid=(t // tm,),
        in_specs=[pl.BlockSpec((tm, d), lambda i: (i, 0)), pick(d, f), pick(f, d),
                  _const((1, d)), _const((1, d))],
        out_specs=pl.BlockSpec((tm, d), lambda i: (i, 0)),
        out_shape=jax.ShapeDtypeStruct((t, d), F32),
        compiler_params=_params("parallel"),
    )(x, w1, w2, g, b)


def _mm_kernel(x_ref, w_ref, o_ref):
    o_ref[...] = _bdot(x_ref[...].astype(BF16), w_ref[...])


def mm(x, w, tm):
    t, d = x.shape
    n = w.shape[1]
    return pl.pallas_call(
        _mm_kernel,
        grid=(t // tm,),
        in_specs=[pl.BlockSpec((tm, d), lambda i: (i, 0)), _const((d, n))],
        out_specs=pl.BlockSpec((tm, n), lambda i: (i, 0)),
        out_shape=jax.ShapeDtypeStruct((t, n), F32),
        compiler_params=_params("parallel"),
    )(x, w)


def _mm_res_ln_kernel(a_ref, w_ref, x_ref, g_ref, b_ref, o_ref, *, alpha):
    y = _bdot(a_ref[...].astype(BF16), w_ref[...])
    o_ref[...] = _ln(alpha * x_ref[...] + y, g_ref[...], b_ref[...])


def mm_res_ln(a, w, x, g, b, alpha, tm):
    t, k = a.shape
    d = w.shape[1]
    return pl.pallas_call(
        functools.partial(_mm_res_ln_kernel, alpha=alpha),
        grid=(t // tm,),
        in_specs=[pl.BlockSpec((tm, k), lambda i: (i, 0)), _const((k, d)),
                  pl.BlockSpec((tm, d), lambda i: (i, 0)), _const((1, d)), _const((1, d))],
        out_specs=pl.BlockSpec((tm, d), lambda i: (i, 0)),
        out_shape=jax.ShapeDtypeStruct((t, d), F32),
        compiler_params=_params("parallel"),
    )(a, w, x, g, b)


HG_C = 128
HG_LEVELS = 7
HG_TS = 256


def _hgrn_constants():
    c = HG_C
    t = np.arange(c)[:, None]
    u = np.arange(c)[None, :]
    m = (u <= t).astype(np.float32)
    x = t ^ u
    lv = np.where(u < t, np.floor(np.log2(np.maximum(x, 1))).astype(np.int32),
                  np.where(u == t, HG_LEVELS, -1)).astype(np.int32)
    return np.concatenate([m, m, m], axis=1), lv


def _lower_bound(lb_ref, layer):
    rows = [lb_ref[i] for i in range(lb_ref.shape[0])]
    mx = rows[0]
    for r in rows[1:]:
        mx = jnp.maximum(mx, r)
    ex = [jnp.exp(r - mx) for r in rows]
    tot = ex[0]
    for e in ex[1:]:
        tot = tot + e
    num = jnp.zeros_like(tot)
    for i in range(1, layer + 1):
        num = num + ex[i]
    return num / tot


def _split3(x):
    hi = x.astype(BF16)
    r1 = x - hi.astype(F32)
    mid = r1.astype(BF16)
    lo = (r1 - mid.astype(F32)).astype(BF16)
    return hi, mid, lo


def _hgrn_prompt_kernel(x_ref, win_ref, lb_ref, gn_ref, wout_ref, g_ref, b_ref, mst_ref, lv_ref,
                        o_ref, st_ref, og_scr, *, alpha, layer):
    c = HG_C
    dk = HG_DK

    @pl.when(pl.program_id(1) == 0)
    def _():
        st_ref[...] = jnp.zeros(st_ref.shape, F32)

    x = x_ref[0]
    d = x.shape[1]
    proj = _bdot(x.astype(BF16), win_ref[...])
    qp = proj[:, 0:d]
    fz = proj[:, d:2 * d]
    v = proj[:, 2 * d:3 * d]
    gp = proj[:, 3 * d:4 * d]
    lb = _lower_bound(lb_ref, layer)
    q = qp * _sigmoid(qp)
    sg = _sigmoid(fz)
    f = lb + (1.0 - lb) * sg
    k = (1.0 - lb) * (1.0 - sg)
    logf = jnp.log(f)
    gate = gp * _sigmoid(gp)
    lv = lv_ref[...]
    row = lax.broadcasted_iota(I32, (c, dk), 0)
    second = [((row >> l) & 1) == 1 for l in range(HG_LEVELS)]
    sub = lax.broadcasted_iota(I32, (c // SUBLANES, SUBLANES, dk), 1)

    def level_exponent(l, bh, lfh):
        half = 1 << l
        blk = 2 * half
        if l == 0:
            return jnp.where(second[0], lfh, 0.0)
        if half >= SUBLANES:
            parts = []
            for i in range(c // blk):
                r = i * blk + half - 1
                parts.append(bh[i * blk:(i + 1) * blk] - jnp.broadcast_to(bh[r:r + 1, :], (blk, dk)))
            dlt = jnp.concatenate(parts, axis=0)
        else:
            b3 = bh.reshape(c // SUBLANES, SUBLANES, dk)
            ref_row = jnp.broadcast_to(b3[:, half - 1:half, :], b3.shape)
            for i in range(1, SUBLANES // blk):
                nxt = jnp.broadcast_to(b3[:, i * blk + half - 1:i * blk + half, :], b3.shape)
                ref_row = jnp.where(sub >= i * blk, nxt, ref_row)
            dlt = (b3 - ref_row).reshape(c, dk)
        return jnp.where(second[l], dlt, -dlt)

    for sc, h in [(sc, h) for sc in range(x.shape[0] // c) for h in range(HG_HEADS)]:
        rs = slice(sc * c, (sc + 1) * c)
        hs = slice(h * dk, (h + 1) * dk)
        if h == 0:
            hi, mid, lo = _split3(logf[rs])
            bsum = _bdot(mst_ref[...], jnp.concatenate([hi, mid, lo], axis=0))
        qh = q[rs, hs]
        kh = k[rs, hs]
        vh = v[rs, hs].astype(BF16)
        bh = bsum[:, hs]
        lfh = logf[rs, hs]
        st = st_ref[0, h]
        eb = jnp.exp(bh)
        o = lax.dot_general((qh * eb).astype(BF16), st.astype(BF16), NT, preferred_element_type=F32)
        att = jnp.zeros((c, c), F32)
        for l in range(HG_LEVELS):
            e = jnp.exp(level_exponent(l, bh, lfh))
            w = (jnp.where(second[l], qh, kh) * e).astype(BF16)
            a = lax.dot_general(w, w, NT, preferred_element_type=F32)
            att = jnp.where(lv == l, a, att)
        a0 = lax.dot_general(qh.astype(BF16), kh.astype(BF16), NT, preferred_element_type=F32)
        att = jnp.where(lv == HG_LEVELS, a0, att)
        o = o + _bdot(att.astype(BF16), vh)
        kd = (kh * jnp.exp(jnp.broadcast_to(bh[c - 1:c, :], (c, dk)) - bh)).astype(BF16)
        st_ref[0, h] = eb[c - 1:c, :] * st + lax.dot_general(vh, kd, TN, preferred_element_type=F32)
        ms = jnp.mean(o * o, axis=-1, keepdims=True)
        og_scr[rs, hs] = o * lax.rsqrt(ms + RMS_EPS) * gn_ref[:, hs] * gate[rs, hs]
    y = _bdot(og_scr[...].astype(BF16), wout_ref[...])
    o_ref[0] = _ln(alpha * x + y, g_ref[...], b_ref[...])


def hgrn_prompt(x, w_in, hg_lb, gnorm, w_out, g, b, alpha, layer):
    n, l, d = x.shape
    mst, lv = _hgrn_constants()
    nl = hg_lb.shape[0]
    out, st = pl.pallas_call(
        functools.partial(_hgrn_prompt_kernel, alpha=alpha, layer=layer),
        grid=(n, l // HG_TS),
        in_specs=[pl.BlockSpec((1, HG_TS, d), lambda s, c: (s, c, 0)),
                  _const((d, 4 * d)), _const((nl, 1, d)), _const((1, d)), _const((d, d)),
                  _const((1, d)), _const((1, d)), _const(mst.shape), _const(lv.shape)],
        out_specs=[pl.BlockSpec((1, HG_TS, d), lambda s, c: (s, c, 0)),
                   pl.BlockSpec((1, HG_HEADS, HG_DK, HG_DK), lambda s, c: (s, 0, 0, 0))],
        out_shape=[jax.ShapeDtypeStruct((n, l, d), F32),
                   jax.ShapeDtypeStruct((n, HG_HEADS, HG_DK, HG_DK), F32)],
        scratch_shapes=[pltpu.VMEM((HG_TS, d), F32)],
        compiler_params=_params("parallel", "arbitrary"),
    )(x, w_in, hg_lb.reshape(nl, 1, d), gnorm, w_out, g, b,
      jnp.asarray(mst, BF16), jnp.asarray(lv))
    return out, jnp.swapaxes(st, 2, 3)


HG_SG = 4


def _hgrn_sample_kernel(fzc_ref, lbt_ref, q_ref, fz_ref, v_ref, gp_ref, lb_ref, gn_ref, s0_ref, *rest, layer):
    o_ref, s1_ref = rest[-2], rest[-1]
    dk = HG_DK
    lbt = _lower_bound(lbt_ref, layer)
    lbr = _lower_bound(lb_ref, layer)
    rid = lax.broadcasted_iota(I32, (dk, dk), 0)
    for i in range(HG_SG):
        sgc = _sigmoid(fzc_ref[i])
        fcol = lbt + (1.0 - lbt) * sgc
        kcol = ((1.0 - lbt) * (1.0 - sgc)).astype(BF16)
        qr = q_ref[i]
        q = qr * _sigmoid(qr)
        sgr = _sigmoid(fz_ref[i])
        f = lbr + (1.0 - lbr) * sgr
        k = (1.0 - lbr) * (1.0 - sgr)
        v = v_ref[i]
        gp = gp_ref[i]
        gate = gp * _sigmoid(gp)
        for h in range(HG_HEADS):
            hs = slice(h * dk, (h + 1) * dk)
            s0 = s0_ref[0, i, h]
            vh = v[:, hs]
            vrow = jnp.where(rid == h, jnp.broadcast_to(vh, (dk, dk)), 0.0).astype(BF16)
            s1_ref[0, i, h] = fcol[:, h:h + 1] * s0 + _bdot(kcol, vrow)
            qf = jnp.broadcast_to((q[:, hs] * f[:, hs]).astype(BF16), (2 * SUBLANES, dk))
            qk = jnp.sum(q[:, hs] * k[:, hs], axis=1, keepdims=True)
            o = _bdot(qf, s0.astype(BF16))[0:1, :] + qk * vh
            ms = jnp.mean(o * o, axis=-1, keepdims=True)
            o_ref[i, :, hs] = o * lax.rsqrt(ms + RMS_EPS) * gn_ref[:, hs] * gate[:, hs]


def hgrn_sample(x, states, j, new_states, w_in, hg_lb, gnorm, w_out, g, b, alpha, layer):
    n, d = x.shape
    nl = hg_lb.shape[0]
    proj = mm(x, w_in, n)
    to_cols = lambda a, lead: jnp.pad(a.reshape(lead, HG_HEADS, HG_DK).transpose(0, 2, 1),
                                      ((0, 0), (0, 0), (0, LANES - HG_HEADS)))
    row3 = lambda a: a.reshape(n, 1, d)
    cols = pl.BlockSpec((HG_SG, HG_DK, LANES), lambda i: (i, 0, 0))
    rows = pl.BlockSpec((HG_SG, 1, d), lambda i: (i, 0, 0))
    st = pl.BlockSpec((1, HG_SG, HG_HEADS, HG_DK, HG_DK), lambda i: (j, i, 0, 0, 0))
    carried = [] if new_states is None else [new_states]
    o, s1 = pl.pallas_call(
        functools.partial(_hgrn_sample_kernel, layer=layer),
        grid=(n // HG_SG,),
        in_specs=[cols, _const((nl, HG_DK, LANES)), rows, rows, rows, rows, _const((nl, 1, d)), _const((1, d)), st]
                 + [pl.BlockSpec(memory_space=pl.ANY)] * len(carried),
        out_specs=[rows, st],
        out_shape=[jax.ShapeDtypeStruct((n, 1, d), F32), jax.ShapeDtypeStruct(states.shape, F32)],
        input_output_aliases={9: 1} if carried else {},
        compiler_params=_params("parallel"),
    )(to_cols(proj[:, d:2 * d], n), to_cols(hg_lb, nl), row3(proj[:, 0:d]), row3(proj[:, d:2 * d]),
      row3(proj[:, 2 * d:3 * d]), row3(proj[:, 3 * d:4 * d]), hg_lb.reshape(nl, 1, d), gnorm, states, *carried)
    return mm_res_ln(o.reshape(n, d), w_out, x, g, b, alpha, n), s1


S5_TC = 512
S5_SLABS = 8
S5_FOLD = 4


def _cmul(ar, ai, br, bi):
    return ar * br - ai * bi, ar * bi + ai * br


def _s5_prep_kernel(are_ref, aim_ref, ldt_ref, are2_ref, aim2_ref, ldt2_ref, bre_ref, bim_ref,
                    pwre_ref, pwim_ref, ckre_ref, ckim_ref, bbre_ref, bbim_ref):
    dt = jnp.exp(ldt_ref[...])
    mag = jnp.exp(dt * are_ref[...])
    a_r = mag * jnp.cos(dt * aim_ref[...])
    a_i = mag * jnp.sin(dt * aim_ref[...])
    pr, pi_ = [a_r], [a_i]
    for _ in range(SUBLANES - 1):
        nr, ni = _cmul(pr[-1], pi_[-1], a_r, a_i)
        pr.append(nr)
        pi_.append(ni)
    pwre_ref[...] = jnp.concatenate(pr, axis=0)
    pwim_ref[...] = jnp.concatenate(pi_, axis=0)
    zero = jnp.zeros_like(a_r)
    ckre_ref[...] = jnp.concatenate([zero] * S5_FOLD + pr[0:SUBLANES - S5_FOLD], axis=0)
    ckim_ref[...] = jnp.concatenate([zero] * S5_FOLD + pi_[0:SUBLANES - S5_FOLD], axis=0)
    ar, ai = are2_ref[...], aim2_ref[...]
    dt2 = jnp.exp(ldt2_ref[...])
    mag2 = jnp.exp(dt2 * ar)
    a2r = mag2 * jnp.cos(dt2 * ai)
    a2i = mag2 * jnp.sin(dt2 * ai)
    nr = a2r - 1.0
    ni = a2i
    den = ar * ar + ai * ai
    z_re = (nr * ar + ni * ai) / den
    z_im = (ni * ar - nr * ai) / den
    br, bi = bre_ref[...], bim_ref[...]
    for j in range(S5_FOLD):
        bbre_ref[j] = z_re[:, None, :] * br - z_im[:, None, :] * bi
        bbim_ref[j] = z_re[:, None, :] * bi + z_im[:, None, :] * br
        z_re, z_im = _cmul(z_re, z_im, a2r, a2i)


def s5_prep(a_re, a_im, log_dt, b_re, b_im):
    g, p = a_re.shape
    k = b_re.shape[2]
    n = g * p
    ldt2 = jnp.broadcast_to(log_dt[:, None], (g, p))
    flat = lambda a: a.reshape(1, n)
    return pl.pallas_call(
        _s5_prep_kernel,
        out_shape=[jax.ShapeDtypeStruct((SUBLANES, n), F32), jax.ShapeDtypeStruct((SUBLANES, n), F32),
                   jax.ShapeDtypeStruct((SUBLANES, n), F32), jax.ShapeDtypeStruct((SUBLANES, n), F32),
                   jax.ShapeDtypeStruct((S5_FOLD, g, k, p), F32), jax.ShapeDtypeStruct((S5_FOLD, g, k, p), F32)],
    )(flat(a_re), flat(a_im), flat(ldt2), a_re, a_im, ldt2,
      b_re.transpose(0, 2, 1), b_im.transpose(0, 2, 1))


def _block_diag_slabs(w_gkp, rows_are_inputs):
    g, k, p = w_gkp.shape
    per = g // S5_SLABS
    w4 = w_gkp.reshape(S5_SLABS, per, k, p)
    eye = jnp.eye(per, dtype=w_gkp.dtype)
    if rows_are_inputs:
        return jnp.einsum('iakp,ab->iakbp', w4, eye).reshape(S5_SLABS, per * k, per * p)
    return jnp.einsum('iakp,ab->ibpak', w4, eye).reshape(S5_SLABS, per * p, per * k)


def _s5_tail(x, y, d_ref, wglu_ref, bglu_ref, g_ref, b_ref, alpha):
    d = x.shape[1]
    gl = jax.nn.gelu(y + d_ref[...] * x)
    u = _bdot(gl.astype(BF16), wglu_ref[...]) + bglu_ref[...]
    out = u[:, 0:d] * _sigmoid(u[:, d:2 * d])
    return _ln(alpha * x + out, g_ref[...], b_ref[...])


def _s5_prompt_kernel(x_ref, wbre_ref, wbim_ref, wcre_ref, wcim_ref, pwre_ref, pwim_ref, ckre_ref, ckim_ref,
                      d_ref, wglu_ref, bglu_ref, g_ref, b_ref,
                      o_ref, hre_ref, him_ref, cre_scr, cim_scr, y_scr, *, alpha):
    tc = x_ref.shape[1]
    ng = tc // SUBLANES
    ci = pl.program_id(1)

    @pl.when(ci == 0)
    def _():
        cre_scr[...] = jnp.zeros(cre_scr.shape, F32)
        cim_scr[...] = jnp.zeros(cim_scr.shape, F32)

    x = x_ref[0]
    rin = lax.broadcasted_iota(I32, x.shape, 0) & (S5_FOLD - 1)
    lags = [x.astype(BF16)] + [jnp.where(rin >= j, pltpu.roll(x, j, axis=0), 0.0).astype(BF16)
                               for j in range(1, S5_FOLD)]
    sw = wbre_ref.shape[2]
    for i in range(S5_SLABS):
        cs = slice(i * sw, (i + 1) * sw)
        xi = jnp.concatenate([xl[:, i * LANES:(i + 1) * LANES] for xl in lags], axis=1)
        br = _bdot(xi, wbre_ref[i]).reshape(ng, SUBLANES, sw)
        bi = _bdot(xi, wbim_ref[i]).reshape(ng, SUBLANES, sw)
        sr = jnp.broadcast_to(br[:, S5_FOLD - 1:S5_FOLD, :], br.shape)
        si = jnp.broadcast_to(bi[:, S5_FOLD - 1:S5_FOLD, :], bi.shape)
        tr, ti = _cmul(ckre_ref[:, cs], ckim_ref[:, cs], sr, si)
        br, bi = br + tr, bi + ti
        pr, pi_ = pwre_ref[:, cs], pwim_ref[:, cs]
        hpr, hpi = cre_scr[:, cs], cim_scr[:, cs]
        hr_all, hi_all = [], []
        for m in range(ng):
            tr, ti = _cmul(pr, pi_, hpr, hpi)
            hr, hi = br[m] + tr, bi[m] + ti
            hr_all.append(hr)
            hi_all.append(hi)
            hpr = jnp.broadcast_to(hr[SUBLANES - 1:SUBLANES, :], hr.shape)
            hpi = jnp.broadcast_to(hi[SUBLANES - 1:SUBLANES, :], hi.shape)
        cre_scr[:, cs] = hpr
        cim_scr[:, cs] = hpi
        hr = jnp.concatenate(hr_all, axis=0).astype(BF16)
        hi = jnp.concatenate(hi_all, axis=0).astype(BF16)
        y_scr[:, i * LANES:(i + 1) * LANES] = _bdot(hr, wcre_ref[i]) - _bdot(hi, wcim_ref[i])
    o_ref[0] = _s5_tail(x, y_scr[...], d_ref, wglu_ref, bglu_ref, g_ref, b_ref, alpha)

    @pl.when(ci == pl.num_programs(1) - 1)
    def _():
        hre_ref[0] = cre_scr[0:1, :]
        him_ref[0] = cim_scr[0:1, :]


def _s5_weight_specs(wb, wc, pw, ck, d, wglu):
    return [_const(wb.shape), _const(wb.shape), _const(wc.shape), _const(wc.shape),
            _const(pw.shape), _const(pw.shape), _const(ck.shape), _const(ck.shape),
            _const((1, d)), _const(wglu.shape), _const((1, wglu.shape[1])), _const((1, d)), _const((1, d))]


def s5_prompt(x, prm, g, b, alpha):
    n, l, d = x.shape
    wb, wc, pw, ck, wglu = prm[0], prm[2], prm[4], prm[6], prm[9]
    ns = pw.shape[1]
    out, hre, him = pl.pallas_call(
        functools.partial(_s5_prompt_kernel, alpha=alpha),
        grid=(n, l // S5_TC),
        in_specs=[pl.BlockSpec((1, S5_TC, d), lambda s, c: (s, c, 0))] + _s5_weight_specs(wb, wc, pw, ck, d, wglu),
        out_specs=[pl.BlockSpec((1, S5_TC, d), lambda s, c: (s, c, 0)),
                   pl.BlockSpec((1, 1, ns), lambda s, c: (s, 0, 0)),
                   pl.BlockSpec((1, 1, ns), lambda s, c: (s, 0, 0))],
        out_shape=[jax.ShapeDtypeStruct((n, l, d), F32),
                   jax.ShapeDtypeStruct((n, 1, ns), F32), jax.ShapeDtypeStruct((n, 1, ns), F32)],
        scratch_shapes=[pltpu.VMEM((SUBLANES, ns), F32), pltpu.VMEM((SUBLANES, ns), F32),
                        pltpu.VMEM((S5_TC, d), F32)],
        compiler_params=_params("parallel", "arbitrary"),
    )(x, *prm, g, b)
    return out, hre, him


def _s5_sample_kernel(x_ref, h0re_ref, h0im_ref, wbre_ref, wbim_ref, wcre_ref, wcim_ref, pwre_ref, pwim_ref,
                      ckre_ref, ckim_ref, d_ref, wglu_ref, bglu_ref, g_ref, b_ref,
                      o_ref, hre_ref, him_ref, y_scr, *, alpha):
    x = x_ref[...]
    xb = x.astype(BF16)
    sw = wbre_ref.shape[2]
    for i in range(S5_SLABS):
        cs = slice(i * sw, (i + 1) * sw)
        xi = xb[:, i * LANES:(i + 1) * LANES]
        tr, ti = _cmul(pwre_ref[0:1, cs], pwim_ref[0:1, cs], h0re_ref[:, cs], h0im_ref[:, cs])
        hr = _bdot(xi, wbre_ref[i, 0:LANES, :]) + tr
        hi = _bdot(xi, wbim_ref[i, 0:LANES, :]) + ti
        hre_ref[:, cs] = hr
        him_ref[:, cs] = hi
        y_scr[:, i * LANES:(i + 1) * LANES] = (_bdot(hr.astype(BF16), wcre_ref[i])
                                               - _bdot(hi.astype(BF16), wcim_ref[i]))
    o_ref[...] = _s5_tail(x, y_scr[...], d_ref, wglu_ref, bglu_ref, g_ref, b_ref, alpha)


def s5_sample(x, h0re, h0im, prm, g, b, alpha):
    n, d = x.shape
    ns = h0re.shape[1]
    return pl.pallas_call(
        functools.partial(_s5_sample_kernel, alpha=alpha),
        out_shape=[jax.ShapeDtypeStruct((n, d), F32),
                   jax.ShapeDtypeStruct((n, ns), F32), jax.ShapeDtypeStruct((n, ns), F32)],
        scratch_shapes=[pltpu.VMEM((n, d), F32)],
        compiler_params=pltpu.CompilerParams(vmem_limit_bytes=VMEM_LIMIT),
    )(x, h0re, h0im, *prm, g, b)


def s5_params(a_re, a_im, log_dt, b_re, b_im, c_re, c_im, d_skip, w_glu, b_glu):
    pwre, pwim, ckre, ckim, bbre, bbim = s5_prep(a_re, a_im, log_dt, b_re, b_im)
    fold = lambda bb: jnp.concatenate([_block_diag_slabs(bb[j], True) for j in range(S5_FOLD)], axis=1)
    return (fold(bbre).astype(BF16), fold(bbim).astype(BF16),
            _block_diag_slabs(c_re, False).astype(BF16), _block_diag_slabs(c_im, False).astype(BF16),
            pwre, pwim, ckre, ckim, d_skip.reshape(1, -1), w_glu.astype(BF16), b_glu.reshape(1, -1))


DSA_TQ = 256
DSA_SUB = 128
DSA_CK = 512
INT_MIN = -2 ** 31
INT_MAX = 2 ** 31 - 1
LOG2E = math.log2(math.e)


def _dsa_proj_kernel(x_ref, wq_ref, wk_ref, wv_ref, wqi_ref, wki_ref, wwi_ref, kg_ref, kb_ref,
                     qb_ref, k_ref, v_ref, kbf_ref, vbf_ref, qi_ref, kiln_ref, kibf_ref, wi_ref):
    xb = x_ref[...].astype(BF16)
    qb_ref[...] = (_bdot(xb, wq_ref[...]) * (AT_DH ** -0.5 * LOG2E)).astype(BF16)
    k = _bdot(xb, wk_ref[...])
    v = _bdot(xb, wv_ref[...])
    k_ref[...] = k
    v_ref[...] = v
    kbf_ref[...] = k.astype(BF16)
    vbf_ref[...] = v.astype(BF16)
    qi_ref[...] = (_bdot(xb, wqi_ref[...]) * (IDX_DIM ** -0.5)).astype(BF16)
    ki = _ln(_bdot(xb, wki_ref[...])[:, 0:IDX_DIM], kg_ref[...], kb_ref[...])
    kiln_ref[...] = ki
    kibf_ref[...] = ki.astype(BF16)
    wi_ref[...] = _bdot(xb, wwi_ref[...]) * (IDX_HEADS ** -0.5)


def dsa_proj(x, w_in, kln_g, kln_b, tm):
    t, d = x.shape
    hd = AT_HEADS * AT_DH
    qd = IDX_HEADS * IDX_DIM
    wb = w_in.astype(BF16)
    pad = lambda w: jnp.pad(w, ((0, 0), (0, LANES - w.shape[1])))
    ws = [wb[:, 0:hd], wb[:, hd:2 * hd], wb[:, 2 * hd:3 * hd], wb[:, 3 * hd:3 * hd + qd],
          pad(wb[:, 3 * hd + qd:3 * hd + qd + IDX_DIM]), pad(wb[:, 3 * hd + qd + IDX_DIM:])]
    row = lambda n, dt: (pl.BlockSpec((tm, n), lambda i: (i, 0)), jax.ShapeDtypeStruct((t, n), dt))
    outs = [row(hd, BF16), row(hd, F32), row(hd, F32), row(hd, BF16), row(hd, BF16), row(qd, BF16),
            row(IDX_DIM, F32), row(IDX_DIM, BF16), row(LANES, F32)]
    return pl.pallas_call(
        _dsa_proj_kernel,
        grid=(t // tm,),
        in_specs=[pl.BlockSpec((tm, d), lambda i: (i, 0))] + [_const(w.shape) for w in ws]
                 + [_const((1, IDX_DIM)), _const((1, IDX_DIM))],
        out_specs=[o[0] for o in outs],
        out_shape=[o[1] for o in outs],
        compiler_params=_params("parallel"),
    )(x, *ws, kln_g.reshape(1, IDX_DIM), kln_b.reshape(1, IDX_DIM))


def _t5_bucket(rel):
    n = jnp.maximum(rel, 0)
    exact = REL_BUCKETS // 2
    nf = jnp.maximum(n, exact).astype(F32)
    large = exact + (jnp.log(nf / exact) / math.log(REL_MAX_DIST / exact) * (REL_BUCKETS - exact)).astype(I32)
    large = jnp.minimum(large, REL_BUCKETS - 1)
    return jnp.where(n < exact, n, large)


def _bias_lookup(bucket, rb_ref, h):
    out = jnp.zeros(bucket.shape, F32)
    for bk in range(REL_BUCKETS):
        out = jnp.where(bucket == bk, rb_ref[bk, h], out)
    return out


def _dsa_bias_kernel(rb_ref, tile_ref, far_ref, samp_ref, *, past):
    c = lax.broadcasted_iota(I32, tile_ref.shape[1:], 0)
    i = lax.broadcasted_iota(I32, tile_ref.shape[1:], 1)
    bt = _t5_bucket(i - c + DSA_SUB)
    bs = _t5_bucket(past - lax.broadcasted_iota(I32, (1, samp_ref.shape[1]), 1))
    for h in range(AT_HEADS):
        tile_ref[h] = _bias_lookup(bt, rb_ref, h) * LOG2E
        far_ref[h] = jnp.full(far_ref.shape[1:], rb_ref[REL_BUCKETS - 1, h] * LOG2E, F32)
        samp_ref[h:h + 1, :] = _bias_lookup(bs, rb_ref, h) * LOG2E


def dsa_bias(rel_bias, past):
    return pl.pallas_call(
        functools.partial(_dsa_bias_kernel, past=past),
        in_specs=[pl.BlockSpec(memory_space=pltpu.SMEM)],
        out_shape=[jax.ShapeDtypeStruct((AT_HEADS, 2 * DSA_SUB, DSA_SUB), F32),
                   jax.ShapeDtypeStruct((AT_HEADS, 1, LANES), F32),
                   jax.ShapeDtypeStruct((AT_HEADS, past + PAGE), F32)],
    )(rel_bias)


def _order_key(s):
    bits = pltpu.bitcast(s, I32)
    return bits ^ ((bits >> 31) & INT_MAX)


def _count(key_scr, nck, pred):
    cw, nq = key_scr.shape[1], key_scr.shape[2]

    def body(c, acc):
        kk = key_scr[c]
        parts = [jnp.where(pred(kk[g * SUBLANES:(g + 1) * SUBLANES], c, g * SUBLANES), 1.0, 0.0)
                 for g in range(cw // SUBLANES)]
        while len(parts) > 1:
            parts = [parts[i] + parts[i + 1] for i in range(0, len(parts), 2)]
        return acc + parts[0]

    acc = lax.fori_loop(0, nck, body, jnp.zeros((SUBLANES, nq), F32))
    return jnp.sum(acc, axis=0, keepdims=True)


def _topk_select(key_scr, cut_scr, nck, n_sel, idx_bits, emit):
    cw, nq = key_scr.shape[1], key_scr.shape[2]
    nf = float(n_sel)
    kidx = lax.broadcasted_iota(I32, (cw, nq), 0)
    sidx = lax.broadcasted_iota(I32, (SUBLANES, nq), 0)

    cnt = _count(key_scr, nck, lambda kk, c, r: kk >= 0)
    thr = jnp.where(cnt >= nf, 0, INT_MIN).astype(I32)
    cnt_ge = jnp.where(cnt >= nf, cnt, (nck * cw).astype(F32) if hasattr(nck, "astype") else float(nck * cw))

    def bit_body(i, carry):
        thr, cnt_ge = carry
        trial = thr | jnp.left_shift(jnp.int32(1), 30 - i)
        cnt = _count(key_scr, nck, lambda kk, c, r: kk >= trial)
        return jnp.where(cnt >= nf, trial, thr), jnp.where(cnt >= nf, cnt, cnt_ge)

    thr, cnt_ge = lax.fori_loop(0, 31, bit_body, (thr, cnt_ge))

    cut_scr[...] = jnp.full(cut_scr.shape, INT_MAX, I32)

    @pl.when(jnp.max(cnt_ge) > nf)
    def _():
        need = nf - _count(key_scr, nck, lambda kk, c, r: kk > thr)

        def tie_body(i, rr):
            t = rr | jnp.left_shift(jnp.int32(1), idx_bits - 1 - i)
            below = _count(key_scr, nck,
                           lambda kk, c, r: jnp.where(kk == thr, c * cw + r + sidx, INT_MAX) < t)
            return jnp.where(below <= need - 1.0, t, rr)

        rr = lax.fori_loop(0, idx_bits, tie_body, jnp.zeros((1, nq), I32))
        cut_scr[...] = jnp.broadcast_to(jnp.where(cnt_ge > nf, rr, INT_MAX), cut_scr.shape)

    cut = cut_scr[0:1, :]

    def sel_body(c, carry):
        kk = key_scr[c]
        tie = jnp.where(c * cw + kidx <= cut, 1, 0)
        key_scr[c] = emit(c, jnp.where(kk > thr, 1, jnp.where(kk == thr, tie, 0)) > 0)
        return carry

    lax.fori_loop(0, nck, sel_body, 0)


def _dsa_prompt_kernel(x_ref, q_ref, qi_ref, wi_ref, ki_ref, k_ref, vt_ref, tile_ref, far_ref,
                       wout_ref, g_ref, b_ref, o_ref, key_scr, cut_scr, sa, sb, pb_scr, cm_scr, m_scr, l_scr, acc_scr,
                       *, alpha, n_sel, idx_bits):
    tq = DSA_TQ
    ck = DSA_CK
    sub = DSA_SUB
    qb = pl.program_id(1)
    q0 = qb * tq
    nck = (q0 + tq + ck - 1) // ck
    kloc = lax.broadcasted_iota(I32, (ck, tq), 0)
    qpos = q0 + lax.broadcasted_iota(I32, (ck, tq), 1)

    wit = wi_ref[0].T
    qis = [qi_ref[0, :, h * IDX_DIM:(h + 1) * IDX_DIM] for h in range(IDX_HEADS)]

    def score_body(c, carry):
        kic = ki_ref[0, pl.ds(pl.multiple_of(c * ck, ck), ck), :]
        s = None
        for h in range(IDX_HEADS):
            dots = lax.dot_general(kic, qis[h], NT, preferred_element_type=F32)
            t = wit[h:h + 1, :] * jnp.maximum(dots, 0.0)
            s = t if s is None else s + t
        s = jnp.where(c * ck + kloc <= qpos, s, NEG_BIG)
        key_scr[c] = _order_key(s)
        return carry

    lax.fori_loop(0, nck, score_body, 0)

    def mask_bits(c, selected):
        keep = jnp.where(c * ck + kloc <= qpos, 0.0, NEG_BIG)
        return pltpu.bitcast(jnp.where(selected, keep, NEG_BIG), I32)

    _topk_select(key_scr, cut_scr, nck, n_sel, idx_bits, mask_bits)

    nmax = key_scr.shape[0] - 1
    key_scr[nmax] = pltpu.bitcast(jnp.full((ck, tq), NEG_BIG, F32), I32)
    m_scr[...] = jnp.full(m_scr.shape, NEG_BIG, F32)
    l_scr[...] = jnp.zeros(l_scr.shape, F32)
    acc_scr[...] = jnp.zeros(acc_scr.shape, F32)
    hsl = [slice(h * AT_DH, (h + 1) * AT_DH) for h in range(AT_HEADS)]
    ones = jnp.ones((SUBLANES, ck), BF16)

    def logits(kv, mk, h):
        k0 = pl.multiple_of(kv * ck, ck)
        return (lax.dot_general(k_ref[0, pl.ds(k0, ck), hsl[h]], q_ref[0, :, hsl[h]], NT,
                                preferred_element_type=F32) + pltpu.bitcast(key_scr[mk], F32))

    def put(buf, h, val):
        ref, slot = buf
        ref[h] = val
        cm_scr[slot, h] = jnp.broadcast_to(jnp.max(val, axis=0, keepdims=True), (SUBLANES, tq))

    buf_a, buf_b = (sa, 0), (sb, 1)

    def softmax_pv(kv, h, buf, fb):
        src, slot = buf
        m = m_scr[h][0:1, :]
        mn = jnp.maximum(m, cm_scr[slot, h][0:1, :] + fb)
        shift = mn - fb
        for j in range(ck // sub):
            pb_scr[h, j * sub:(j + 1) * sub, :] = jnp.exp2(src[h, j * sub:(j + 1) * sub, :] - shift).astype(BF16)
        a = jnp.exp2(m - mn)
        pb = pb_scr[h]
        l_scr[h] = a * l_scr[h] + _bdot(ones, pb)
        acc_scr[h] = a * acc_scr[h] + _bdot(vt_ref[0, h, kv], pb)
        m_scr[h] = jnp.broadcast_to(mn, (SUBLANES, tq))

    def near_logits(kv, mk, h):
        k0 = kv * ck
        rows = []
        for j in range(ck // sub):
            cols = []
            for qh in range(tq // sub):
                delta = (q0 + qh * sub) - (k0 + j * sub)
                cols.append(jnp.where(delta == 0, tile_ref[h, sub:2 * sub, :],
                                      jnp.where(delta == sub, tile_ref[h, 0:sub, :],
                                                jnp.broadcast_to(far_ref[h], (sub, sub)))))
            rows.append(jnp.concatenate(cols, axis=1))
        return logits(kv, mk, h) + jnp.concatenate(rows, axis=0)

    c_last = nck - 1
    has_prev = nck >= 2
    kv_prev = jnp.where(has_prev, nck - 2, 0)
    mk_prev = jnp.where(has_prev, nck - 2, nmax)
    for h in range(AT_HEADS):
        put(buf_a, h, near_logits(c_last, c_last, h))
    for h in range(AT_HEADS):
        put(buf_b, h, near_logits(kv_prev, mk_prev, h))
        softmax_pv(c_last, h, buf_a, 0.0)
    for h in range(AT_HEADS):
        softmax_pv(kv_prev, h, buf_b, 0.0)

    nfar = jnp.maximum(nck - 2, 0)

    def ids(c):
        ok = c < nfar
        return jnp.where(ok, c, 0), jnp.where(ok, c, nmax)

    @pl.when(nfar > 0)
    def _():
        for h in range(AT_HEADS):
            put(buf_a, h, logits(0, 0, h))

    def pair_body(p, carry):
        kv0 = 2 * p
        kv1, mk1 = ids(2 * p + 1)
        kv2, mk2 = ids(2 * p + 2)
        for h in range(AT_HEADS):
            put(buf_b, h, logits(kv1, mk1, h))
        for h in range(AT_HEADS):
            softmax_pv(kv0, h, buf_a, far_ref[h][:, 0:1])
            put(buf_a, h, logits(kv2, mk2, h))
        for h in range(AT_HEADS):
            softmax_pv(kv1, h, buf_b, far_ref[h][:, 0:1])
        return carry

    lax.fori_loop(0, (nfar + 1) // 2, pair_body, 0)
    att = jnp.concatenate([(acc_scr[h] / l_scr[h][0:1, :]).T for h in range(AT_HEADS)], axis=1)
    y = _bdot(att.astype(BF16), wout_ref[...])
    o_ref[0] = _ln(alpha * x_ref[0] + y, g_ref[...], b_ref[...])


def dsa_prompt(x, qb, qi, wi, kibf, kbf, vbf, tile, far, w_out, g, b, alpha):
    n, l, d = x.shape
    n_sel = min(TOPK_MAX, l // 4)
    nck_max = (l + DSA_CK - 1) // DSA_CK
    idx_bits = max(1, int(nck_max * DSA_CK - 1).bit_length())
    nqb = l // DSA_TQ
    vt = vbf.reshape(n, nck_max, DSA_CK, AT_HEADS, AT_DH).transpose(0, 3, 1, 4, 2)
    tok = lambda w: pl.BlockSpec((1, DSA_TQ, w), lambda s, t: (s, t, 0))
    seq = lambda a: pl.BlockSpec((1,) + a.shape[1:], lambda s, t: (s,) + (0,) * (a.ndim - 1),
                                 pipeline_mode=pl.Buffered(1))
    return pl.pallas_call(
        functools.partial(_dsa_prompt_kernel, alpha=alpha, n_sel=n_sel, idx_bits=idx_bits),
        grid=(n, nqb),
        in_specs=[tok(d), tok(d), tok(qi.shape[2]), tok(LANES), seq(kibf), seq(kbf), seq(vt),
                  _const(tile.shape), _const(far.shape), _const(w_out.shape), _const((1, d)), _const((1, d))],
        out_specs=tok(d),
        out_shape=jax.ShapeDtypeStruct((n, l, d), F32),
        scratch_shapes=[pltpu.VMEM((nck_max + 1, DSA_CK, DSA_TQ), I32), pltpu.VMEM((SUBLANES, DSA_TQ), I32),
                        pltpu.VMEM((AT_HEADS, DSA_CK, DSA_TQ), F32), pltpu.VMEM((AT_HEADS, DSA_CK, DSA_TQ), F32),
                        pltpu.VMEM((AT_HEADS, DSA_CK, DSA_TQ), BF16), pltpu.VMEM((2, AT_HEADS, SUBLANES, DSA_TQ), F32),
                        pltpu.VMEM((AT_HEADS, SUBLANES, DSA_TQ), F32), pltpu.VMEM((AT_HEADS, SUBLANES, DSA_TQ), F32),
                        pltpu.VMEM((AT_HEADS, AT_DH, DSA_TQ), F32)],
        compiler_params=_params("parallel", "arbitrary"),
    )(x, qb, qi, wi, kibf, kbf, vt, tile, far, w_out, g, b)


def _dsa_sample_score_kernel(pt_ref, qi_ref, wi_ref, *refs):
    o_ref = refs[-1]
    for p, kidx_ref in enumerate(refs[:-1]):
        kp = kidx_ref[0, 0].astype(BF16)
        dots = _bdot(qi_ref[0], kp)
        o_ref[0, :, p * PAGE:(p + 1) * PAGE] = jnp.sum(wi_ref[0] * jnp.maximum(dots, 0.0), axis=0, keepdims=True)


def _dsa_sample_select_kernel(sc_ref, qi_ref, ki_ref, wi_ref, sel_ref, key_scr, cut_scr, *, n_sel, idx_bits):
    nb = sc_ref.shape[1]
    pw = PAGE
    npg = sc_ref.shape[0] // pw
    for c in range(npg):
        key_scr[c] = _order_key(sc_ref[c * pw:(c + 1) * pw, :])
    qi = qi_ref[...].astype(F32)
    ki = ki_ref[...].astype(F32)
    wi = wi_ref[...]
    s = jnp.zeros((1, nb), F32)
    for h in range(IDX_HEADS):
        dot = jnp.sum(qi[h * IDX_DIM:(h + 1) * IDX_DIM, :] * ki, axis=0, keepdims=True)
        s = s + wi[h:h + 1, :] * jnp.maximum(dot, 0.0)
    row = lax.broadcasted_iota(I32, (pw, nb), 0)
    key_scr[npg] = _order_key(jnp.where(row == 0, s, NEG_BIG))
    _topk_select(key_scr, cut_scr, npg + 1, n_sel, idx_bits, lambda c, selected: jnp.where(selected, 1, 0))
    for c in range(npg + 1):
        sel_ref[c * pw:(c + 1) * pw, :] = key_scr[c]


def _dsa_sample_attn_kernel(pt_ref, q_ref, sel_ref, b8_ref, sb_ref, kn_ref, vn_ref, *refs):
    npg = (len(refs) - 1) // 2
    k_refs, v_refs, o_ref = refs[:npg], refs[npg:2 * npg], refs[-1]
    rows = PAGE * AT_HEADS
    past = npg * PAGE
    q = q_ref[0]
    s = jnp.concatenate([lax.dot_general(q, kr[0, 0].astype(BF16), NT, preferred_element_type=F32)
                         for kr in k_refs], axis=1)
    ok = jnp.broadcast_to(sel_ref[0, :, 0:npg * rows], s.shape) > 0
    s = jnp.where(ok, s + b8_ref[:, 0:npg * rows], NEG_BIG)
    sn = jnp.sum(q.astype(F32) * kn_ref[0].astype(F32), axis=1, keepdims=True) + sb_ref[:, past:past + 1]
    okn = jnp.broadcast_to(sel_ref[0, :, npg * rows:npg * rows + 1], sn.shape) > 0
    sn = jnp.where(okn, sn, NEG_BIG)
    m = jnp.maximum(jnp.max(s, axis=1, keepdims=True), sn)
    p = jnp.exp2(s - m)
    pn = jnp.exp2(sn - m)
    l = jnp.sum(p, axis=1, keepdims=True) + pn
    acc = pn.astype(BF16).astype(F32) * vn_ref[0].astype(F32)
    for i, vr in enumerate(v_refs):
        acc = acc + _bdot(p[:, i * rows:(i + 1) * rows].astype(BF16), vr[0, 0].astype(BF16))
    o_ref[0] = acc / l


def dsa_sample(x, j, qb, qib, kiln, wi, kbf, vbf, cache_k, cache_v, cache_kidx, page_table, sbias,
               w_out, g, b, alpha):
    n, d = x.shape
    npg = page_table.shape[1]
    past = npg * PAGE
    n_sel = min(TOPK_MAX, (past + 1) // 4)
    idx_bits = int((npg + 1) * PAGE - 1).bit_length()
    pt = page_table.reshape(-1)
    npool = cache_k.shape[1]

    def paged(shape):
        return [pl.BlockSpec((1, 1) + shape, lambda i, t, p=p: (j, t[i * npg + p], 0, 0)) for p in range(npg)]

    scores = pl.pallas_call(
        _dsa_sample_score_kernel,
        grid_spec=pltpu.PrefetchScalarGridSpec(
            num_scalar_prefetch=1, grid=(n,),
            in_specs=[pl.BlockSpec((1, IDX_HEADS, IDX_DIM), lambda i, t: (i, 0, 0)),
                      pl.BlockSpec((1, IDX_HEADS, 1), lambda i, t: (i, 0, 0))] + paged((IDX_DIM, PAGE)),
            out_specs=pl.BlockSpec((1, 1, past), lambda i, t: (i, 0, 0))),
        out_shape=jax.ShapeDtypeStruct((n, 1, past), F32),
        compiler_params=_params("parallel"),
    )(pt, qib.reshape(n, IDX_HEADS, IDX_DIM), wi[:, 0:IDX_HEADS].reshape(n, IDX_HEADS, 1),
      *([jnp.swapaxes(cache_kidx, 2, 3)] * npg))

    sel = pl.pallas_call(
        functools.partial(_dsa_sample_select_kernel, n_sel=n_sel, idx_bits=idx_bits),
        out_shape=jax.ShapeDtypeStruct((past + PAGE, n), I32),
        scratch_shapes=[pltpu.VMEM((npg + 1, PAGE, n), I32), pltpu.VMEM((SUBLANES, n), I32)],
        compiler_params=pltpu.CompilerParams(vmem_limit_bytes=VMEM_LIMIT),
    )(scores.reshape(n, past).T, qib.T, kiln.astype(BF16).T, wi[:, 0:IDX_HEADS].T).T

    rows = PAGE * AT_HEADS
    sel8 = jnp.repeat(sel, AT_HEADS, axis=1).reshape(n, 1, (npg + 1) * rows)
    rep = jnp.repeat(sbias, AT_HEADS, axis=1)
    own_head = (jnp.arange(rep.shape[1]) % AT_HEADS)[None, :] == jnp.arange(AT_HEADS)[:, None]
    b8 = jnp.where(own_head, rep, NEG_BIG)
    ck4 = cache_k.reshape(cache_k.shape[0], npool, rows, AT_DH)
    cv4 = cache_v.reshape(cache_v.shape[0], npool, rows, AT_DH)
    heads = pl.BlockSpec((1, AT_HEADS, AT_DH), lambda i, t: (i, 0, 0))
    o = pl.pallas_call(
        _dsa_sample_attn_kernel,
        grid_spec=pltpu.PrefetchScalarGridSpec(
            num_scalar_prefetch=1, grid=(n,),
            in_specs=[heads, pl.BlockSpec((1, 1, sel8.shape[2]), lambda i, t: (i, 0, 0)),
                      pl.BlockSpec(b8.shape, lambda i, t: (0, 0), pipeline_mode=pl.Buffered(1)),
                      pl.BlockSpec(sbias.shape, lambda i, t: (0, 0), pipeline_mode=pl.Buffered(1)),
                      heads, heads] + paged((rows, AT_DH)) + paged((rows, AT_DH)),
            out_specs=heads),
        out_shape=jax.ShapeDtypeStruct((n, AT_HEADS, AT_DH), F32),
        compiler_params=_params("parallel"),
    )(pt, qb.reshape(n, AT_HEADS, AT_DH), sel8, b8, sbias, kbf.reshape(n, AT_HEADS, AT_DH),
      vbf.reshape(n, AT_HEADS, AT_DH), *([ck4] * npg), *([cv4] * npg))
    return mm_res_ln(o.reshape(n, d), w_out, x, g, b, alpha, n)


MLP_TM = 1024


def kernel(x_prompt, x_sample, state_hgrn, cache_k, cache_v, cache_kidx, page_table, state_s5_re, state_s5_im, hg_w_in, hg_lb, hg_gnorm, hg_w_out, dsa_w_in, dsa_kln_g, dsa_kln_b, dsa_w_out, rel_bias, s5_a_re, s5_a_im, s5_log_dt, s5_b_re, s5_b_im, s5_c_re, s5_c_im, s5_d, s5_w_glu, s5_b_glu, ln_mix_g, ln_mix_b, mlp_w1, mlp_w2, ln_ffn_g, ln_ffn_b):
    n, l, d = x_prompt.shape
    nb = x_sample.shape[0]
    depth = ln_mix_g.shape[0]
    alpha = (2 * depth) ** 0.25
    past = page_table.shape[1] * PAGE
    row = lambda a: a.reshape(1, -1)
    xp = x_prompt
    xs = x_sample.reshape(nb, d)
    hg_p, hg_s = [], None
    w1_all, w2_all = mlp_w1.astype(BF16), mlp_w2.astype(BF16)
    k_p, v_p, ki_p, k_s, v_s, ki_s = [], [], [], [], [], []
    s5r_p, s5i_p, s5r_s, s5i_s = [], [], [], []
    for i in range(depth):
        j = i // N_MIXERS
        kind = i % N_MIXERS
        g, b = row(ln_mix_g[i]), row(ln_mix_b[i])
        if kind == 0:
            w_in, w_out, gn = hg_w_in[j].astype(BF16), hg_w_out[j].astype(BF16), row(hg_gnorm[j])
            xp, st_p = hgrn_prompt(xp, w_in, hg_lb, gn, w_out, g, b, alpha, i)
            xs, hg_s = hgrn_sample(xs, state_hgrn, j, hg_s, w_in, hg_lb, gn, w_out, g, b, alpha, i)
            hg_p.append(st_p)
        elif kind == 1:
            w_out = dsa_w_out[j].astype(BF16)
            tile, far, sbias = dsa_bias(rel_bias, past)
            qb, k, v, kbf, vbf, qib, kiln, kibf, wi = dsa_proj(xp.reshape(n * l, d), dsa_w_in[j],
                                                               dsa_kln_g[j], dsa_kln_b[j], MLP_TM)
            seq = lambda a: a.reshape(n, l, a.shape[1])
            xp = dsa_prompt(xp, seq(qb), seq(qib), seq(wi), seq(kibf), seq(kbf), seq(vbf), tile, far,
                            w_out, g, b, alpha)
            k_p.append(k.reshape(n, l, AT_HEADS, AT_DH))
            v_p.append(v.reshape(n, l, AT_HEADS, AT_DH))
            ki_p.append(kiln.reshape(n, l, IDX_DIM))
            qb, k, v, kbf, vbf, qib, kiln, kibf, wi = dsa_proj(xs, dsa_w_in[j], dsa_kln_g[j], dsa_kln_b[j], nb)
            xs = dsa_sample(xs, j, qb, qib, kiln, wi, kbf, vbf, cache_k, cache_v, cache_kidx, page_table,
                            sbias, w_out, g, b, alpha)
            k_s.append(k.reshape(nb, 1, AT_HEADS, AT_DH))
            v_s.append(v.reshape(nb, 1, AT_HEADS, AT_DH))
            ki_s.append(kiln.reshape(nb, 1, IDX_DIM))
        else:
            prm = s5_params(s5_a_re[j], s5_a_im[j], s5_log_dt[j], s5_b_re[j], s5_b_im[j], s5_c_re[j],
                            s5_c_im[j], s5_d[j], s5_w_glu[j], s5_b_glu[j])
            sg, sp = s5_a_re.shape[1], s5_a_re.shape[2]
            xp, hr, hi = s5_prompt(xp, prm, g, b, alpha)
            xs, hr2, hi2 = s5_sample(xs, state_s5_re[j].reshape(nb, sg * sp), state_s5_im[j].reshape(nb, sg * sp),
                                     prm, g, b, alpha)
            s5r_p.append(hr.reshape(n, sg, sp))
            s5i_p.append(hi.reshape(n, sg, sp))
            s5r_s.append(hr2.reshape(nb, sg, sp))
            s5i_s.append(hi2.reshape(nb, sg, sp))
        fg, fb = row(ln_ffn_g[i]), row(ln_ffn_b[i])
        xp = mlp_res_ln(xp.reshape(n * l, d), w1_all, w2_all, i, fg, fb, alpha, MLP_TM).reshape(n, l, d)
        xs = mlp_res_ln(xs, w1_all, w2_all, i, fg, fb, alpha, nb)
    return (xp, xs.reshape(nb, 1, d), jnp.stack(hg_p), hg_s, jnp.stack(k_p), jnp.stack(v_p),
            jnp.stack(ki_p), jnp.stack(k_s), jnp.stack(v_s), jnp.stack(ki_s), jnp.stack(s5r_p),
            jnp.stack(s5i_p), jnp.stack(s5r_s), jnp.stack(s5i_s))
```
